```python
import math
import jax, jax.numpy as jnp
from jax import lax
import numpy as np

D_MODEL = 2048
BATCH = 4
SEQ = 4096
DEPTH = 4

BRANCH_D = D_MODEL // 2
NORM_EPS = 1e-6

SSM_D_INNER = BRANCH_D
SSM_HEAD_DIM = 64
SSM_HEADS = SSM_D_INNER // SSM_HEAD_DIM
SSM_GROUPS = 4
SSM_STATE = 128
SSM_CONV = 4
SSD_CHUNK = 128
SSM_CONV_DIM = SSM_D_INNER + 2 * SSM_GROUPS * SSM_STATE
A_COLS = SSM_D_INNER + SSM_CONV_DIM + SSM_HEADS

RWKV_D = BRANCH_D
RWKV_HEAD_DIM = 64
RWKV_HEADS = RWKV_D // RWKV_HEAD_DIM
RWKV_DECAY_LORA = 64
RWKV_A_LORA = 64
RWKV_SHIFT_DIM = 3 * RWKV_D + RWKV_DECAY_LORA + RWKV_A_LORA
B_COLS = RWKV_SHIFT_DIM + RWKV_D
RWKV_GN_EPS = 64e-5

ATT_D = BRANCH_D
ATT_HEAD_DIM = 64
ATT_HEADS = ATT_D // ATT_HEAD_DIM
ATT_KV_HEADS = 4
ATT_KV_D = ATT_KV_HEADS * ATT_HEAD_DIM
IDX_HEADS = 16
IDX_DIM = 64
TOPK_MAX = 256
Q_BLOCK = 128
ROPE_THETA = 10000.0
C_COLS = ATT_D + 2 * ATT_KV_D + ATT_D + IDX_HEADS * IDX_DIM + IDX_DIM + IDX_HEADS

N_BRANCHES = 3
GATE_COLS = N_BRANCHES * D_MODEL
IN_COLS = A_COLS + B_COLS + C_COLS + GATE_COLS

kernel_name = "hybrid_ssd_rwkv7_dsa_gated_block"


def _split(y, widths):
    offs = np.cumsum(widths)[:-1].tolist()
    return jnp.split(y, offs, axis=-1)


def rmsnorm(x, g):
    xf = x.astype(jnp.float32)
    y = xf * lax.rsqrt(jnp.mean(xf * xf, axis=-1, keepdims=True) + NORM_EPS)
    return (y * g.astype(jnp.float32)).astype(x.dtype)


def layernorm(x, w, b):
    xf = x.astype(jnp.float32)
    mu = jnp.mean(xf, axis=-1, keepdims=True)
    var = jnp.mean((xf - mu) ** 2, axis=-1, keepdims=True)
    return ((xf - mu) * lax.rsqrt(var + NORM_EPS) * w + b).astype(x.dtype)


def rope(x, positions):
    d = x.shape[-1]
    inv = ROPE_THETA ** (-(jnp.arange(d // 2, dtype=jnp.float32) * 2.0 / d))
    ang = positions.astype(jnp.float32)[..., None] * inv
    cos, sin = jnp.cos(ang)[:, :, None, :], jnp.sin(ang)[:, :, None, :]
    xf = x.astype(jnp.float32)
    x1, x2 = xf[..., : d // 2], xf[..., d // 2:]
    return jnp.concatenate([x1 * cos - x2 * sin, x2 * cos + x1 * sin], axis=-1).astype(x.dtype)


def causal_dwconv(x, w, b):
    K, C = w.shape
    y = lax.conv_general_dilated(x, w[:, None, :], window_strides=(1,), padding=[(K - 1, 0)],
                                 dimension_numbers=('NWC', 'WIO', 'NWC'), feature_group_count=C)
    return y + b


def ssd_chunked(xh, dt, a_log, bm, cm):
    f32 = jnp.float32
    Bsz, S, H, P = xh.shape
    G, N = bm.shape[-2:]
    R, Q = H // G, SSD_CHUNK
    nc = S // Q
    adt = dt * (-jnp.exp(a_log.astype(f32)))
    X = (xh.astype(f32) * dt[..., None]).reshape(Bsz, nc, Q, G, R, P)
    Bc = bm.astype(f32).reshape(Bsz, nc, Q, G, N)
    Cc = cm.astype(f32).reshape(Bsz, nc, Q, G, N)
    a_cs = jnp.cumsum(adt.reshape(Bsz, nc, Q, G, R), axis=2)
    causal = jnp.tril(jnp.ones((Q, Q), dtype=bool))[:, :, None, None]
    seg = a_cs[:, :, :, None] - a_cs[:, :, None, :]
    lmat = jnp.exp(jnp.where(causal, seg, -jnp.inf))
    cb = jnp.einsum('bclgn,bcsgn->bclsg', Cc, Bc)
    y_diag = jnp.einsum('bclsg,bclsgr,bcsgrp->bclgrp', cb, lmat, X)
    decay_states = jnp.exp(a_cs[:, :, -1:] - a_cs)
    states = jnp.einsum('bclgn,bclgr,bclgrp->bcgrpn', Bc, decay_states, X)
    chunk_decay = jnp.exp(a_cs[:, :, -1])

    def step(h, inp):
        st, dec = inp
        return h * dec[..., None, None] + st, h

    init = jnp.zeros((Bsz, G, R, P, N), f32)
    _, h_start = lax.scan(step, init, (jnp.moveaxis(states, 1, 0), jnp.moveaxis(chunk_decay, 1, 0)))
    h_start = jnp.moveaxis(h_start, 0, 1)
    y_off = jnp.einsum('bclgn,bcgrpn,bclgr->bclgrp', Cc, h_start, jnp.exp(a_cs))
    return (y_diag + y_off).reshape(Bsz, S, H, P)


def mamba2_branch(p, conv_w, conv_b, dt_bias, a_log, d_skip, norm_w):
    Bsz, S, _ = p.shape
    f32 = jnp.float32
    z, xbc, dt = _split(p, [SSM_D_INNER, SSM_CONV_DIM, SSM_HEADS])
    xbc = jax.nn.silu(causal_dwconv(xbc, conv_w, conv_b))
    xs, bm, cm = _split(xbc, [SSM_D_INNER, SSM_GROUPS * SSM_STATE, SSM_GROUPS * SSM_STATE])
    xh = xs.reshape(Bsz, S, SSM_HEADS, SSM_HEAD_DIM)
    dt = jax.nn.softplus(dt.astype(f32) + dt_bias.astype(f32))
    y = ssd_chunked(xh, dt, a_log, bm.reshape(Bsz, S, SSM_GROUPS, SSM_STATE),
                    cm.reshape(Bsz, S, SSM_GROUPS, SSM_STATE))
    y = y + d_skip.astype(f32)[:, None] * xh.astype(f32)
    yz = y.reshape(Bsz, S, SSM_D_INNER) * jax.nn.silu(z.astype(f32))
    yg = yz.reshape(Bsz, S, SSM_GROUPS, -1)
    yg = yg * lax.rsqrt(jnp.mean(yg * yg, axis=-1, keepdims=True) + NORM_EPS)
    return (yg.reshape(Bsz, S, SSM_D_INNER) * norm_w).astype(p.dtype)


def rwkv7_branch(p, mu, w0, w2, a0, a2, k_k, k_a, r_k, ln_w, ln_b):
    Bsz, S, _ = p.shape
    f32 = jnp.float32
    H, N = RWKV_HEADS, RWKV_HEAD_DIM
    shift_in, gate = _split(p, [RWKV_SHIFT_DIM, RWKV_D])
    prev = jnp.pad(shift_in, ((0, 0), (1, 0), (0, 0)))[:, :S]
    mixed = shift_in + (prev - shift_in) * mu
    r, wd, k, v, ad = _split(mixed, [RWKV_D, RWKV_DECAY_LORA, RWKV_D, RWKV_D, RWKV_A_LORA])
    w = -jax.nn.softplus(-(w0 + jnp.tanh(wd) @ w2).astype(f32)) - 0.5
    decay = jnp.exp(-jnp.exp(w))
    a = jax.nn.sigmoid((a0 + ad @ a2).astype(f32))
    heads = lambda t: t.astype(f32).reshape(Bsz, S, H, N)
    r, k, v, decay, a = heads(r), heads(k), heads(v), heads(decay), heads(a)
    kk = k * k_k.astype(f32).reshape(H, N)
    kk = kk / jnp.maximum(jnp.sqrt(jnp.sum(kk * kk, axis=-1, keepdims=True)), 1e-12)
    k = k * (1.0 + (a - 1.0) * k_a.astype(f32).reshape(H, N))
    b = kk * a

    def step(state, inp):
        r_t, w_t, k_t, v_t, kk_t, b_t = inp
        sa = -jnp.einsum('bhvk,bhk->bhv', state, kk_t)
        state = state * w_t[:, :, None, :] + sa[..., None] * b_t[:, :, None, :] + v_t[..., None] * k_t[:, :, None, :]
        return state, jnp.einsum('bhvk,bhk->bhv', state, r_t)

    tm = lambda t: jnp.moveaxis(t, 1, 0)
    init = jnp.zeros((Bsz, H, N, N), f32)
    _, ys = lax.scan(step, init, (tm(r), tm(decay), tm(k), tm(v), tm(kk), tm(b)))
    y = jnp.moveaxis(ys, 0, 1)
    mean = jnp.mean(y, axis=-1, keepdims=True)
    var = jnp.mean((y - mean) ** 2, axis=-1, keepdims=True)
    y = ((y - mean) * lax.rsqrt(var + RWKV_GN_EPS)).reshape(Bsz, S, RWKV_D) * ln_w + ln_b
    bonus = jnp.sum(r * k * r_k.astype(f32), axis=-1, keepdims=True) * v
    y = y + bonus.reshape(Bsz, S, RWKV_D)
    return (y * jax.nn.silu(gate.astype(f32))).astype(p.dtype)


def dsa_sparse_attention(q, k, v, iq, ik, iw, top_k):
    Bsz, S, H, D = q.shape
    KVH = k.shape[2]
    R = H // KVH
    nb = S // Q_BLOCK
    key_pos = jnp.arange(S)
    bidx = jnp.arange(Bsz)[:, None, None]

    def block(args):
        qb, iqb, iwb, qpos = args
        causal = key_pos[None, :] <= qpos[:, None]
        logits_idx = jnp.einsum('bqhd,bsd->bqhs', iqb, ik).astype(jnp.float32)
        score = jnp.einsum('bqh,bqhs->bqs', iwb.astype(jnp.float32), jax.nn.relu(logits_idx))
        score = jnp.where(causal[None], score, -jnp.inf)
        _, sel = lax.top_k(score, top_k)
        valid = sel <= qpos[None, :, None]
        ks = k[bidx, sel]
        vs = v[bidx, sel]
        qg = qb.reshape(Bsz, Q_BLOCK, KVH, R, D)
        logits = jnp.einsum('bqgrd,bqkgd->bqgrk', qg, ks).astype(jnp.float32) * (D ** -0.5)
        logits = jnp.where(valid[:, :, None, None, :], logits, -jnp.inf)
        prob = jax.nn.softmax(logits, axis=-1)
        o = jnp.einsum('bqgrk,bqkgd->bqgrd', prob.astype(vs.dtype), vs)
        return o.reshape(Bsz, Q_BLOCK, H, D)

    blk = lambda t: jnp.moveaxis(t.reshape((Bsz, nb, Q_BLOCK) + t.shape[2:]), 1, 0)
    out = lax.map(block, (blk(q), blk(iq), blk(iw), jnp.arange(S).reshape(nb, Q_BLOCK)))
    return jnp.moveaxis(out, 0, 1).reshape(Bsz, S, H, D)


def dsa_branch(p, positions, k_norm_w, k_norm_b, top_k):
    Bsz, S, _ = p.shape
    q, k, v, gate, iq, ik, iw = _split(p, [ATT_D, ATT_KV_D, ATT_KV_D, ATT_D, IDX_HEADS * IDX_DIM, IDX_DIM, IDX_HEADS])
    q = rope(q.reshape(Bsz, S, ATT_HEADS, ATT_HEAD_DIM), positions)
    k = rope(k.reshape(Bsz, S, ATT_KV_HEADS, ATT_HEAD_DIM), positions)
    v = v.reshape(Bsz, S, ATT_KV_HEADS, ATT_HEAD_DIM)
    iq = rope(iq.reshape(Bsz, S, IDX_HEADS, IDX_DIM), positions)
    ik = rope(layernorm(ik, k_norm_w, k_norm_b)[:, :, None, :], positions)[:, :, 0]
    iw = iw * (IDX_HEADS ** -0.5 * IDX_DIM ** -0.5)
    o = dsa_sparse_attention(q, k, v, iq, ik, iw, top_k)
    return (o.reshape(Bsz, S, ATT_D).astype(jnp.float32) * jax.nn.silu(gate.astype(jnp.float32))).astype(p.dtype)


def setup_inputs(seed: int = 0) -> dict:
    key = jax.random.key(seed)
    ks = jax.random.split(key, 32)
    f32 = jnp.float32
    L, D = DEPTH, D_MODEL
    nrm = lambda k, shape, scale: jax.random.normal(k, shape, f32) * scale
    dt0 = jnp.exp(jax.random.uniform(ks[9], (L, SSM_HEADS), f32, math.log(1e-3), math.log(1e-1)))
    return {
        "x": nrm(ks[0], (BATCH, SEQ, D), 1.0),
        "positions": jnp.broadcast_to(jnp.arange(SEQ, dtype=jnp.int32)[None, :], (BATCH, SEQ)),
        "pre_norm": 1.0 + nrm(ks[1], (L, D), 0.05),
        "post_norm": 1.0 + nrm(ks[2], (L, D), 0.05),
        "w_in": nrm(ks[3], (L, D, IN_COLS), D ** -0.5),
        "b_gate": nrm(ks[4], (L, GATE_COLS), 0.1),
        "ssm_conv_w": nrm(ks[5], (L, SSM_CONV, SSM_CONV_DIM), SSM_CONV ** -0.5),
        "ssm_conv_b": nrm(ks[6], (L, SSM_CONV_DIM), 0.02),
        "ssm_dt_bias": dt0 + jnp.log(-jnp.expm1(-dt0)),
        "ssm_a_log": jnp.log(jax.random.uniform(ks[10], (L, SSM_HEADS), f32, 1.0, 16.0)),
        "ssm_d": 1.0 + nrm(ks[11], (L, SSM_HEADS), 0.1),
        "ssm_norm": 1.0 + nrm(ks[12], (L, SSM_D_INNER), 0.05),
        "rwkv_mu": jax.random.uniform(ks[13], (L, RWKV_SHIFT_DIM), f32, 0.0, 1.0),
        "rwkv_w0": jax.random.uniform(ks[14], (L, RWKV_D), f32, -6.0, -1.0),
        "rwkv_w2": nrm(ks[15], (L, RWKV_DECAY_LORA, RWKV_D), 0.1 * RWKV_DECAY_LORA ** -0.5),
        "rwkv_a0": nrm(ks[16], (L, RWKV_D), 0.1),
        "rwkv_a2": nrm(ks[17], (L, RWKV_A_LORA, RWKV_D), 0.1 * RWKV_A_LORA ** -0.5),
        "rwkv_k_k": 0.85 + nrm(ks[18], (L, RWKV_D), 0.05),
        "rwkv_k_a": 1.0 + nrm(ks[19], (L, RWKV_D), 0.05),
        "rwkv_r_k": nrm(ks[20], (L, RWKV_HEADS, RWKV_HEAD_DIM), 0.1),
        "rwkv_ln_w": 1.0 + nrm(ks[21], (L, RWKV_D), 0.05),
        "rwkv_ln_b": nrm(ks[22], (L, RWKV_D), 0.02),
        "idx_k_norm_w": 1.0 + nrm(ks[23], (L, IDX_DIM), 0.05),
        "idx_k_norm_b": nrm(ks[24], (L, IDX_DIM), 0.02),
        "w_branch_a": nrm(ks[25], (L, SSM_D_INNER, D), SSM_D_INNER ** -0.5),
        "w_branch_b": nrm(ks[26], (L, RWKV_D, D), RWKV_D ** -0.5),
        "w_branch_c": nrm(ks[27], (L, ATT_D, D), ATT_D ** -0.5),
        "w_out": nrm(ks[28], (L, D, D), D ** -0.5),
    }


def reference(x, positions, pre_norm, post_norm, w_in, b_gate, ssm_conv_w, ssm_conv_b, ssm_dt_bias,
              ssm_a_log, ssm_d, ssm_norm, rwkv_mu, rwkv_w0, rwkv_w2, rwkv_a0, rwkv_a2, rwkv_k_k, rwkv_k_a,
              rwkv_r_k, rwkv_ln_w, rwkv_ln_b, idx_k_norm_w, idx_k_norm_b, w_branch_a, w_branch_b,
              w_branch_c, w_out):
    Bsz, S, D = x.shape
    top_k = min(TOPK_MAX, S // 4)
    for i in range(DEPTH):
        h = rmsnorm(x, pre_norm[i])
        proj = h @ w_in[i]
        pa, pb, pc, pg = _split(proj, [A_COLS, B_COLS, C_COLS, GATE_COLS])
        y_a = mamba2_branch(pa, ssm_conv_w[i], ssm_conv_b[i], ssm_dt_bias[i], ssm_a_log[i], ssm_d[i], ssm_norm[i])
        y_b = rwkv7_branch(pb, rwkv_mu[i], rwkv_w0[i], rwkv_w2[i], rwkv_a0[i], rwkv_a2[i], rwkv_k_k[i],
                           rwkv_k_a[i], rwkv_r_k[i], rwkv_ln_w[i], rwkv_ln_b[i])
        y_c = dsa_branch(pc, positions, idx_k_norm_w[i], idx_k_norm_b[i], top_k)
        gates = jax.nn.sigmoid(pg + b_gate[i]).reshape(Bsz, S, N_BRANCHES, D)
        merged = (gates[:, :, 0] * (y_a @ w_branch_a[i])
                  + gates[:, :, 1] * (y_b @ w_branch_b[i])
                  + gates[:, :, 2] * (y_c @ w_branch_c[i]))
        x = x + rmsnorm(merged @ w_out[i], post_norm[i])
    return x
```

```python
import functools
import math

import numpy as np
import jax
import jax.numpy as jnp
from jax import lax
from jax.experimental import pallas as pl
from jax.experimental.pallas import tpu as pltpu

F32 = jnp.float32
BF16 = jnp.bfloat16
I32 = jnp.int32
HI = lax.Precision.HIGHEST

D_MODEL = 2048
BRANCH_D = D_MODEL // 2
NORM_EPS = 1e-6
HEAD_DIM = 64
N_HEADS = BRANCH_D // HEAD_DIM

SSM_GROUPS = 4
SSM_STATE = 128
SSM_CONV = 4
SSD_CHUNK = 128
SSM_CONV_DIM = BRANCH_D + 2 * SSM_GROUPS * SSM_STATE
A_COLS = BRANCH_D + SSM_CONV_DIM + N_HEADS

RWKV_LORA = 64
RWKV_SHIFT_DIM = 3 * BRANCH_D + 2 * RWKV_LORA
B_COLS = RWKV_SHIFT_DIM + BRANCH_D
RWKV_GN_EPS = 64e-5
RWKV_CHUNK = 64

ATT_KV_HEADS = 4
ATT_KV_D = ATT_KV_HEADS * HEAD_DIM
IDX_HEADS = 16
IDX_DIM = 64
TOPK_MAX = 256
Q_BLOCK = 128
ROPE_THETA = 10000.0
C_COLS = BRANCH_D + 2 * ATT_KV_D + BRANCH_D + IDX_HEADS * IDX_DIM + IDX_DIM + IDX_HEADS
GATE_COLS = 3 * D_MODEL

LANES = 128
SUBLANES = 8
VMEM_LIMIT = 56 * 1024 * 1024

OFF_XBC = 0
OFF_Z = 2048
OFF_RB = 3072
OFF_KB = 4096
OFF_VB = 5120
OFF_GB = 6144
OFF_QC = 7168
OFF_GC = 8192
OFF_IQ = 9216
OFF_GATES = 10240
OFF_KC = 16384
OFF_VC = 16640
OFF_DT = 16896
OFF_LORA = 17024
OFF_IDX = 17152
NP_COLS = 17408

INT_MIN = -2 ** 31


def _cparams(sem):
    return pltpu.CompilerParams(dimension_semantics=sem, vmem_limit_bytes=VMEM_LIMIT)


def _mm(a, b):
    return jnp.dot(a.astype(BF16), b.astype(BF16), preferred_element_type=F32)


def _mm_nt(a, b):
    return lax.dot_general(a.astype(BF16), b.astype(BF16), (((1,), (1,)), ((), ())),
                           preferred_element_type=F32)


def _mm_tn(a, b):
    return lax.dot_general(a.astype(BF16), b.astype(BF16), (((0,), (0,)), ((), ())),
                           preferred_element_type=F32)


def _mm_hi(a, b):
    return jnp.dot(a, b, preferred_element_type=F32, precision=HI)


def _head_expand():
    r = lax.broadcasted_iota(I32, (LANES, BRANCH_D), 0)
    c = lax.broadcasted_iota(I32, (LANES, BRANCH_D), 1)
    return jnp.where((c >> 6) == r, 1.0, 0.0).astype(F32)


def _head_reduce():
    r = lax.broadcasted_iota(I32, (BRANCH_D, LANES), 0)
    c = lax.broadcasted_iota(I32, (BRANCH_D, LANES), 1)
    return jnp.where((r >> 6) == c, 1.0, 0.0).astype(F32)


def _tril(n, strict=False):
    r = lax.broadcasted_iota(I32, (n, n), 0)
    c = lax.broadcasted_iota(I32, (n, n), 1)
    return (c < r) if strict else (c <= r)


def _softplus(x):
    return jnp.maximum(x, 0.0) + jnp.log1p(jnp.exp(-jnp.abs(x)))


def _sigmoid(x):
    return 1.0 / (1.0 + jnp.exp(-x))


def _shift_rows(x, carry, s):
    rolled = pltpu.roll(x, s, 0)
    rows = lax.broadcasted_iota(I32, (SUBLANES, x.shape[1]), 0)
    top = jnp.where(rows < s, pltpu.roll(carry, s, 0), rolled[0:SUBLANES])
    return jnp.concatenate([top, rolled[SUBLANES:]], axis=0)


def _norm_matmul_kernel(x_ref, g_ref, w_ref, o_ref, xn_ref):
    @pl.when(pl.program_id(1) == 0)
    def _():
        x = x_ref[...]
        ms = jnp.mean(x * x, axis=-1, keepdims=True)
        xn_ref[...] = (x * lax.rsqrt(ms + NORM_EPS) * g_ref[...]).astype(BF16)

    o_ref[...] = jnp.dot(xn_ref[...], w_ref[...], preferred_element_type=F32)


def _norm_matmul(x, g, w):
    T, D = x.shape
    Np = w.shape[1]
    tm = min(1024, T)
    tn = 1024
    return pl.pallas_call(
        _norm_matmul_kernel,
        grid=(T // tm, Np // tn),
        in_specs=[pl.BlockSpec((tm, D), lambda i, j: (i, 0)),
                  pl.BlockSpec((1, D), lambda i, j: (0, 0)),
                  pl.BlockSpec((D, tn), lambda i, j: (0, j))],
        out_specs=pl.BlockSpec((tm, tn), lambda i, j: (i, j)),
        out_shape=jax.ShapeDtypeStruct((T, Np), F32),
        scratch_shapes=[pltpu.VMEM((tm, D), BF16)],
        compiler_params=_cparams(("parallel", "arbitrary")),
        name="norm_in_proj",
    )(x, g, w)


def _ssd_kernel(xbc_ref, z_ref, dt_ref, cw_ref, cb_ref, dtb_ref, alog_ref, dskip_ref, nw_ref,
                o_ref, carry_ref, h_ref, y_ref):
    Q = SSD_CHUNK
    GN = SSM_GROUPS * SSM_STATE
    GP = BRANCH_D // SSM_GROUPS

    @pl.when(pl.program_id(1) == 0)
    def _():
        carry_ref[...] = jnp.zeros_like(carry_ref)
        h_ref[...] = jnp.zeros_like(h_ref)

    x = xbc_ref[...]
    carry = carry_ref[...]
    cw = cw_ref[...]
    acc = x * cw[SSM_CONV - 1:SSM_CONV] + cb_ref[...]
    for s in range(1, SSM_CONV):
        acc = acc + _shift_rows(x, carry, s) * cw[SSM_CONV - 1 - s:SSM_CONV - s]
    carry_ref[...] = x[Q - SUBLANES:Q]
    xc = acc * _sigmoid(acc)
    xs = xc[:, :BRANCH_D]
    bm = xc[:, BRANCH_D:BRANCH_D + GN]
    cm = xc[:, BRANCH_D + GN:]

    dt = _softplus(dt_ref[...] + dtb_ref[...])
    adt = dt * (-jnp.exp(alog_ref[...]))
    a_cs = _mm_hi(_tril(Q).astype(F32), adt)
    a_cs_t = a_cs.T
    a_last = a_cs[Q - 1:Q]
    ex = _head_expand()
    dt_e = _mm_hi(dt, ex)
    ecs_e = _mm_hi(jnp.exp(a_cs), ex)
    ds_e = _mm_hi(jnp.exp(a_last - a_cs), ex)
    cd_e = _mm_hi(jnp.exp(a_last), ex)
    xd = xs * dt_e
    xds = xd * ds_e
    causal = _tril(Q)

    for g in range(SSM_GROUPS):
        bm_g = bm[:, g * SSM_STATE:(g + 1) * SSM_STATE]
        cm_g = cm[:, g * SSM_STATE:(g + 1) * SSM_STATE]
        cb = _mm_nt(cm_g, bm_g)
        for r in range(N_HEADS // SSM_GROUPS):
            h = g * (N_HEADS // SSM_GROUPS) + r
            seg = a_cs[:, h:h + 1] - a_cs_t[h:h + 1, :]
            lmat = jnp.exp(jnp.where(causal, seg, -1e30))
            y_ref[:, h * HEAD_DIM:(h + 1) * HEAD_DIM] = _mm(cb * lmat, xd[:, h * HEAD_DIM:(h + 1) * HEAD_DIM])
        hg = h_ref[g]
        y_off = _mm(cm_g, hg) * ecs_e[:, g * GP:(g + 1) * GP]
        y_ref[:, g * GP:(g + 1) * GP] += y_off
        h_ref[g] = hg * cd_e[:, g * GP:(g + 1) * GP] + _mm_tn(bm_g, xds[:, g * GP:(g + 1) * GP])

    y = y_ref[...] + dskip_ref[...] * xs
    z = z_ref[...]
    yz = y * (z * _sigmoid(z))
    nw = nw_ref[...]
    for g in range(SSM_GROUPS):
        seg = yz[:, g * GP:(g + 1) * GP]
        ms = jnp.mean(seg * seg, axis=-1, keepdims=True)
        o_ref[:, g * GP:(g + 1) * GP] = (seg * lax.rsqrt(ms + NORM_EPS) * nw[:, g * GP:(g + 1) * GP]).astype(BF16)


def _pad_lanes(v, n=LANES):
    return jnp.zeros((1, n), F32).at[0, :v.shape[0]].set(v.astype(F32))


def _ssd_branch(proj, B, S, conv_w, conv_b, dt_bias, a_log, d_skip, norm_w):
    Q = SSD_CHUNK
    nc = S // Q
    row = lambda b, c: b * nc + c
    full = lambda shape: pl.BlockSpec(shape, lambda b, c: (0, 0))
    return pl.pallas_call(
        _ssd_kernel,
        grid=(B, nc),
        in_specs=[pl.BlockSpec((Q, SSM_CONV_DIM), lambda b, c: (row(b, c), OFF_XBC // SSM_CONV_DIM)),
                  pl.BlockSpec((Q, BRANCH_D), lambda b, c: (row(b, c), OFF_Z // BRANCH_D)),
                  pl.BlockSpec((Q, LANES), lambda b, c: (row(b, c), OFF_DT // LANES)),
                  full((SSM_CONV, SSM_CONV_DIM)), full((1, SSM_CONV_DIM)), full((1, LANES)), full((1, LANES)),
                  full((1, BRANCH_D)), full((1, BRANCH_D))],
        out_specs=pl.BlockSpec((Q, BRANCH_D), lambda b, c: (row(b, c), 0)),
        out_shape=jax.ShapeDtypeStruct((B * S, BRANCH_D), BF16),
        scratch_shapes=[pltpu.VMEM((SUBLANES, SSM_CONV_DIM), F32),
                        pltpu.VMEM((SSM_GROUPS, SSM_STATE, BRANCH_D // SSM_GROUPS), F32),
                        pltpu.VMEM((Q, BRANCH_D), F32)],
        compiler_params=_cparams(("parallel", "arbitrary")),
        name="ssd_branch",
    )(proj, proj, proj, conv_w, conv_b[None, :], _pad_lanes(dt_bias), _pad_lanes(a_log),
      jnp.repeat(d_skip, HEAD_DIM)[None, :], norm_w[None, :])


def _mix(x, carry_ref, mu):
    rolled = pltpu.roll(x, 1, 0)
    rows = lax.broadcasted_iota(I32, x.shape, 0)
    prev = jnp.where(rows == 0, carry_ref[SUBLANES - 1:SUBLANES], rolled)
    carry_ref[...] = x[x.shape[0] - SUBLANES:]
    return x + (prev - x) * mu


def _rwkv_kernel(r_ref, k_ref, v_ref, g_ref, lo_ref, mu_r_ref, mu_k_ref, mu_v_ref, mu_lo_ref,
                 w0_ref, w2_ref, a0_ref, a2_ref, kk_ref, ka_ref, rk_ref, lnw_ref, lnb_ref,
                 o_ref, cr_ref, ck_ref, cv_ref, clo_ref, s_ref, y_ref):
    C = RWKV_CHUNK
    mmc = _mm_hi

    @pl.when(pl.program_id(1) == 0)
    def _():
        for ref in (cr_ref, ck_ref, cv_ref, clo_ref, s_ref):
            ref[...] = jnp.zeros_like(ref)

    r = _mix(r_ref[...], cr_ref, mu_r_ref[...])
    k = _mix(k_ref[...], ck_ref, mu_k_ref[...])
    v = _mix(v_ref[...], cv_ref, mu_v_ref[...])
    lo = _mix(lo_ref[...], clo_ref, mu_lo_ref[...])
    wd = lo[:, :RWKV_LORA]
    ad = lo[:, RWKV_LORA:]

    wlog = -_softplus(-(w0_ref[...] + _mm(jnp.tanh(wd), w2_ref[...]))) - 0.5
    lw = -jnp.exp(wlog)
    a = _sigmoid(a0_ref[...] + _mm(ad, a2_ref[...]))

    ex = _head_expand()
    red = _head_reduce()
    kk = k * kk_ref[...]
    nrm = jnp.maximum(jnp.sqrt(_mm_hi(kk * kk, red)), 1e-12)
    kk = kk * _mm_hi(1.0 / nrm, ex)
    kmod = k * (1.0 + (a - 1.0) * ka_ref[...])
    b = kk * a

    cum = _mm_hi(_tril(C).astype(F32), lw)
    gam = jnp.exp(cum)
    igam = jnp.exp(-cum)
    at = jnp.exp(cum - lw) * kk
    bt = b * igam
    kt = kmod * igam
    rt = r * gam
    g_last = gam[C - 1:C]

    strict = _tril(C, strict=True)
    incl = _tril(C)
    eye = jnp.where(lax.broadcasted_iota(I32, (C, C), 0) == lax.broadcasted_iota(I32, (C, C), 1), 1.0, 0.0)
    dot_nt = lambda p, q: lax.dot_general(p, q, (((1,), (1,)), ((), ())), preferred_element_type=F32, precision=HI)
    dot_tn = lambda p, q: lax.dot_general(p, q, (((0,), (0,)), ((), ())), preferred_element_type=F32, precision=HI)

    for h in range(N_HEADS):
        sl = slice(h * HEAD_DIM, (h + 1) * HEAD_DIM)
        at_h, bt_h, kt_h, rt_h, v_h = at[:, sl], bt[:, sl], kt[:, sl], rt[:, sl], v[:, sl]
        s_h = s_ref[h]
        l_ab = jnp.where(strict, dot_nt(at_h, bt_h), 0.0)
        l_ak = jnp.where(strict, dot_nt(at_h, kt_h), 0.0)
        m_rb = jnp.where(incl, dot_nt(rt_h, bt_h), 0.0)
        m_rk = jnp.where(incl, dot_nt(rt_h, kt_h), 0.0)
        pw = -l_ab
        t_inv = eye + pw
        for _ in range(int(math.log2(C)) - 1):
            pw = mmc(pw, pw)
            t_inv = t_inv + mmc(t_inv, pw)
        x_m = mmc(t_inv, at_h)
        z_m = mmc(t_inv, l_ak)
        r_hat = rt_h - mmc(m_rb, x_m)
        a_in = m_rk - mmc(m_rb, z_m)
        y_ref[:, sl] = dot_nt(r_hat, s_h) + mmc(a_in, v_h)
        g_m = eye - dot_tn(x_m, bt_h)
        k_hat = kt_h - dot_tn(z_m, bt_h)
        s_ref[h] = (mmc(s_h, g_m) + dot_tn(v_h, k_hat)) * g_last[:, sl]

    y = y_ref[...]
    inv_n = 1.0 / HEAD_DIM
    mean = _mm_hi(_mm_hi(y, red) * inv_n, ex)
    yc = y - mean
    var = _mm_hi(yc * yc, red) * inv_n
    yn = yc * _mm_hi(lax.rsqrt(var + RWKV_GN_EPS), ex) * lnw_ref[...] + lnb_ref[...]
    bonus = _mm_hi(_mm_hi(r * kmod * rk_ref[...], red), ex) * v
    gate = g_ref[...]
    o_ref[...] = ((yn + bonus) * (gate * _sigmoid(gate))).astype(BF16)


def _rwkv_branch(proj, B, S, mu, w0, w2, a0, a2, k_k, k_a, r_k, ln_w, ln_b):
    C = RWKV_CHUNK
    nc = S // C
    D = BRANCH_D
    row = lambda b, c: b * nc + c
    full = lambda shape: pl.BlockSpec(shape, lambda b, c: (0, 0))
    slab = lambda off: pl.BlockSpec((C, D), lambda b, c: (row(b, c), off // D))
    mu_r, mu_wd, mu_k, mu_v, mu_ad = (mu[:D], mu[D:D + RWKV_LORA], mu[D + RWKV_LORA:2 * D + RWKV_LORA],
                                      mu[2 * D + RWKV_LORA:3 * D + RWKV_LORA], mu[3 * D + RWKV_LORA:])
    r1 = lambda t: t.reshape(1, -1).astype(F32)
    return pl.pallas_call(
        _rwkv_kernel,
        grid=(B, nc),
        in_specs=[slab(OFF_RB), slab(OFF_KB), slab(OFF_VB), slab(OFF_GB),
                  pl.BlockSpec((C, LANES), lambda b, c: (row(b, c), OFF_LORA // LANES)),
                  full((1, D)), full((1, D)), full((1, D)), full((1, LANES)),
                  full((1, D)), full((RWKV_LORA, D)), full((1, D)), full((RWKV_LORA, D)),
                  full((1, D)), full((1, D)), full((1, D)), full((1, D)), full((1, D))],
        out_specs=pl.BlockSpec((C, D), lambda b, c: (row(b, c), 0)),
        out_shape=jax.ShapeDtypeStruct((B * S, D), BF16),
        scratch_shapes=[pltpu.VMEM((SUBLANES, D), F32), pltpu.VMEM((SUBLANES, D), F32),
                        pltpu.VMEM((SUBLANES, D), F32), pltpu.VMEM((SUBLANES, LANES), F32),
                        pltpu.VMEM((N_HEADS, HEAD_DIM, HEAD_DIM), F32),
                        pltpu.VMEM((C, D), F32)],
        compiler_params=_cparams(("parallel", "arbitrary")),
        name="rwkv_branch",
    )(proj, proj, proj, proj, proj, r1(mu_r), r1(mu_k), r1(mu_v), r1(jnp.concatenate([mu_wd, mu_ad])),
      r1(w0), w2, r1(a0), a2, r1(k_k), r1(k_a), r1(r_k), r1(ln_w), r1(ln_b))


def _rope_table_kernel(pos_ref, inv_ref, cos_ref, sin_ref):
    ang = pos_ref[...] * inv_ref[...]
    lane = lax.broadcasted_iota(I32, ang.shape, 1)
    cos_ref[...] = jnp.cos(ang)
    s = jnp.sin(ang)
    sin_ref[...] = jnp.where((lane & (HEAD_DIM - 1)) < HEAD_DIM // 2, -s, s)


def _rope_tables(positions):
    T = positions.size
    tq = min(512, T)
    pos = positions.reshape(T, 1).astype(F32)
    inv = ROPE_THETA ** (-(jnp.arange(HEAD_DIM // 2, dtype=F32) * 2.0 / HEAD_DIM))
    inv = jnp.tile(inv, LANES // (HEAD_DIM // 2))[None, :]
    return pl.pallas_call(
        _rope_table_kernel,
        grid=(T // tq,),
        in_specs=[pl.BlockSpec((tq, 1), lambda i: (i, 0)), pl.BlockSpec((1, LANES), lambda i: (0, 0))],
        out_specs=[pl.BlockSpec((tq, LANES), lambda i: (i, 0))] * 2,
        out_shape=[jax.ShapeDtypeStruct((T, LANES), F32)] * 2,
        compiler_params=_cparams(("parallel",)),
        name="rope_tables",
    )(pos, inv)


def _rope(x, cos, sin_signed):
    lane = lax.broadcasted_iota(I32, (1, LANES), 1)
    first = (lane & (HEAD_DIM - 1)) < HEAD_DIM // 2
    outs = []
    for c in range(x.shape[1] // LANES):
        xb = x[:, c * LANES:(c + 1) * LANES]
        partner = jnp.where(first, pltpu.roll(xb, LANES - HEAD_DIM // 2, 1), pltpu.roll(xb, HEAD_DIM // 2, 1))
        outs.append(xb * cos + partner * sin_signed)
    return outs[0] if len(outs) == 1 else jnp.concatenate(outs, axis=1)


def _dsa_prep_kernel(q_ref, iq_ref, k_ref, v_ref, idx_ref, cos_ref, sin_ref, nw_ref, nb_ref,
                     qo_ref, iqo_ref, ko_ref, vo_ref, iko_ref, iwo_ref):
    cos = cos_ref[...]
    sin = sin_ref[...]
    qo_ref[...] = (_rope(q_ref[...], cos, sin) * (HEAD_DIM ** -0.5)).astype(BF16)
    iqo_ref[...] = _rope(iq_ref[...], cos, sin).astype(BF16)
    ko_ref[...] = _rope(k_ref[...], cos, sin).astype(BF16)
    vo_ref[...] = v_ref[...].astype(BF16)
    idx = idx_ref[...]
    lane = lax.broadcasted_iota(I32, idx.shape, 1)
    is_k = lane < IDX_DIM
    mu = jnp.sum(jnp.where(is_k, idx, 0.0), axis=-1, keepdims=True) * (1.0 / IDX_DIM)
    dk = jnp.where(is_k, idx - mu, 0.0)
    var = jnp.sum(dk * dk, axis=-1, keepdims=True) * (1.0 / IDX_DIM)
    ikn = dk * lax.rsqrt(var + NORM_EPS) * nw_ref[...] + nb_ref[...]
    iko_ref[...] = _rope(ikn, cos, sin).astype(BF16)
    iw = pltpu.roll(idx, LANES - IDX_DIM, 1) * (IDX_HEADS ** -0.5 * IDX_DIM ** -0.5)
    iwo_ref[...] = jnp.where(lane < IDX_HEADS, iw, 0.0)


def _dsa_prep(proj, cos, sin, k_norm_w, k_norm_b):
    T = proj.shape[0]
    tq = min(256, T)
    D = BRANCH_D
    spec = lambda w, off: pl.BlockSpec((tq, w), lambda i: (i, off // w))
    out = lambda w: pl.BlockSpec((tq, w), lambda i: (i, 0))
    return pl.pallas_call(
        _dsa_prep_kernel,
        grid=(T // tq,),
        in_specs=[spec(D, OFF_QC), spec(D, OFF_IQ), spec(ATT_KV_D, OFF_KC), spec(ATT_KV_D, OFF_VC),
                  spec(LANES, OFF_IDX), out(LANES), out(LANES),
                  pl.BlockSpec((1, LANES), lambda i: (0, 0)), pl.BlockSpec((1, LANES), lambda i: (0, 0))],
        out_specs=[out(D), out(D), out(ATT_KV_D), out(ATT_KV_D), out(LANES), out(LANES)],
        out_shape=[jax.ShapeDtypeStruct((T, D), BF16), jax.ShapeDtypeStruct((T, D), BF16),
                   jax.ShapeDtypeStruct((T, ATT_KV_D), BF16), jax.ShapeDtypeStruct((T, ATT_KV_D), BF16),
                   jax.ShapeDtypeStruct((T, LANES), BF16), jax.ShapeDtypeStruct((T, LANES), F32)],
        compiler_params=_cparams(("parallel",)),
        name="dsa_prep",
    )(proj, proj, proj, proj, proj, cos, sin, _pad_lanes(k_norm_w), _pad_lanes(k_norm_b))


DSA_KT = 512


def _dsa_main_kernel(top_k, q_ref, iq_ref, iw_ref, gate_ref, ik_ref, k_ref, v_ref, o_ref, keys_ref, acc_ref):
    QB, KT = Q_BLOCK, DSA_KT
    qi = pl.program_id(1)
    nk = (qi * QB + QB + KT - 1) // KT
    qpos = qi * QB + lax.broadcasted_iota(I32, (QB, KT), 0)
    kcol = lax.broadcasted_iota(I32, (QB, KT), 1)

    iq = iq_ref[...]
    iw = iw_ref[...]
    iq_h = [iq[:, h * IDX_DIM:(h + 1) * IDX_DIM] for h in range(IDX_HEADS)]

    def score_tile(kt, _):
        off = pl.multiple_of(kt * KT, KT)
        ik_t = ik_ref[pl.ds(off, KT), :][:, :IDX_DIM]
        acc = jnp.zeros((QB, KT), F32)
        for h in range(IDX_HEADS):
            lg = lax.dot_general(iq_h[h], ik_t, (((1,), (1,)), ((), ())), preferred_element_type=F32)
            acc = acc + jnp.maximum(lg, 0.0) * iw[:, h:h + 1]
        bits = pltpu.bitcast(acc, I32)
        skey = jnp.where(bits < 0, bits ^ jnp.int32(0x7FFFFFFF), bits)
        keys_ref[kt] = jnp.where(kcol + off <= qpos, skey, jnp.int32(INT_MIN))
        return 0

    lax.fori_loop(0, nk, score_tile, 0)

    def count_ge(cand):
        def body(kt, acc):
            return acc + jnp.where(keys_ref[kt] >= cand, 1.0, 0.0)
        acc = lax.fori_loop(0, nk, body, jnp.zeros((QB, KT), F32))
        return jnp.sum(acc, axis=1, keepdims=True)

    def bit_body(i, res):
        cand_u = res | jnp.left_shift(jnp.int32(1), 31 - i)
        cnt = count_ge(cand_u ^ jnp.int32(INT_MIN))
        return jnp.where(cnt >= float(top_k), cand_u, res)

    res = lax.fori_loop(0, 32, bit_body, jnp.zeros((QB, 1), I32))
    thr = jnp.maximum(res ^ jnp.int32(INT_MIN), jnp.int32(INT_MIN + 1))

    q = q_ref[...]
    rep = N_HEADS // ATT_KV_HEADS
    for g in range(ATT_KV_HEADS):
        qg = jnp.concatenate([q[:, (g * rep + r) * HEAD_DIM:(g * rep + r + 1) * HEAD_DIM] for r in range(rep)], axis=0)

        def att_tile(kt, carry):
            m, l, acc = carry
            off = pl.multiple_of(kt * KT, KT)
            k_t = k_ref[pl.ds(off, KT), :][:, g * HEAD_DIM:(g + 1) * HEAD_DIM]
            v_t = v_ref[pl.ds(off, KT), :][:, g * HEAD_DIM:(g + 1) * HEAD_DIM]
            bias = jnp.where(keys_ref[kt] >= thr, 0.0, -1e30)
            lg = lax.dot_general(qg, k_t, (((1,), (1,)), ((), ())), preferred_element_type=F32)
            lg = lg + jnp.concatenate([bias] * rep, axis=0)
            m_new = jnp.maximum(m, jnp.max(lg, axis=1, keepdims=True))
            alpha = jnp.exp(m - m_new)
            p = jnp.exp(lg - m_new)
            l = l * alpha + jnp.sum(p, axis=1, keepdims=True)
            acc = acc * alpha + jnp.dot(p.astype(BF16), v_t, preferred_element_type=F32)
            return m_new, l, acc

        m0 = jnp.full((rep * QB, 1), -1e29, F32)
        m, l, acc = lax.fori_loop(0, nk, att_tile, (m0, jnp.zeros((rep * QB, 1), F32),
                                                    jnp.zeros((rep * QB, HEAD_DIM), F32)))
        og = acc / l
        for r in range(rep):
            h = g * rep + r
            acc_ref[:, h * HEAD_DIM:(h + 1) * HEAD_DIM] = og[r * QB:(r + 1) * QB]

    gate = gate_ref[...]
    o_ref[...] = (acc_ref[...] * (gate * _sigmoid(gate))).astype(BF16)


def _dsa_main(proj, qr, iqr, kr, vb, ikr, iws, B, S, top_k):
    QB = Q_BLOCK
    nq = S // QB
    D = BRANCH_D
    row = lambda b, i: b * nq + i
    return pl.pallas_call(
        functools.partial(_dsa_main_kernel, top_k),
        grid=(B, nq),
        in_specs=[pl.BlockSpec((QB, D), lambda b, i: (row(b, i), 0)),
                  pl.BlockSpec((QB, D), lambda b, i: (row(b, i), 0)),
                  pl.BlockSpec((QB, LANES), lambda b, i: (row(b, i), 0)),
                  pl.BlockSpec((QB, D), lambda b, i: (row(b, i), OFF_GC // D)),
                  pl.BlockSpec((S, LANES), lambda b, i: (b, 0)),
                  pl.BlockSpec((S, ATT_KV_D), lambda b, i: (b, 0)),
                  pl.BlockSpec((S, ATT_KV_D), lambda b, i: (b, 0))],
        out_specs=pl.BlockSpec((QB, D), lambda b, i: (row(b, i), 0)),
        out_shape=jax.ShapeDtypeStruct((B * S, D), BF16),
        scratch_shapes=[pltpu.VMEM((S // DSA_KT, QB, DSA_KT), I32), pltpu.VMEM((QB, D), F32)],
        compiler_params=_cparams(("parallel", "arbitrary")),
        name="dsa_main",
    )(qr, iqr, iws, proj, ikr, kr, vb)


def _dsa_branch(proj, cos, sin, B, S, k_norm_w, k_norm_b, top_k):
    qr, iqr, kr, vb, ikr, iws = _dsa_prep(proj, cos, sin, k_norm_w, k_norm_b)
    return _dsa_main(proj, qr, iqr, kr, vb, ikr, iws, B, S, top_k)


def _merge_kernel(ya_ref, yb_ref, yc_ref, ga_ref, gb_ref, gc_ref, ba_ref, bb_ref, bc_ref,
                  wa_ref, wb_ref, wc_ref, o_ref):
    acc = _sigmoid(ga_ref[...] + ba_ref[...]) * jnp.dot(ya_ref[...], wa_ref[...], preferred_element_type=F32)
    acc += _sigmoid(gb_ref[...] + bb_ref[...]) * jnp.dot(yb_ref[...], wb_ref[...], preferred_element_type=F32)
    acc += _sigmoid(gc_ref[...] + bc_ref[...]) * jnp.dot(yc_ref[...], wc_ref[...], preferred_element_type=F32)
    o_ref[...] = acc.astype(BF16)


def _merge(ya, yb, yc, proj, b_gate, wa, wb, wc):
    T = ya.shape[0]
    D, Db = D_MODEL, BRANCH_D
    tm = min(512, T)
    tn = 1024
    nj = D // tn
    y = pl.BlockSpec((tm, Db), lambda i, j: (i, 0))
    gate = lambda n: pl.BlockSpec((tm, tn), lambda i, j: (i, (OFF_GATES + n * D) // tn + j))
    bias = lambda n: pl.BlockSpec((1, tn), lambda i, j: (0, n * nj + j))
    w = pl.BlockSpec((Db, tn), lambda i, j: (0, j))
    bg = b_gate[None, :]
    return pl.pallas_call(
        _merge_kernel,
        grid=(T // tm, nj),
        in_specs=[y, y, y, gate(0), gate(1), gate(2), bias(0), bias(1), bias(2), w, w, w],
        out_specs=pl.BlockSpec((tm, tn), lambda i, j: (i, j)),
        out_shape=jax.ShapeDtypeStruct((T, D), BF16),
        compiler_params=_cparams(("parallel", "arbitrary")),
        name="gated_merge",
    )(ya, yb, yc, proj, proj, proj, bg, bg, bg, wa, wb, wc)


def _out_proj_kernel(m_ref, w_ref, g_ref, x_ref, o_ref):
    y = jnp.dot(m_ref[...], w_ref[...], preferred_element_type=F32)
    ms = jnp.mean(y * y, axis=-1, keepdims=True)
    o_ref[...] = x_ref[...] + y * lax.rsqrt(ms + NORM_EPS) * g_ref[...]


def _out_proj(merged, w_out, g, x):
    T, D = x.shape
    tm = min(256, T)
    return pl.pallas_call(
        _out_proj_kernel,
        grid=(T // tm,),
        in_specs=[pl.BlockSpec((tm, D), lambda i: (i, 0)), pl.BlockSpec((D, D), lambda i: (0, 0)),
                  pl.BlockSpec((1, D), lambda i: (0, 0)), pl.BlockSpec((tm, D), lambda i: (i, 0))],
        out_specs=pl.BlockSpec((tm, D), lambda i: (i, 0)),
        out_shape=jax.ShapeDtypeStruct((T, D), F32),
        compiler_params=_cparams(("parallel",)),
        name="out_proj_norm_residual",
    )(merged, w_out, g, x)


def _pack_w_in(w_in):
    a0, b0 = 0, A_COLS
    c0 = A_COLS + B_COLS
    g0 = c0 + C_COLS
    D = BRANCH_D
    sl = lambda s, n: w_in[:, :, s:s + n]
    zeros = lambda n: jnp.zeros(w_in.shape[:2] + (n,), w_in.dtype)
    pieces = [
        sl(a0 + D, SSM_CONV_DIM),
        sl(a0, D),
        sl(b0, D),
        sl(b0 + D + RWKV_LORA, D),
        sl(b0 + 2 * D + RWKV_LORA, D),
        sl(b0 + RWKV_SHIFT_DIM, D),
        sl(c0, D),
        sl(c0 + D + 2 * ATT_KV_D, D),
        sl(c0 + 2 * D + 2 * ATT_KV_D, D),
        sl(g0, GATE_COLS),
        sl(c0 + D, ATT_KV_D),
        sl(c0 + D + ATT_KV_D, ATT_KV_D),
        sl(a0 + D + SSM_CONV_DIM, N_HEADS), zeros(LANES - N_HEADS),
        sl(b0 + D, RWKV_LORA), sl(b0 + 3 * D + RWKV_LORA, RWKV_LORA),
        sl(c0 + 3 * D + 2 * ATT_KV_D, IDX_DIM + IDX_HEADS), zeros(LANES - IDX_DIM - IDX_HEADS),
    ]
    packed = jnp.concatenate(pieces, axis=-1)
    packed = jnp.concatenate([packed, zeros(NP_COLS - packed.shape[-1])], axis=-1)
    return packed.astype(BF16)


def kernel(x, positions, pre_norm, post_norm, w_in, b_gate, ssm_conv_w, ssm_conv_b, ssm_dt_bias, ssm_a_log,
           ssm_d, ssm_norm, rwkv_mu, rwkv_w0, rwkv_w2, rwkv_a0, rwkv_a2, rwkv_k_k, rwkv_k_a, rwkv_r_k,
           rwkv_ln_w, rwkv_ln_b, idx_k_norm_w, idx_k_norm_b, w_branch_a, w_branch_b, w_branch_c, w_out):
    B, S, D = x.shape
    depth = w_in.shape[0]
    top_k = min(TOPK_MAX, S // 4)
    xt = x.reshape(B * S, D)
    w_in_p = _pack_w_in(w_in)
    wa, wb, wc, wo = (w.astype(BF16) for w in (w_branch_a, w_branch_b, w_branch_c, w_out))
    cos, sin = _rope_tables(positions)
    for i in range(depth):
        proj = _norm_matmul(xt, pre_norm[i][None, :], w_in_p[i])
        ya = _ssd_branch(proj, B, S, ssm_conv_w[i], ssm_conv_b[i], ssm_dt_bias[i], ssm_a_log[i], ssm_d[i], ssm_norm[i])
        yb = _rwkv_branch(proj, B, S, rwkv_mu[i], rwkv_w0[i], rwkv_w2[i], rwkv_a0[i], rwkv_a2[i], rwkv_k_k[i],
                          rwkv_k_a[i], rwkv_r_k[i], rwkv_ln_w[i], rwkv_ln_b[i])
        yc = _dsa_branch(proj, cos, sin, B, S, idx_k_norm_w[i], idx_k_norm_b[i], top_k)
        merged = _merge(ya, yb, yc, proj, b_gate[i], wa[i], wb[i], wc[i])
        xt = _out_proj(merged, wo[i], post_norm[i][None, :], xt)
    return xt.reshape(B, S, D)
```

```python
import functools
import math

import numpy as np
import jax
import jax.numpy as jnp
from jax import lax
from jax.experimental import pallas as pl
from jax.experimental.pallas import tpu as pltpu

F32 = jnp.float32
BF16 = jnp.bfloat16
I32 = jnp.int32
HI = lax.Precision.HIGHEST

D_MODEL = 2048
BRANCH_D = D_MODEL // 2
NORM_EPS = 1e-6
HEAD_DIM = 64
N_HEADS = BRANCH_D // HEAD_DIM

SSM_GROUPS = 4
SSM_STATE = 128
SSM_CONV = 4
SSD_CHUNK = 128
SSM_CONV_DIM = BRANCH_D + 2 * SSM_GROUPS * SSM_STATE
A_COLS = BRANCH_D + SSM_CONV_DIM + N_HEADS

RWKV_LORA = 64
RWKV_SHIFT_DIM = 3 * BRANCH_D + 2 * RWKV_LORA
B_COLS = RWKV_SHIFT_DIM + BRANCH_D
RWKV_GN_EPS = 64e-5
RWKV_CHUNK = 64

ATT_KV_HEADS = 4
ATT_KV_D = ATT_KV_HEADS * HEAD_DIM
IDX_HEADS = 16
IDX_DIM = 64
TOPK_MAX = 256
Q_BLOCK = 128
ROPE_THETA = 10000.0
C_COLS = BRANCH_D + 2 * ATT_KV_D + BRANCH_D + IDX_HEADS * IDX_DIM + IDX_DIM + IDX_HEADS
GATE_COLS = 3 * D_MODEL

LANES = 128
SUBLANES = 8
VMEM_LIMIT = 56 * 1024 * 1024

OFF_XBC = 0
OFF_Z = 2048
OFF_RB = 3072
OFF_KB = 4096
OFF_VB = 5120
OFF_GB = 6144
OFF_QC = 7168
OFF_GC = 8192
OFF_IQ = 9216
OFF_GATES = 10240
OFF_KC = 16384
OFF_VC = 16640
OFF_DT = 16896
OFF_LORA = 17024
OFF_IDX = 17152
NP_COLS = 17408

INT_MIN = -2 ** 31


def _cparams(sem):
    return pltpu.CompilerParams(dimension_semantics=sem, vmem_limit_bytes=VMEM_LIMIT)


def _mm(a, b):
    return jnp.dot(a.astype(BF16), b.astype(BF16), preferred_element_type=F32)


def _mm_nt(a, b):
    return lax.dot_general(a.astype(BF16), b.astype(BF16), (((1,), (1,)), ((), ())),
                           preferred_element_type=F32)


def _mm_tn(a, b):
    return lax.dot_general(a.astype(BF16), b.astype(BF16), (((0,), (0,)), ((), ())),
                           preferred_element_type=F32)


def _mm_hi(a, b):
    return jnp.dot(a, b, preferred_element_type=F32, precision=HI)


def _head_expand():
    r = lax.broadcasted_iota(I32, (LANES, BRANCH_D), 0)
    c = lax.broadcasted_iota(I32, (LANES, BRANCH_D), 1)
    return jnp.where((c >> 6) == r, 1.0, 0.0).astype(F32)


def _head_reduce():
    r = lax.broadcasted_iota(I32, (BRANCH_D, LANES), 0)
    c = lax.broadcasted_iota(I32, (BRANCH_D, LANES), 1)
    return jnp.where((r >> 6) == c, 1.0, 0.0).astype(F32)


def _tril(n, strict=False):
    r = lax.broadcasted_iota(I32, (n, n), 0)
    c = lax.broadcasted_iota(I32, (n, n), 1)
    return (c < r) if strict else (c <= r)


def _softplus(x):
    return jnp.maximum(x, 0.0) + jnp.log1p(jnp.exp(-jnp.abs(x)))


def _sigmoid(x):
    return 1.0 / (1.0 + jnp.exp(-x))


def _shift_rows(x, carry, s):
    rolled = pltpu.roll(x, s, 0)
    rows = lax.broadcasted_iota(I32, (SUBLANES, x.shape[1]), 0)
    top = jnp.where(rows < s, pltpu.roll(carry, s, 0), rolled[0:SUBLANES])
    return jnp.concatenate([top, rolled[SUBLANES:]], axis=0)


def _norm_matmul_kernel(x_ref, g_ref, w_ref, o_ref, xn_ref):
    @pl.when(pl.program_id(1) == 0)
    def _():
        x = x_ref[...]
        ms = jnp.mean(x * x, axis=-1, keepdims=True)
        xn_ref[...] = (x * lax.rsqrt(ms + NORM_EPS) * g_ref[...]).astype(BF16)

    o_ref[...] = jnp.dot(xn_ref[...], w_ref[...], preferred_element_type=F32)


def _norm_matmul(x, g, w):
    T, D = x.shape
    Np = w.shape[1]
    tm = min(1024, T)
    tn = 1024
    return pl.pallas_call(
        _norm_matmul_kernel,
        grid=(T // tm, Np // tn),
        in_specs=[pl.BlockSpec((tm, D), lambda i, j: (i, 0)),
                  pl.BlockSpec((1, D), lambda i, j: (0, 0)),
                  pl.BlockSpec((D, tn), lambda i, j: (0, j))],
        out_specs=pl.BlockSpec((tm, tn), lambda i, j: (i, j)),
        out_shape=jax.ShapeDtypeStruct((T, Np), F32),
        scratch_shapes=[pltpu.VMEM((tm, D), BF16)],
        compiler_params=_cparams(("parallel", "arbitrary")),
        name="norm_in_proj",
    )(x, g, w)


def _ssd_kernel(xbc_ref, z_ref, dt_ref, cw_ref, cb_ref, dtb_ref, alog_ref, dskip_ref, nw_ref,
                o_ref, carry_ref, h_ref, y_ref):
    Q = SSD_CHUNK
    GN = SSM_GROUPS * SSM_STATE
    GP = BRANCH_D // SSM_GROUPS

    @pl.when(pl.program_id(1) == 0)
    def _():
        carry_ref[...] = jnp.zeros_like(carry_ref)
        h_ref[...] = jnp.zeros_like(h_ref)

    x = xbc_ref[...]
    carry = carry_ref[...]
    cw = cw_ref[...]
    acc = x * cw[SSM_CONV - 1:SSM_CONV] + cb_ref[...]
    for s in range(1, SSM_CONV):
        acc = acc + _shift_rows(x, carry, s) * cw[SSM_CONV - 1 - s:SSM_CONV - s]
    carry_ref[...] = x[Q - SUBLANES:Q]
    xc = acc * _sigmoid(acc)
    xs = xc[:, :BRANCH_D]
    bm = xc[:, BRANCH_D:BRANCH_D + GN]
    cm = xc[:, BRANCH_D + GN:]

    dt = _softplus(dt_ref[...] + dtb_ref[...])
    adt = dt * (-jnp.exp(alog_ref[...]))
    a_cs = _mm_hi(_tril(Q).astype(F32), adt)
    a_cs_t = a_cs.T
    a_last = a_cs[Q - 1:Q]
    ex = _head_expand()
    dt_e = _mm_hi(dt, ex)
    ecs_e = _mm_hi(jnp.exp(a_cs), ex)
    ds_e = _mm_hi(jnp.exp(a_last - a_cs), ex)
    cd_e = _mm_hi(jnp.exp(a_last), ex)
    xd = xs * dt_e
    xds = xd * ds_e
    causal = _tril(Q)

    for g in range(SSM_GROUPS):
        bm_g = bm[:, g * SSM_STATE:(g + 1) * SSM_STATE]
        cm_g = cm[:, g * SSM_STATE:(g + 1) * SSM_STATE]
        cb = _mm_nt(cm_g, bm_g)
        for r in range(N_HEADS // SSM_GROUPS):
            h = g * (N_HEADS // SSM_GROUPS) + r
            seg = a_cs[:, h:h + 1] - a_cs_t[h:h + 1, :]
            lmat = jnp.exp(jnp.where(causal, seg, -1e30))
            y_ref[:, h * HEAD_DIM:(h + 1) * HEAD_DIM] = _mm(cb * lmat, xd[:, h * HEAD_DIM:(h + 1) * HEAD_DIM])
        hg = h_ref[g]
        y_off = _mm(cm_g, hg) * ecs_e[:, g * GP:(g + 1) * GP]
        y_ref[:, g * GP:(g + 1) * GP] += y_off
        h_ref[g] = hg * cd_e[:, g * GP:(g + 1) * GP] + _mm_tn(bm_g, xds[:, g * GP:(g + 1) * GP])

    y = y_ref[...] + dskip_ref[...] * xs
    z = z_ref[...]
    yz = y * (z * _sigmoid(z))
    nw = nw_ref[...]
    for g in range(SSM_GROUPS):
        seg = yz[:, g * GP:(g + 1) * GP]
        ms = jnp.mean(seg * seg, axis=-1, keepdims=True)
        o_ref[:, g * GP:(g + 1) * GP] = (seg * lax.rsqrt(ms + NORM_EPS) * nw[:, g * GP:(g + 1) * GP]).astype(BF16)


def _pad_lanes(v, n=LANES):
    return jnp.zeros((1, n), F32).at[0, :v.shape[0]].set(v.astype(F32))


def _ssd_branch(proj, B, S, conv_w, conv_b, dt_bias, a_log, d_skip, norm_w):
    Q = SSD_CHUNK
    nc = S // Q
    row = lambda b, c: b * nc + c
    full = lambda shape: pl.BlockSpec(shape, lambda b, c: (0, 0))
    return pl.pallas_call(
        _ssd_kernel,
        grid=(B, nc),
        in_specs=[pl.BlockSpec((Q, SSM_CONV_DIM), lambda b, c: (row(b, c), OFF_XBC // SSM_CONV_DIM)),
                  pl.BlockSpec((Q, BRANCH_D), lambda b, c: (row(b, c), OFF_Z // BRANCH_D)),
                  pl.BlockSpec((Q, LANES), lambda b, c: (row(b, c), OFF_DT // LANES)),
                  full((SSM_CONV, SSM_CONV_DIM)), full((1, SSM_CONV_DIM)), full((1, LANES)), full((1, LANES)),
                  full((1, BRANCH_D)), full((1, BRANCH_D))],
        out_specs=pl.BlockSpec((Q, BRANCH_D), lambda b, c: (row(b, c), 0)),
        out_shape=jax.ShapeDtypeStruct((B * S, BRANCH_D), BF16),
        scratch_shapes=[pltpu.VMEM((SUBLANES, SSM_CONV_DIM), F32),
                        pltpu.VMEM((SSM_GROUPS, SSM_STATE, BRANCH_D // SSM_GROUPS), F32),
                        pltpu.VMEM((Q, BRANCH_D), F32)],
        compiler_params=_cparams(("parallel", "arbitrary")),
        name="ssd_branch",
    )(proj, proj, proj, conv_w, conv_b[None, :], _pad_lanes(dt_bias), _pad_lanes(a_log),
      jnp.repeat(d_skip, HEAD_DIM)[None, :], norm_w[None, :])


def _mix(x, carry_ref, mu):
    rolled = pltpu.roll(x, 1, 0)
    rows = lax.broadcasted_iota(I32, x.shape, 0)
    prev = jnp.where(rows == 0, carry_ref[SUBLANES - 1:SUBLANES], rolled)
    carry_ref[...] = x[x.shape[0] - SUBLANES:]
    return x + (prev - x) * mu


def _rwkv_kernel(r_ref, k_ref, v_ref, g_ref, lo_ref, mu_r_ref, mu_k_ref, mu_v_ref, mu_lo_ref,
                 w0_ref, w2_ref, a0_ref, a2_ref, kk_ref, ka_ref, rk_ref, lnw_ref, lnb_ref,
                 o_ref, cr_ref, ck_ref, cv_ref, clo_ref, s_ref, y_ref):
    C = RWKV_CHUNK
    mmc = _mm

    @pl.when(pl.program_id(1) == 0)
    def _():
        for ref in (cr_ref, ck_ref, cv_ref, clo_ref, s_ref):
            ref[...] = jnp.zeros_like(ref)

    r = _mix(r_ref[...], cr_ref, mu_r_ref[...])
    k = _mix(k_ref[...], ck_ref, mu_k_ref[...])
    v = _mix(v_ref[...], cv_ref, mu_v_ref[...])
    lo = _mix(lo_ref[...], clo_ref, mu_lo_ref[...])
    wd = lo[:, :RWKV_LORA]
    ad = lo[:, RWKV_LORA:]

    wlog = -_softplus(-(w0_ref[...] + _mm(jnp.tanh(wd), w2_ref[...]))) - 0.5
    lw = -jnp.exp(wlog)
    a = _sigmoid(a0_ref[...] + _mm(ad, a2_ref[...]))

    ex = _head_expand()
    red = _head_reduce()
    kk = k * kk_ref[...]
    nrm = jnp.maximum(jnp.sqrt(_mm_hi(kk * kk, red)), 1e-12)
    kk = kk * _mm_hi(1.0 / nrm, ex)
    kmod = k * (1.0 + (a - 1.0) * ka_ref[...])
    b = kk * a

    cum = _mm_hi(_tril(C).astype(F32), lw)
    gam = jnp.exp(cum)
    igam = jnp.exp(-cum)
    at = jnp.exp(cum - lw) * kk
    bt = b * igam
    kt = kmod * igam
    rt = r * gam
    g_last = gam[C - 1:C]

    strict = _tril(C, strict=True)
    incl = _tril(C)
    eye = jnp.where(lax.broadcasted_iota(I32, (C, C), 0) == lax.broadcasted_iota(I32, (C, C), 1), 1.0, 0.0)
    dot_nt, dot_tn = _mm_nt, _mm_tn

    H = range(N_HEADS)
    hs = lambda t: [t[:, h * HEAD_DIM:(h + 1) * HEAD_DIM] for h in H]
    at_b, bt_b, kt_b, rt_b, v_b = (hs(t.astype(BF16)) for t in (at, bt, kt, rt, v))
    kt_f, rt_f = hs(kt), hs(rt)
    s0 = [s_ref[h] for h in H]
    l_ab = [jnp.where(strict, dot_nt(at_b[h], bt_b[h]), 0.0) for h in H]
    l_ak = [jnp.where(strict, dot_nt(at_b[h], kt_b[h]), 0.0).astype(BF16) for h in H]
    m_rb = [jnp.where(incl, dot_nt(rt_b[h], bt_b[h]), 0.0).astype(BF16) for h in H]
    m_rk = [jnp.where(incl, dot_nt(rt_b[h], kt_b[h]), 0.0) for h in H]
    pw = [-l for l in l_ab]
    t_inv = [eye + p for p in pw]
    for _ in range(int(math.log2(C)) - 1):
        pw = [mmc(p, p) for p in pw]
        t_inv = [t + mmc(t, p) for t, p in zip(t_inv, pw)]
    t_b = [t.astype(BF16) for t in t_inv]
    x_m = [mmc(t_b[h], at_b[h]).astype(BF16) for h in H]
    z_m = [mmc(t_b[h], l_ak[h]).astype(BF16) for h in H]
    r_hat = [rt_f[h] - mmc(m_rb[h], x_m[h]) for h in H]
    a_in = [m_rk[h] - mmc(m_rb[h], z_m[h]) for h in H]
    y_h = [dot_nt(r_hat[h], s0[h]) + mmc(a_in[h], v_b[h]) for h in H]
    g_m = [eye - dot_tn(x_m[h], bt_b[h]) for h in H]
    k_hat = [kt_f[h] - dot_tn(z_m[h], bt_b[h]) for h in H]
    s_new = [(mmc(s0[h], g_m[h]) + dot_tn(v_b[h], k_hat[h])) * g_last[:, h * HEAD_DIM:(h + 1) * HEAD_DIM] for h in H]
    for h in H:
        y_ref[:, h * HEAD_DIM:(h + 1) * HEAD_DIM] = y_h[h]
        s_ref[h] = s_new[h]

    y = y_ref[...]
    inv_n = 1.0 / HEAD_DIM
    mean = _mm_hi(_mm_hi(y, red) * inv_n, ex)
    yc = y - mean
    var = _mm_hi(yc * yc, red) * inv_n
    yn = yc * _mm_hi(lax.rsqrt(var + RWKV_GN_EPS), ex) * lnw_ref[...] + lnb_ref[...]
    bonus = _mm_hi(_mm_hi(r * kmod * rk_ref[...], red), ex) * v
    gate = g_ref[...]
    o_ref[...] = ((yn + bonus) * (gate * _sigmoid(gate))).astype(BF16)


def _rwkv_branch(proj, B, S, mu, w0, w2, a0, a2, k_k, k_a, r_k, ln_w, ln_b):
    C = RWKV_CHUNK
    nc = S // C
    D = BRANCH_D
    row = lambda b, c: b * nc + c
    full = lambda shape: pl.BlockSpec(shape, lambda b, c: (0, 0))
    slab = lambda off: pl.BlockSpec((C, D), lambda b, c: (row(b, c), off // D))
    mu_r, mu_wd, mu_k, mu_v, mu_ad = (mu[:D], mu[D:D + RWKV_LORA], mu[D + RWKV_LORA:2 * D + RWKV_LORA],
                                      mu[2 * D + RWKV_LORA:3 * D + RWKV_LORA], mu[3 * D + RWKV_LORA:])
    r1 = lambda t: t.reshape(1, -1).astype(F32)
    return pl.pallas_call(
        _rwkv_kernel,
        grid=(B, nc),
        in_specs=[slab(OFF_RB), slab(OFF_KB), slab(OFF_VB), slab(OFF_GB),
                  pl.BlockSpec((C, LANES), lambda b, c: (row(b, c), OFF_LORA // LANES)),
                  full((1, D)), full((1, D)), full((1, D)), full((1, LANES)),
                  full((1, D)), full((RWKV_LORA, D)), full((1, D)), full((RWKV_LORA, D)),
                  full((1, D)), full((1, D)), full((1, D)), full((1, D)), full((1, D))],
        out_specs=pl.BlockSpec((C, D), lambda b, c: (row(b, c), 0)),
        out_shape=jax.ShapeDtypeStruct((B * S, D), BF16),
        scratch_shapes=[pltpu.VMEM((SUBLANES, D), F32), pltpu.VMEM((SUBLANES, D), F32),
                        pltpu.VMEM((SUBLANES, D), F32), pltpu.VMEM((SUBLANES, LANES), F32),
                        pltpu.VMEM((N_HEADS, HEAD_DIM, HEAD_DIM), F32),
                        pltpu.VMEM((C, D), F32)],
        compiler_params=_cparams(("parallel", "arbitrary")),
        name="rwkv_branch",
    )(proj, proj, proj, proj, proj, r1(mu_r), r1(mu_k), r1(mu_v), r1(jnp.concatenate([mu_wd, mu_ad])),
      r1(w0), w2, r1(a0), a2, r1(k_k), r1(k_a), r1(r_k), r1(ln_w), r1(ln_b))


def _rope_table_kernel(pos_ref, inv_ref, cos_ref, sin_ref):
    ang = pos_ref[...] * inv_ref[...]
    lane = lax.broadcasted_iota(I32, ang.shape, 1)
    cos_ref[...] = jnp.cos(ang)
    s = jnp.sin(ang)
    sin_ref[...] = jnp.where((lane & (HEAD_DIM - 1)) < HEAD_DIM // 2, -s, s)


def _rope_tables(positions):
    T = positions.size
    tq = min(512, T)
    pos = positions.reshape(T, 1).astype(F32)
    inv = ROPE_THETA ** (-(jnp.arange(HEAD_DIM // 2, dtype=F32) * 2.0 / HEAD_DIM))
    inv = jnp.tile(inv, LANES // (HEAD_DIM // 2))[None, :]
    return pl.pallas_call(
        _rope_table_kernel,
        grid=(T // tq,),
        in_specs=[pl.BlockSpec((tq, 1), lambda i: (i, 0)), pl.BlockSpec((1, LANES), lambda i: (0, 0))],
        out_specs=[pl.BlockSpec((tq, LANES), lambda i: (i, 0))] * 2,
        out_shape=[jax.ShapeDtypeStruct((T, LANES), F32)] * 2,
        compiler_params=_cparams(("parallel",)),
        name="rope_tables",
    )(pos, inv)


def _rope(x, cos, sin_signed):
    lane = lax.broadcasted_iota(I32, (1, LANES), 1)
    first = (lane & (HEAD_DIM - 1)) < HEAD_DIM // 2
    outs = []
    for c in range(x.shape[1] // LANES):
        xb = x[:, c * LANES:(c + 1) * LANES]
        partner = jnp.where(first, pltpu.roll(xb, LANES - HEAD_DIM // 2, 1), pltpu.roll(xb, HEAD_DIM // 2, 1))
        outs.append(xb * cos + partner * sin_signed)
    return outs[0] if len(outs) == 1 else jnp.concatenate(outs, axis=1)


DSA_QB = 256
DSA_KT = 512


def _dsa_prep_kernel(q_ref, iq_ref, k_ref, v_ref, idx_ref, cos_ref, sin_ref, nw_ref, nb_ref,
                     qo_ref, iqo_ref, kto_ref, vo_ref, ikto_ref, iwo_ref):
    cos = cos_ref[...]
    sin = sin_ref[...]
    qo_ref[...] = (_rope(q_ref[...], cos, sin) * (HEAD_DIM ** -0.5)).astype(BF16)
    iqo_ref[...] = _rope(iq_ref[...], cos, sin).astype(BF16)
    kto_ref[0] = _rope(k_ref[...], cos, sin).T.astype(BF16)
    v = v_ref[...]
    ones = jnp.ones((v.shape[0], HEAD_DIM), F32)
    vo_ref[...] = jnp.concatenate(
        [t for g in range(ATT_KV_HEADS) for t in (v[:, g * HEAD_DIM:(g + 1) * HEAD_DIM], ones)], axis=1).astype(BF16)
    idx = idx_ref[...]
    lane = lax.broadcasted_iota(I32, idx.shape, 1)
    is_k = lane < IDX_DIM
    mu = jnp.sum(jnp.where(is_k, idx, 0.0), axis=-1, keepdims=True) * (1.0 / IDX_DIM)
    dk = jnp.where(is_k, idx - mu, 0.0)
    var = jnp.sum(dk * dk, axis=-1, keepdims=True) * (1.0 / IDX_DIM)
    ikn = dk * lax.rsqrt(var + NORM_EPS) * nw_ref[...] + nb_ref[...]
    ikto_ref[0] = _rope(ikn, cos, sin).T[:IDX_DIM].astype(BF16)
    iw = pltpu.roll(idx, LANES - IDX_DIM, 1) * (IDX_HEADS ** -0.5 * IDX_DIM ** -0.5)
    iwo_ref[...] = jnp.where(lane < IDX_HEADS, iw, 0.0)


def _dsa_prep(proj, cos, sin, k_norm_w, k_norm_b):
    T = proj.shape[0]
    tq = DSA_KT
    D = BRANCH_D
    VA = 2 * ATT_KV_D
    spec = lambda w, off: pl.BlockSpec((tq, w), lambda i: (i, off // w))
    out = lambda w: pl.BlockSpec((tq, w), lambda i: (i, 0))
    tile_t = lambda n: pl.BlockSpec((1, n, tq), lambda i: (i, 0, 0))
    return pl.pallas_call(
        _dsa_prep_kernel,
        grid=(T // tq,),
        in_specs=[spec(D, OFF_QC), spec(D, OFF_IQ), spec(ATT_KV_D, OFF_KC), spec(ATT_KV_D, OFF_VC),
                  spec(LANES, OFF_IDX), out(LANES), out(LANES),
                  pl.BlockSpec((1, LANES), lambda i: (0, 0)), pl.BlockSpec((1, LANES), lambda i: (0, 0))],
        out_specs=[out(D), out(D), tile_t(ATT_KV_D), out(VA), tile_t(IDX_DIM), out(LANES)],
        out_shape=[jax.ShapeDtypeStruct((T, D), BF16), jax.ShapeDtypeStruct((T, D), BF16),
                   jax.ShapeDtypeStruct((T // tq, ATT_KV_D, tq), BF16), jax.ShapeDtypeStruct((T, VA), BF16),
                   jax.ShapeDtypeStruct((T // tq, IDX_DIM, tq), BF16), jax.ShapeDtypeStruct((T, LANES), F32)],
        compiler_params=_cparams(("parallel",)),
        name="dsa_prep",
    )(proj, proj, proj, proj, proj, cos, sin, _pad_lanes(k_norm_w), _pad_lanes(k_norm_b))


def _dsa_main_kernel(top_k, seq_len, q_ref, iq_ref, iw_ref, gate_ref, ikt_ref, kt_ref, v_ref, o_ref,
                     keys_ref, iwb_ref, bias_ref, m_ref, acc_ref, tie_ref):
    QB, KT = DSA_QB, DSA_KT
    NB = KT // LANES
    RH = QB // 2
    CH = KT // 2
    qi = pl.program_id(1)
    nk = (qi * QB + QB + KT - 1) // KT
    int_min = jnp.int32(INT_MIN)

    iq = iq_ref[...]
    iw = iw_ref[...]
    for h in range(IDX_HEADS):
        iwb_ref[h] = jnp.broadcast_to(iw[:, h:h + 1], (QB, LANES))
    qpos = qi * QB + lax.broadcasted_iota(I32, (RH, CH), 0)
    kcol = lax.broadcasted_iota(I32, (RH, CH), 1)

    def score_tile(kt, _):
        ikt = ikt_ref[kt]
        for rh in range(2):
            for ch in range(2):
                acc = jnp.zeros((RH, CH), F32)
                for h in range(IDX_HEADS):
                    lg = jnp.dot(iq[rh * RH:(rh + 1) * RH, h * IDX_DIM:(h + 1) * IDX_DIM],
                                 ikt[:, ch * CH:(ch + 1) * CH], preferred_element_type=F32)
                    w = iwb_ref[h, rh * RH:(rh + 1) * RH, :]
                    acc = acc + jnp.maximum(lg, 0.0) * jnp.concatenate([w] * (CH // LANES), axis=1)
                bits = pltpu.bitcast(acc, I32)
                skey = jnp.where(bits < 0, bits ^ jnp.int32(0x7FFFFFFF), bits)
                causal = kcol + (kt * KT + ch * CH) <= qpos + rh * RH
                keys_ref[kt, rh * RH:(rh + 1) * RH, ch * CH:(ch + 1) * CH] = jnp.where(causal, skey, int_min)
        return 0

    lax.fori_loop(0, nk, score_tile, 0)

    ones_b = jnp.ones((LANES, LANES), BF16)
    col0 = lax.broadcasted_iota(I32, (QB, LANES), 1)

    def count(pred):
        def body(kt, acc):
            kk = keys_ref[kt]
            for c in range(NB):
                hit = pred(kk[:, c * LANES:(c + 1) * LANES], col0 + (kt * KT + c * LANES))
                acc = acc + jnp.where(hit, 1.0, 0.0)
            return acc
        acc = lax.fori_loop(0, nk, body, jnp.zeros((QB, LANES), F32))
        return jnp.dot(acc.astype(BF16), ones_b, preferred_element_type=F32)

    def bit_body(i, res):
        cand_u = res | jnp.left_shift(jnp.int32(1), 31 - i)
        cand = cand_u ^ int_min
        cnt = count(lambda k, c: k >= cand)
        return jnp.where(cnt >= float(top_k), cand_u, res)

    res = lax.fori_loop(0, 32, bit_body, jnp.zeros((QB, LANES), I32))
    thr = jnp.maximum(res ^ int_min, jnp.int32(INT_MIN + 1))

    n_ge = count(lambda k, c: k >= thr)
    tie_ref[...] = jnp.full((QB, LANES), seq_len, I32)

    @pl.when(jnp.max(n_ge) > float(top_k))
    def _():
        need = float(top_k) - count(lambda k, c: k > thr)
        n_bits = max(1, (seq_len - 1).bit_length())

        def tie_body(i, x):
            cand = x | jnp.left_shift(jnp.int32(1), n_bits - 1 - i)
            below = count(lambda k, c: (k == thr) & (c < cand))
            return jnp.where(below < need, cand, x)

        tie_ref[...] = lax.fori_loop(0, n_bits, tie_body, jnp.zeros((QB, LANES), I32))

    tie = tie_ref[...]

    q = q_ref[...]
    rep = N_HEADS // ATT_KV_HEADS
    for h in range(N_HEADS):
        m_ref[h] = jnp.full((QB, 1), -1e29, F32)
        acc_ref[h] = jnp.zeros((QB, LANES), F32)

    def att_tile(kt, _):
        off = pl.multiple_of(kt * KT, KT)
        kk = keys_ref[kt]
        for c in range(NB):
            kb = kk[:, c * LANES:(c + 1) * LANES]
            sel = (kb > thr) | ((kb == thr) & (col0 + (kt * KT + c * LANES) <= tie))
            bias_ref[:, c * LANES:(c + 1) * LANES] = jnp.where(sel, 0.0, -1e30)
        for h in range(N_HEADS):
            g = h // rep
            lg = jnp.dot(q[:, h * HEAD_DIM:(h + 1) * HEAD_DIM], kt_ref[kt, g * HEAD_DIM:(g + 1) * HEAD_DIM, :],
                         preferred_element_type=F32) + bias_ref[...]
            m_old = m_ref[h]
            m_new = jnp.maximum(m_old, jnp.max(lg, axis=1, keepdims=True))
            p = jnp.exp(lg - m_new).astype(BF16)
            v_t = v_ref[pl.ds(off, KT), g * LANES:(g + 1) * LANES]
            acc_ref[h] = acc_ref[h] * jnp.exp(m_old - m_new) + jnp.dot(p, v_t, preferred_element_type=F32)
            m_ref[h] = m_new
        return 0

    lax.fori_loop(0, nk, att_tile, 0)

    gate = gate_ref[...]
    low = col0 < HEAD_DIM
    for j in range(N_HEADS // 2):
        a_e = acc_ref[2 * j]
        a_o = acc_ref[2 * j + 1]
        o2 = jnp.where(low, a_e / pltpu.roll(a_e, HEAD_DIM, 1), pltpu.roll(a_o, HEAD_DIM, 1) / a_o)
        gj = gate[:, j * LANES:(j + 1) * LANES]
        o_ref[:, j * LANES:(j + 1) * LANES] = (o2 * (gj * _sigmoid(gj))).astype(BF16)


def _dsa_main(proj, qr, iqr, ktr, va, iktr, iws, B, S, top_k):
    QB, KT = DSA_QB, DSA_KT
    nq = S // QB
    nkt = S // KT
    D = BRANCH_D
    row = lambda b, i: b * nq + i
    return pl.pallas_call(
        functools.partial(_dsa_main_kernel, top_k, S),
        grid=(B, nq),
        in_specs=[pl.BlockSpec((QB, D), lambda b, i: (row(b, i), 0)),
                  pl.BlockSpec((QB, D), lambda b, i: (row(b, i), 0)),
                  pl.BlockSpec((QB, LANES), lambda b, i: (row(b, i), 0)),
                  pl.BlockSpec((QB, D), lambda b, i: (row(b, i), OFF_GC // D)),
                  pl.BlockSpec((nkt, IDX_DIM, KT), lambda b, i: (b, 0, 0)),
                  pl.BlockSpec((nkt, ATT_KV_D, KT), lambda b, i: (b, 0, 0)),
                  pl.BlockSpec((S, 2 * ATT_KV_D), lambda b, i: (b, 0))],
        out_specs=pl.BlockSpec((QB, D), lambda b, i: (row(b, i), 0)),
        out_shape=jax.ShapeDtypeStruct((B * S, D), BF16),
        scratch_shapes=[pltpu.VMEM((nkt, QB, KT), I32),
                        pltpu.VMEM((IDX_HEADS, QB, LANES), F32),
                        pltpu.VMEM((QB, KT), F32),
                        pltpu.VMEM((N_HEADS, QB, 1), F32),
                        pltpu.VMEM((N_HEADS, QB, LANES), F32),
                        pltpu.VMEM((QB, LANES), I32)],
        compiler_params=_cparams(("parallel", "arbitrary")),
        name="dsa_main",
    )(qr, iqr, iws, proj, iktr, ktr, va)


def _fold_rows(x, op):
    parts = x.reshape(x.shape[0] // SUBLANES, SUBLANES, x.shape[1])
    while parts.shape[0] > 1:
        half = parts.shape[0] // 2
        parts = op(parts[:half], parts[half:])
    return parts[0]


def _dsa_prep_t_kernel(q_ref, iq_ref, k_ref, v_ref, idx_ref, cos_ref, sin_ref, nw_ref, nb_ref,
                       qt_ref, iqt_ref, ko_ref, vat_ref, iko_ref, iwt_ref):
    QB = DSA_QB
    cos = cos_ref[...]
    sin = sin_ref[...]
    nblk = q_ref.shape[0] // QB
    qt = (_rope(q_ref[...], cos, sin) * (HEAD_DIM ** -0.5)).T
    iqt = _rope(iq_ref[...], cos, sin).T
    for j in range(nblk):
        qt_ref[j] = qt[:, j * QB:(j + 1) * QB].astype(BF16)
        iqt_ref[j] = iqt[:, j * QB:(j + 1) * QB].astype(BF16)
    ko_ref[...] = _rope(k_ref[...], cos, sin).astype(BF16)
    vt = v_ref[...].T
    ones = jnp.ones((HEAD_DIM, vt.shape[1]), F32)
    vat_ref[0] = jnp.concatenate(
        [t for g in range(ATT_KV_HEADS) for t in (vt[g * HEAD_DIM:(g + 1) * HEAD_DIM], ones)], axis=0).astype(BF16)
    idx = idx_ref[...]
    lane = lax.broadcasted_iota(I32, idx.shape, 1)
    is_k = lane < IDX_DIM
    mu = jnp.sum(jnp.where(is_k, idx, 0.0), axis=-1, keepdims=True) * (1.0 / IDX_DIM)
    dk = jnp.where(is_k, idx - mu, 0.0)
    var = jnp.sum(dk * dk, axis=-1, keepdims=True) * (1.0 / IDX_DIM)
    ikn = dk * lax.rsqrt(var + NORM_EPS) * nw_ref[...] + nb_ref[...]
    iko_ref[...] = _rope(ikn, cos, sin).astype(BF16)
    iwt = (pltpu.roll(idx, LANES - IDX_DIM, 1) * (IDX_HEADS ** -0.5 * IDX_DIM ** -0.5)).T
    for j in range(nblk):
        iwt_ref[j] = iwt[:IDX_HEADS, j * QB:(j + 1) * QB]


def _dsa_prep_t(proj, cos, sin, k_norm_w, k_norm_b):
    T = proj.shape[0]
    tq = DSA_KT
    QB = DSA_QB
    D = BRANCH_D
    VA = 2 * ATT_KV_D
    nb = tq // QB
    spec = lambda w, off: pl.BlockSpec((tq, w), lambda i: (i, off // w))
    out = lambda w: pl.BlockSpec((tq, w), lambda i: (i, 0))
    blk = lambda n: pl.BlockSpec((nb, n, QB), lambda i: (i, 0, 0))
    return pl.pallas_call(
        _dsa_prep_t_kernel,
        grid=(T // tq,),
        in_specs=[spec(D, OFF_QC), spec(D, OFF_IQ), spec(ATT_KV_D, OFF_KC), spec(ATT_KV_D, OFF_VC),
                  spec(LANES, OFF_IDX), out(LANES), out(LANES),
                  pl.BlockSpec((1, LANES), lambda i: (0, 0)), pl.BlockSpec((1, LANES), lambda i: (0, 0))],
        out_specs=[blk(D), blk(D), out(ATT_KV_D), pl.BlockSpec((1, VA, tq), lambda i: (i, 0, 0)), out(LANES),
                   blk(IDX_HEADS)],
        out_shape=[jax.ShapeDtypeStruct((T // QB, D, QB), BF16), jax.ShapeDtypeStruct((T // QB, D, QB), BF16),
                   jax.ShapeDtypeStruct((T, ATT_KV_D), BF16), jax.ShapeDtypeStruct((T // tq, VA, tq), BF16),
                   jax.ShapeDtypeStruct((T, LANES), BF16), jax.ShapeDtypeStruct((T // QB, IDX_HEADS, QB), F32)],
        compiler_params=_cparams(("parallel",)),
        name="dsa_prep",
    )(proj, proj, proj, proj, proj, cos, sin, _pad_lanes(k_norm_w), _pad_lanes(k_norm_b))


def _dsa_main_t_kernel(top_k, seq_len, qt_ref, iqt_ref, iwt_ref, gate_ref, ik_ref, k_ref, vat_ref, o_ref,
                       keys_ref, sacc_ref, bias_ref, m_ref, acc_ref, tie_ref, ot_ref):
    QB, KT = DSA_QB, DSA_KT
    qi = pl.program_id(1)
    nk = (qi * QB + QB + KT - 1) // KT
    int_min = jnp.int32(INT_MIN)
    rep = N_HEADS // ATT_KV_HEADS
    kpos0 = lax.broadcasted_iota(I32, (KT, QB), 0)
    qpos = qi * QB + lax.broadcasted_iota(I32, (KT, QB), 1)

    def score_tile(kt, _):
        off = pl.multiple_of(kt * KT, KT)
        ik_t = ik_ref[pl.ds(off, KT), :][:, :IDX_DIM]
        for h0 in range(0, IDX_HEADS, 4):
            lg = [jnp.dot(ik_t, iqt_ref[h * IDX_DIM:(h + 1) * IDX_DIM, :], preferred_element_type=F32)
                  for h in range(h0, h0 + 4)]
            term = sum(jnp.maximum(l, 0.0) * iwt_ref[h:h + 1, :] for l, h in zip(lg, range(h0, h0 + 4)))
            if h0 == 0:
                sacc_ref[...] = term
            else:
                sacc_ref[...] += term
        bits = pltpu.bitcast(sacc_ref[...], I32)
        skey = jnp.where(bits < 0, bits ^ jnp.int32(0x7FFFFFFF), bits)
        keys_ref[kt] = jnp.where(kpos0 + off <= qpos, skey, int_min)
        return 0

    lax.fori_loop(0, nk, score_tile, 0)

    def count(pred):
        def body(kt, acc):
            hit = pred(keys_ref[kt], kpos0 + kt * KT)
            return acc + _fold_rows(jnp.where(hit, 1.0, 0.0), jnp.add)
        acc = lax.fori_loop(0, nk, body, jnp.zeros((SUBLANES, QB), F32))
        return jnp.sum(acc, axis=0, keepdims=True)

    def bit_body(i, res):
        cand_u = res | jnp.left_shift(jnp.int32(1), 31 - i)
        cand = cand_u ^ int_min
        cnt = count(lambda k, c: k >= cand)
        return jnp.where(cnt >= float(top_k), cand_u, res)

    res = lax.fori_loop(0, 32, bit_body, jnp.zeros((1, QB), I32))
    thr = jnp.maximum(res ^ int_min, jnp.int32(INT_MIN + 1))

    n_ge = count(lambda k, c: k >= thr)
    tie_ref[...] = jnp.full((1, QB), seq_len, I32)

    @pl.when(jnp.max(n_ge) > float(top_k))
    def _():
        need = float(top_k) - count(lambda k, c: k > thr)
        n_bits = max(1, (seq_len - 1).bit_length())

        def tie_body(i, x):
            cand = x | jnp.left_shift(jnp.int32(1), n_bits - 1 - i)
            below = count(lambda k, c: (k == thr) & (c < cand))
            return jnp.where(below < need, cand, x)

        tie_ref[...] = lax.fori_loop(0, n_bits, tie_body, jnp.zeros((1, QB), I32))

    tie = tie_ref[...]

    m_ref[...] = jnp.full(m_ref.shape, -1e29, F32)
    acc_ref[...] = jnp.zeros(acc_ref.shape, F32)

    def att_tile(kt, _):
        off = pl.multiple_of(kt * KT, KT)
        kk = keys_ref[kt]
        sel = (kk > thr) | ((kk == thr) & (kpos0 + off <= tie))
        bias_ref[...] = jnp.where(sel, 0.0, -1e30)
        k_t = k_ref[pl.ds(off, KT), :]
        for g in range(ATT_KV_HEADS):
            k_g = k_t[:, g * HEAD_DIM:(g + 1) * HEAD_DIM]
            va_g = vat_ref[kt, g * LANES:(g + 1) * LANES, :]
            hs = [g * rep + r for r in range(rep)]
            lg = [jnp.dot(k_g, qt_ref[h * HEAD_DIM:(h + 1) * HEAD_DIM, :], preferred_element_type=F32)
                  + bias_ref[...] for h in hs]
            m_old = [m_ref[h] for h in hs]
            m_new = [jnp.maximum(mo, jnp.max(_fold_rows(l, jnp.maximum), axis=0, keepdims=True))
                     for mo, l in zip(m_old, lg)]
            p = [jnp.exp(l - mn).astype(BF16) for l, mn in zip(lg, m_new)]
            pv = [jnp.dot(va_g, pp, preferred_element_type=F32) for pp in p]
            for j, h in enumerate(hs):
                acc_ref[h] = acc_ref[h] * jnp.exp(m_old[j] - m_new[j]) + pv[j]
                m_ref[h] = m_new[j]
        return 0

    lax.fori_loop(0, nk, att_tile, 0)

    for h in range(N_HEADS):
        a = acc_ref[h]
        ot_ref[h * HEAD_DIM:(h + 1) * HEAD_DIM, :] = a[:HEAD_DIM] / a[HEAD_DIM:]
    gate = gate_ref[...]
    o_ref[...] = (ot_ref[...].T * (gate * _sigmoid(gate))).astype(BF16)


def _dsa_main_t(proj, qt, iqt, kr, vat, ikr, iwt, B, S, top_k):
    QB, KT = DSA_QB, DSA_KT
    nq = S // QB
    nkt = S // KT
    D = BRANCH_D
    row = lambda b, i: b * nq + i
    return pl.pallas_call(
        functools.partial(_dsa_main_t_kernel, top_k, S),
        grid=(B, nq),
        in_specs=[pl.BlockSpec((None, D, QB), lambda b, i: (row(b, i), 0, 0)),
                  pl.BlockSpec((None, D, QB), lambda b, i: (row(b, i), 0, 0)),
                  pl.BlockSpec((None, IDX_HEADS, QB), lambda b, i: (row(b, i), 0, 0)),
                  pl.BlockSpec((QB, D), lambda b, i: (row(b, i), OFF_GC // D)),
                  pl.BlockSpec((S, LANES), lambda b, i: (b, 0)),
                  pl.BlockSpec((S, ATT_KV_D), lambda b, i: (b, 0)),
                  pl.BlockSpec((nkt, 2 * ATT_KV_D, KT), lambda b, i: (b, 0, 0))],
        out_specs=pl.BlockSpec((QB, D), lambda b, i: (row(b, i), 0)),
        out_shape=jax.ShapeDtypeStruct((B * S, D), BF16),
        scratch_shapes=[pltpu.VMEM((nkt, KT, QB), I32),
                        pltpu.VMEM((KT, QB), F32),
                        pltpu.VMEM((KT, QB), F32),
                        pltpu.VMEM((N_HEADS, 1, QB), F32),
                        pltpu.VMEM((N_HEADS, LANES, QB), F32),
                        pltpu.VMEM((1, QB), I32),
                        pltpu.VMEM((D, QB), F32)],
        compiler_params=_cparams(("parallel", "arbitrary")),
        name="dsa_main",
    )(qt, iqt, iwt, proj, ikr, kr, vat)


def _dsa_branch(proj, cos, sin, B, S, k_norm_w, k_norm_b, top_k):
    qt, iqt, kr, vat, ikr, iwt = _dsa_prep_t(proj, cos, sin, k_norm_w, k_norm_b)
    return _dsa_main_t(proj, qt, iqt, kr, vat, ikr, iwt, B, S, top_k)


def _merge_kernel(ya_ref, yb_ref, yc_ref, ga_ref, gb_ref, gc_ref, ba_ref, bb_ref, bc_ref,
                  wa_ref, wb_ref, wc_ref, o_ref):
    acc = _sigmoid(ga_ref[...] + ba_ref[...]) * jnp.dot(ya_ref[...], wa_ref[...], preferred_element_type=F32)
    acc += _sigmoid(gb_ref[...] + bb_ref[...]) * jnp.dot(yb_ref[...], wb_ref[...], preferred_element_type=F32)
    acc += _sigmoid(gc_ref[...] + bc_ref[...]) * jnp.dot(yc_ref[...], wc_ref[...], preferred_element_type=F32)
    o_ref[...] = acc.astype(BF16)


def _merge(ya, yb, yc, proj, b_gate, wa, wb, wc):
    T = ya.shape[0]
    D, Db = D_MODEL, BRANCH_D
    tm = min(512, T)
    tn = 1024
    nj = D // tn
    y = pl.BlockSpec((tm, Db), lambda i, j: (i, 0))
    gate = lambda n: pl.BlockSpec((tm, tn), lambda i, j: (i, (OFF_GATES + n * D) // tn + j))
    bias = lambda n: pl.BlockSpec((1, tn), lambda i, j: (0, n * nj + j))
    w = pl.BlockSpec((Db, tn), lambda i, j: (0, j))
    bg = b_gate[None, :]
    return pl.pallas_call(
        _merge_kernel,
        grid=(T // tm, nj),
        in_specs=[y, y, y, gate(0), gate(1), gate(2), bias(0), bias(1), bias(2), w, w, w],
        out_specs=pl.BlockSpec((tm, tn), lambda i, j: (i, j)),
        out_shape=jax.ShapeDtypeStruct((T, D), BF16),
        compiler_params=_cparams(("parallel", "arbitrary")),
        name="gated_merge",
    )(ya, yb, yc, proj, proj, proj, bg, bg, bg, wa, wb, wc)


def _out_proj_kernel(m_ref, w_ref, g_ref, x_ref, o_ref):
    y = jnp.dot(m_ref[...], w_ref[...], preferred_element_type=F32)
    ms = jnp.mean(y * y, axis=-1, keepdims=True)
    o_ref[...] = x_ref[...] + y * lax.rsqrt(ms + NORM_EPS) * g_ref[...]


def _out_proj(merged, w_out, g, x):
    T, D = x.shape
    tm = min(256, T)
    return pl.pallas_call(
        _out_proj_kernel,
        grid=(T // tm,),
        in_specs=[pl.BlockSpec((tm, D), lambda i: (i, 0)), pl.BlockSpec((D, D), lambda i: (0, 0)),
                  pl.BlockSpec((1, D), lambda i: (0, 0)), pl.BlockSpec((tm, D), lambda i: (i, 0))],
        out_specs=pl.BlockSpec((tm, D), lambda i: (i, 0)),
        out_shape=jax.ShapeDtypeStruct((T, D), F32),
        compiler_params=_cparams(("parallel",)),
        name="out_proj_norm_residual",
    )(merged, w_out, g, x)


def _pack_w_in(w_in):
    a0, b0 = 0, A_COLS
    c0 = A_COLS + B_COLS
    g0 = c0 + C_COLS
    D = BRANCH_D
    sl = lambda s, n: w_in[:, :, s:s + n]
    zeros = lambda n: jnp.zeros(w_in.shape[:2] + (n,), w_in.dtype)
    pieces = [
        sl(a0 + D, SSM_CONV_DIM),
        sl(a0, D),
        sl(b0, D),
        sl(b0 + D + RWKV_LORA, D),
        sl(b0 + 2 * D + RWKV_LORA, D),
        sl(b0 + RWKV_SHIFT_DIM, D),
        sl(c0, D),
        sl(c0 + D + 2 * ATT_KV_D, D),
        sl(c0 + 2 * D + 2 * ATT_KV_D, D),
        sl(g0, GATE_COLS),
        sl(c0 + D, ATT_KV_D),
        sl(c0 + D + ATT_KV_D, ATT_KV_D),
        sl(a0 + D + SSM_CONV_DIM, N_HEADS), zeros(LANES - N_HEADS),
        sl(b0 + D, RWKV_LORA), sl(b0 + 3 * D + RWKV_LORA, RWKV_LORA),
        sl(c0 + 3 * D + 2 * ATT_KV_D, IDX_DIM + IDX_HEADS), zeros(LANES - IDX_DIM - IDX_HEADS),
    ]
    packed = jnp.concatenate(pieces, axis=-1)
    packed = jnp.concatenate([packed, zeros(NP_COLS - packed.shape[-1])], axis=-1)
    return packed.astype(BF16)


def kernel(x, positions, pre_norm, post_norm, w_in, b_gate, ssm_conv_w, ssm_conv_b, ssm_dt_bias, ssm_a_log,
           ssm_d, ssm_norm, rwkv_mu, rwkv_w0, rwkv_w2, rwkv_a0, rwkv_a2, rwkv_k_k, rwkv_k_a, rwkv_r_k,
           rwkv_ln_w, rwkv_ln_b, idx_k_norm_w, idx_k_norm_b, w_branch_a, w_branch_b, w_branch_c, w_out):
    B, S, D = x.shape
    depth = w_in.shape[0]
    top_k = min(TOPK_MAX, S // 4)
    xt = x.reshape(B * S, D)
    w_in_p = _pack_w_in(w_in)
    wa, wb, wc, wo = (w.astype(BF16) for w in (w_branch_a, w_branch_b, w_branch_c, w_out))
    cos, sin = _rope_tables(positions)
    for i in range(depth):
        proj = _norm_matmul(xt, pre_norm[i][None, :], w_in_p[i])
        ya = _ssd_branch(proj, B, S, ssm_conv_w[i], ssm_conv_b[i], ssm_dt_bias[i], ssm_a_log[i], ssm_d[i], ssm_norm[i])
        yb = _rwkv_branch(proj, B, S, rwkv_mu[i], rwkv_w0[i], rwkv_w2[i], rwkv_a0[i], rwkv_a2[i], rwkv_k_k[i],
                          rwkv_k_a[i], rwkv_r_k[i], rwkv_ln_w[i], rwkv_ln_b[i])
        yc = _dsa_branch(proj, cos, sin, B, S, idx_k_norm_w[i], idx_k_norm_b[i], top_k)
        merged = _merge(ya, yb, yc, proj, b_gate[i], wa[i], wb[i], wc[i])
        xt = _out_proj(merged, wo[i], post_norm[i][None, :], xt)
    return xt.reshape(B, S, D)
```

```python
import functools
import math

import numpy as np
import jax
import jax.numpy as jnp
from jax import lax
from jax.experimental import pallas as pl
from jax.experimental.pallas import tpu as pltpu

F32 = jnp.float32
BF16 = jnp.bfloat16
I32 = jnp.int32
HI = lax.Precision.HIGHEST

D_MODEL = 2048
BRANCH_D = D_MODEL // 2
NORM_EPS = 1e-6
HEAD_DIM = 64
N_HEADS = BRANCH_D // HEAD_DIM

SSM_GROUPS = 4
SSM_STATE = 128
SSM_CONV = 4
SSD_CHUNK = 128
SSM_CONV_DIM = BRANCH_D + 2 * SSM_GROUPS * SSM_STATE
A_COLS = BRANCH_D + SSM_CONV_DIM + N_HEADS

RWKV_LORA = 64
RWKV_SHIFT_DIM = 3 * BRANCH_D + 2 * RWKV_LORA
B_COLS = RWKV_SHIFT_DIM + BRANCH_D
RWKV_GN_EPS = 64e-5
RWKV_CHUNK = 64

ATT_KV_HEADS = 4
ATT_KV_D = ATT_KV_HEADS * HEAD_DIM
IDX_HEADS = 16
IDX_DIM = 64
TOPK_MAX = 256
Q_BLOCK = 128
ROPE_THETA = 10000.0
C_COLS = BRANCH_D + 2 * ATT_KV_D + BRANCH_D + IDX_HEADS * IDX_DIM + IDX_DIM + IDX_HEADS
GATE_COLS = 3 * D_MODEL

LANES = 128
SUBLANES = 8
VMEM_LIMIT = 56 * 1024 * 1024

OFF_XBC = 0
OFF_Z = 2048
OFF_RB = 3072
OFF_KB = 4096
OFF_VB = 5120
OFF_GB = 6144
OFF_QC = 7168
OFF_GC = 8192
OFF_IQ = 9216
OFF_GATES = 10240
OFF_KC = 16384
OFF_VC = 16640
OFF_DT = 16896
OFF_LORA = 17024
OFF_IDX = 17152
NP_COLS = 17408

INT_MIN = -2 ** 31


def _cparams(sem):
    return pltpu.CompilerParams(dimension_semantics=sem, vmem_limit_bytes=VMEM_LIMIT)


def _mm(a, b):
    return jnp.dot(a.astype(BF16), b.astype(BF16), preferred_element_type=F32)


def _mm_nt(a, b):
    return lax.dot_general(a.astype(BF16), b.astype(BF16), (((1,), (1,)), ((), ())),
                           preferred_element_type=F32)


def _mm_tn(a, b):
    return lax.dot_general(a.astype(BF16), b.astype(BF16), (((0,), (0,)), ((), ())),
                           preferred_element_type=F32)


def _mm_hi(a, b, exact="b"):
    x, m = (a, b) if exact == "b" else (b, a)
    m = m.astype(BF16)
    acc = None
    for _ in range(3):
        xb = x.astype(BF16)
        part = jnp.dot(xb, m, preferred_element_type=F32) if exact == "b" else jnp.dot(m, xb, preferred_element_type=F32)
        acc = part if acc is None else acc + part
        x = x - xb.astype(F32)
    return acc


def _head_expand():
    r = lax.broadcasted_iota(I32, (LANES, BRANCH_D), 0)
    c = lax.broadcasted_iota(I32, (LANES, BRANCH_D), 1)
    return jnp.where((c >> 6) == r, 1.0, 0.0).astype(F32)


def _head_reduce():
    r = lax.broadcasted_iota(I32, (BRANCH_D, LANES), 0)
    c = lax.broadcasted_iota(I32, (BRANCH_D, LANES), 1)
    return jnp.where((r >> 6) == c, 1.0, 0.0).astype(F32)


def _tril(n, strict=False):
    r = lax.broadcasted_iota(I32, (n, n), 0)
    c = lax.broadcasted_iota(I32, (n, n), 1)
    return (c < r) if strict else (c <= r)


def _softplus(x):
    return jnp.maximum(x, 0.0) + jnp.log1p(jnp.exp(-jnp.abs(x)))


def _sigmoid(x):
    return 1.0 / (1.0 + jnp.exp(-x))


def _shift_rows(x, carry, s):
    rolled = pltpu.roll(x, s, 0)
    rows = lax.broadcasted_iota(I32, (SUBLANES, x.shape[1]), 0)
    top = jnp.where(rows < s, pltpu.roll(carry, s, 0), rolled[0:SUBLANES])
    return jnp.concatenate([top, rolled[SUBLANES:]], axis=0)


def _norm_matmul_kernel(x_ref, g_ref, w_ref, o_ref, xn_ref):
    @pl.when(pl.program_id(1) == 0)
    def _():
        x = x_ref[...]
        ms = jnp.mean(x * x, axis=-1, keepdims=True)
        xn_ref[...] = (x * lax.rsqrt(ms + NORM_EPS) * g_ref[...]).astype(BF16)

    o_ref[...] = jnp.dot(xn_ref[...], w_ref[...], preferred_element_type=F32)


def _norm_matmul(x, g, w):
    T, D = x.shape
    Np = w.shape[1]
    tm = min(1024, T)
    tn = 1024
    return pl.pallas_call(
        _norm_matmul_kernel,
        grid=(T // tm, Np // tn),
        in_specs=[pl.BlockSpec((tm, D), lambda i, j: (i, 0)),
                  pl.BlockSpec((1, D), lambda i, j: (0, 0)),
                  pl.BlockSpec((D, tn), lambda i, j: (0, j))],
        out_specs=pl.BlockSpec((tm, tn), lambda i, j: (i, j)),
        out_shape=jax.ShapeDtypeStruct((T, Np), F32),
        scratch_shapes=[pltpu.VMEM((tm, D), BF16)],
        compiler_params=_cparams(("parallel", "arbitrary")),
        name="norm_in_proj",
    )(x, g, w)


def _ssd_kernel(xbc_ref, z_ref, dt_ref, cw_ref, cb_ref, dtb_ref, alog_ref, dskip_ref, nw_ref,
                o_ref, carry_ref, h_ref, y_ref):
    Q = SSD_CHUNK
    GN = SSM_GROUPS * SSM_STATE
    GP = BRANCH_D // SSM_GROUPS

    @pl.when(pl.program_id(1) == 0)
    def _():
        carry_ref[...] = jnp.zeros_like(carry_ref)
        h_ref[...] = jnp.zeros_like(h_ref)

    x = xbc_ref[...]
    carry = carry_ref[...]
    cw = cw_ref[...]
    acc = x * cw[SSM_CONV - 1:SSM_CONV] + cb_ref[...]
    for s in range(1, SSM_CONV):
        acc = acc + _shift_rows(x, carry, s) * cw[SSM_CONV - 1 - s:SSM_CONV - s]
    carry_ref[...] = x[Q - SUBLANES:Q]
    xc = acc * _sigmoid(acc)
    xs = xc[:, :BRANCH_D]
    bm = xc[:, BRANCH_D:BRANCH_D + GN]
    cm = xc[:, BRANCH_D + GN:]

    dt = _softplus(dt_ref[...] + dtb_ref[...])
    adt = dt * (-jnp.exp(alog_ref[...]))
    a_cs = _mm_hi(_tril(Q).astype(F32), adt, exact="a")
    a_cs_t = a_cs.T
    a_last = a_cs[Q - 1:Q]
    ex = _head_expand()
    dt_e = _mm_hi(dt, ex)
    ecs_e = _mm_hi(jnp.exp(a_cs), ex)
    ds_e = _mm_hi(jnp.exp(a_last - a_cs), ex)
    cd_e = _mm_hi(jnp.exp(a_last), ex)
    xd = xs * dt_e
    xds = xd * ds_e
    causal = _tril(Q)

    G = range(SSM_GROUPS)
    H = range(N_HEADS)
    rep = N_HEADS // SSM_GROUPS
    bm_b, cm_b, xd_b, xds_b = (t.astype(BF16) for t in (bm, cm, xd, xds))
    bm_g = [bm_b[:, g * SSM_STATE:(g + 1) * SSM_STATE] for g in G]
    cm_g = [cm_b[:, g * SSM_STATE:(g + 1) * SSM_STATE] for g in G]
    hg = [h_ref[g] for g in G]
    cb = [_mm_nt(cm_g[g], bm_g[g]) for g in G]
    lmat = [jnp.exp(jnp.where(causal, a_cs[:, h:h + 1] - a_cs_t[h:h + 1, :], -1e30)) for h in H]
    y_diag = [_mm(cb[h // rep] * lmat[h], xd_b[:, h * HEAD_DIM:(h + 1) * HEAD_DIM]) for h in H]
    y_off = [_mm(cm_g[g], hg[g]) * ecs_e[:, g * GP:(g + 1) * GP] for g in G]
    h_new = [hg[g] * cd_e[:, g * GP:(g + 1) * GP] + _mm_tn(bm_g[g], xds_b[:, g * GP:(g + 1) * GP]) for g in G]
    for g in G:
        h_ref[g] = h_new[g]
        y_ref[:, g * GP:(g + 1) * GP] = y_off[g]
    for h in H:
        y_ref[:, h * HEAD_DIM:(h + 1) * HEAD_DIM] += y_diag[h]

    y = y_ref[...] + dskip_ref[...] * xs
    z = z_ref[...]
    yz = y * (z * _sigmoid(z))
    nw = nw_ref[...]
    for g in range(SSM_GROUPS):
        seg = yz[:, g * GP:(g + 1) * GP]
        ms = jnp.mean(seg * seg, axis=-1, keepdims=True)
        o_ref[:, g * GP:(g + 1) * GP] = (seg * lax.rsqrt(ms + NORM_EPS) * nw[:, g * GP:(g + 1) * GP]).astype(BF16)


def _pad_lanes(v, n=LANES):
    return jnp.zeros((1, n), F32).at[0, :v.shape[0]].set(v.astype(F32))


def _ssd_branch(proj, B, S, conv_w, conv_b, dt_bias, a_log, d_skip, norm_w):
    Q = SSD_CHUNK
    nc = S // Q
    row = lambda b, c: b * nc + c
    full = lambda shape: pl.BlockSpec(shape, lambda b, c: (0, 0))
    return pl.pallas_call(
        _ssd_kernel,
        grid=(B, nc),
        in_specs=[pl.BlockSpec((Q, SSM_CONV_DIM), lambda b, c: (row(b, c), OFF_XBC // SSM_CONV_DIM)),
                  pl.BlockSpec((Q, BRANCH_D), lambda b, c: (row(b, c), OFF_Z // BRANCH_D)),
                  pl.BlockSpec((Q, LANES), lambda b, c: (row(b, c), OFF_DT // LANES)),
                  full((SSM_CONV, SSM_CONV_DIM)), full((1, SSM_CONV_DIM)), full((1, LANES)), full((1, LANES)),
                  full((1, BRANCH_D)), full((1, BRANCH_D))],
        out_specs=pl.BlockSpec((Q, BRANCH_D), lambda b, c: (row(b, c), 0)),
        out_shape=jax.ShapeDtypeStruct((B * S, BRANCH_D), BF16),
        scratch_shapes=[pltpu.VMEM((SUBLANES, SSM_CONV_DIM), F32),
                        pltpu.VMEM((SSM_GROUPS, SSM_STATE, BRANCH_D // SSM_GROUPS), F32),
                        pltpu.VMEM((Q, BRANCH_D), F32)],
        compiler_params=_cparams(("parallel", "arbitrary")),
        name="ssd_branch",
    )(proj, proj, proj, conv_w, conv_b[None, :], _pad_lanes(dt_bias), _pad_lanes(a_log),
      jnp.repeat(d_skip, HEAD_DIM)[None, :], norm_w[None, :])


def _mix(x, carry_ref, mu):
    rolled = pltpu.roll(x, 1, 0)
    rows = lax.broadcasted_iota(I32, x.shape, 0)
    prev = jnp.where(rows == 0, carry_ref[SUBLANES - 1:SUBLANES], rolled)
    carry_ref[...] = x[x.shape[0] - SUBLANES:]
    return x + (prev - x) * mu


def _rwkv_kernel(r_ref, k_ref, v_ref, g_ref, lo_ref, mu_r_ref, mu_k_ref, mu_v_ref, mu_lo_ref,
                 w0_ref, w2_ref, a0_ref, a2_ref, kk_ref, ka_ref, rk_ref, lnw_ref, lnb_ref,
                 o_ref, cr_ref, ck_ref, cv_ref, clo_ref, s_ref, y_ref):
    C = RWKV_CHUNK

    @pl.when(pl.program_id(1) == 0)
    def _():
        for ref in (cr_ref, ck_ref, cv_ref, clo_ref, s_ref):
            ref[...] = jnp.zeros_like(ref)

    r = _mix(r_ref[...], cr_ref, mu_r_ref[...])
    k = _mix(k_ref[...], ck_ref, mu_k_ref[...])
    v = _mix(v_ref[...], cv_ref, mu_v_ref[...])
    lo = _mix(lo_ref[...], clo_ref, mu_lo_ref[...])
    wd = lo[:, :RWKV_LORA]
    ad = lo[:, RWKV_LORA:]

    wlog = -_softplus(-(w0_ref[...] + _mm(jnp.tanh(wd), w2_ref[...]))) - 0.5
    lw = -jnp.exp(wlog)
    a = _sigmoid(a0_ref[...] + _mm(ad, a2_ref[...]))

    ex = _head_expand()
    red = _head_reduce()
    kk = k * kk_ref[...]
    nrm = jnp.maximum(jnp.sqrt(_mm_hi(kk * kk, red)), 1e-12)
    kk = kk * _mm_hi(1.0 / nrm, ex)
    kmod = k * (1.0 + (a - 1.0) * ka_ref[...])
    b = kk * a

    cum = _mm_hi(_tril(C).astype(F32), lw, exact="a")
    gam = jnp.exp(cum)
    igam = jnp.exp(-cum)
    at = jnp.exp(cum - lw) * kk
    bt = b * igam
    kt = kmod * igam
    rt = r * gam
    g_last = gam[C - 1:C]

    H = range(N_HEADS)
    P = range(N_HEADS // 2)
    lane = lax.broadcasted_iota(I32, (C, LANES), 1)
    rowi = lax.broadcasted_iota(I32, (C, LANES), 0)
    low = lane < HEAD_DIM
    lane_in = lane & (HEAD_DIM - 1)
    strict2 = lane_in < rowi
    incl2 = lane_in <= rowi
    eye2 = jnp.where(lane_in == rowi, 1.0, 0.0)
    low2 = jnp.concatenate([low, low], axis=0)
    zeros_b = jnp.zeros((C, LANES), BF16)
    cols = lambda t: [t[:, p * LANES:(p + 1) * LANES] for p in P]
    at_c, bt_c, kt_c, rt_c, v_c = (cols(t.astype(BF16)) for t in (at, bt, kt, rt, v))
    rt_f = cols(rt)
    nat = lambda h, a, b: jnp.where(low, a, b) if h % 2 == 0 else jnp.where(low, b, a)
    nat2 = lambda h, a, b: jnp.where(low2, a, b) if h % 2 == 0 else jnp.where(low2, b, a)
    rows = lambda top, bot: jnp.concatenate([top, bot], axis=0)
    nat_rows = lambda h, t: rows(t, zeros_b) if h % 2 == 0 else rows(zeros_b, t)
    oth_rows = lambda h, t: rows(zeros_b, t) if h % 2 == 0 else rows(t, zeros_b)

    p1 = [_mm_nt(rows(nat(h, at_c[h // 2], zeros_b), nat(h, rt_c[h // 2], zeros_b)),
                 rows(bt_c[h // 2], kt_c[h // 2]) if h % 2 == 0 else rows(kt_c[h // 2], bt_c[h // 2])) for h in H]
    top = [jnp.where(strict2, p[:C], 0.0) for p in p1]
    bot = [jnp.where(incl2, p[C:], 0.0) for p in p1]
    tp = [nat(h, -top[h], eye2) for h in H]
    for _ in range(int(math.log2(C))):
        tp_b = [t.astype(BF16) for t in tp]
        out = [_mm(tp_b[h], nat_rows(h, tp_b[h])) for h in H]
        tp = [out[h] + nat(h, 0.0, tp[h]) for h in H]
    tp_b = [t.astype(BF16) for t in tp]
    top_b = [t.astype(BF16) for t in top]
    bot_b = [t.astype(BF16) for t in bot]
    xz = [_mm(tp_b[h], oth_rows(h, nat(h, at_c[h // 2], top_b[h]))) for h in H]
    xz_b = [t.astype(BF16) for t in xz]
    mxz = [_mm(bot_b[h], nat_rows(h, xz_b[h])) for h in H]
    ra = [nat(h, rt_f[h // 2], bot[h]) - mxz[h] for h in H]
    stack = [rows(ra[h], xz[h]).astype(BF16) for h in H]
    s_b = [s_ref[p].astype(BF16) for p in P]
    yw = [_mm_nt(nat2(h, stack[h], jnp.zeros_like(stack[h])), s_b[h // 2]) + _mm(stack[h], oth_rows(h, v_c[h // 2]))
          for h in H]
    yw_p = [jnp.where(low2, yw[2 * p], yw[2 * p + 1]) for p in P]
    blockdiag = (lax.broadcasted_iota(I32, (LANES, LANES), 0) < HEAD_DIM) == (
        lax.broadcasted_iota(I32, (LANES, LANES), 1) < HEAD_DIM)
    inc = [_mm_tn(rows(v_c[p], (-yw_p[p][C:]).astype(BF16)), rows(kt_c[p], bt_c[p])) for p in P]
    for p in P:
        y_ref[:, p * LANES:(p + 1) * LANES] = yw_p[p][:C]
        s_ref[p] = (s_ref[p] + jnp.where(blockdiag, inc[p], 0.0)) * g_last[:, p * LANES:(p + 1) * LANES]

    y = y_ref[...]
    inv_n = 1.0 / HEAD_DIM
    mean = _mm_hi(_mm_hi(y, red) * inv_n, ex)
    yc = y - mean
    var = _mm_hi(yc * yc, red) * inv_n
    yn = yc * _mm_hi(lax.rsqrt(var + RWKV_GN_EPS), ex) * lnw_ref[...] + lnb_ref[...]
    bonus = _mm_hi(_mm_hi(r * kmod * rk_ref[...], red), ex) * v
    gate = g_ref[...]
    o_ref[...] = ((yn + bonus) * (gate * _sigmoid(gate))).astype(BF16)


def _rwkv_branch(proj, B, S, mu, w0, w2, a0, a2, k_k, k_a, r_k, ln_w, ln_b):
    C = RWKV_CHUNK
    nc = S // C
    D = BRANCH_D
    row = lambda b, c: b * nc + c
    full = lambda shape: pl.BlockSpec(shape, lambda b, c: (0, 0))
    slab = lambda off: pl.BlockSpec((C, D), lambda b, c: (row(b, c), off // D))
    mu_r, mu_wd, mu_k, mu_v, mu_ad = (mu[:D], mu[D:D + RWKV_LORA], mu[D + RWKV_LORA:2 * D + RWKV_LORA],
                                      mu[2 * D + RWKV_LORA:3 * D + RWKV_LORA], mu[3 * D + RWKV_LORA:])
    r1 = lambda t: t.reshape(1, -1).astype(F32)
    return pl.pallas_call(
        _rwkv_kernel,
        grid=(B, nc),
        in_specs=[slab(OFF_RB), slab(OFF_KB), slab(OFF_VB), slab(OFF_GB),
                  pl.BlockSpec((C, LANES), lambda b, c: (row(b, c), OFF_LORA // LANES)),
                  full((1, D)), full((1, D)), full((1, D)), full((1, LANES)),
                  full((1, D)), full((RWKV_LORA, D)), full((1, D)), full((RWKV_LORA, D)),
                  full((1, D)), full((1, D)), full((1, D)), full((1, D)), full((1, D))],
        out_specs=pl.BlockSpec((C, D), lambda b, c: (row(b, c), 0)),
        out_shape=jax.ShapeDtypeStruct((B * S, D), BF16),
        scratch_shapes=[pltpu.VMEM((SUBLANES, D), F32), pltpu.VMEM((SUBLANES, D), F32),
                        pltpu.VMEM((SUBLANES, D), F32), pltpu.VMEM((SUBLANES, LANES), F32),
                        pltpu.VMEM((N_HEADS // 2, LANES, LANES), F32),
                        pltpu.VMEM((C, D), F32)],
        compiler_params=_cparams(("parallel", "arbitrary")),
        name="rwkv_branch",
    )(proj, proj, proj, proj, proj, r1(mu_r), r1(mu_k), r1(mu_v), r1(jnp.concatenate([mu_wd, mu_ad])),
      r1(w0), w2, r1(a0), a2, r1(k_k), r1(k_a), r1(r_k), r1(ln_w), r1(ln_b))


def _rope_table_kernel(pos_ref, inv_ref, cos_ref, sin_ref):
    ang = pos_ref[...] * inv_ref[...]
    lane = lax.broadcasted_iota(I32, ang.shape, 1)
    cos_ref[...] = jnp.cos(ang)
    s = jnp.sin(ang)
    sin_ref[...] = jnp.where((lane & (HEAD_DIM - 1)) < HEAD_DIM // 2, -s, s)


def _rope_tables(positions):
    T = positions.size
    tq = min(512, T)
    pos = positions.reshape(T, 1).astype(F32)
    inv = ROPE_THETA ** (-(jnp.arange(HEAD_DIM // 2, dtype=F32) * 2.0 / HEAD_DIM))
    inv = jnp.tile(inv, LANES // (HEAD_DIM // 2))[None, :]
    return pl.pallas_call(
        _rope_table_kernel,
        grid=(T // tq,),
        in_specs=[pl.BlockSpec((tq, 1), lambda i: (i, 0)), pl.BlockSpec((1, LANES), lambda i: (0, 0))],
        out_specs=[pl.BlockSpec((tq, LANES), lambda i: (i, 0))] * 2,
        out_shape=[jax.ShapeDtypeStruct((T, LANES), F32)] * 2,
        compiler_params=_cparams(("parallel",)),
        name="rope_tables",
    )(pos, inv)


def _rope(x, cos, sin_signed):
    lane = lax.broadcasted_iota(I32, (1, LANES), 1)
    first = (lane & (HEAD_DIM - 1)) < HEAD_DIM // 2
    outs = []
    for c in range(x.shape[1] // LANES):
        xb = x[:, c * LANES:(c + 1) * LANES]
        partner = jnp.where(first, pltpu.roll(xb, LANES - HEAD_DIM // 2, 1), pltpu.roll(xb, HEAD_DIM // 2, 1))
        outs.append(xb * cos + partner * sin_signed)
    return outs[0] if len(outs) == 1 else jnp.concatenate(outs, axis=1)


DSA_QB = 256
DSA_KT = 512


def _dsa_prep_kernel(q_ref, iq_ref, k_ref, v_ref, idx_ref, cos_ref, sin_ref, nw_ref, nb_ref,
                     qo_ref, iqo_ref, kto_ref, vo_ref, ikto_ref, iwo_ref):
    cos = cos_ref[...]
    sin = sin_ref[...]
    qo_ref[...] = (_rope(q_ref[...], cos, sin) * (HEAD_DIM ** -0.5)).astype(BF16)
    iqo_ref[...] = _rope(iq_ref[...], cos, sin).astype(BF16)
    kto_ref[0] = _rope(k_ref[...], cos, sin).T.astype(BF16)
    v = v_ref[...]
    ones = jnp.ones((v.shape[0], HEAD_DIM), F32)
    vo_ref[...] = jnp.concatenate(
        [t for g in range(ATT_KV_HEADS) for t in (v[:, g * HEAD_DIM:(g + 1) * HEAD_DIM], ones)], axis=1).astype(BF16)
    idx = idx_ref[...]
    lane = lax.broadcasted_iota(I32, idx.shape, 1)
    is_k = lane < IDX_DIM
    mu = jnp.sum(jnp.where(is_k, idx, 0.0), axis=-1, keepdims=True) * (1.0 / IDX_DIM)
    dk = jnp.where(is_k, idx - mu, 0.0)
    var = jnp.sum(dk * dk, axis=-1, keepdims=True) * (1.0 / IDX_DIM)
    ikn = dk * lax.rsqrt(var + NORM_EPS) * nw_ref[...] + nb_ref[...]
    ikto_ref[0] = _rope(ikn, cos, sin).T[:IDX_DIM].astype(BF16)
    iw = pltpu.roll(idx, LANES - IDX_DIM, 1) * (IDX_HEADS ** -0.5 * IDX_DIM ** -0.5)
    iwo_ref[...] = jnp.where(lane < IDX_HEADS, iw, 0.0)


def _dsa_prep(proj, cos, sin, k_norm_w, k_norm_b):
    T = proj.shape[0]
    tq = DSA_KT
    D = BRANCH_D
    VA = 2 * ATT_KV_D
    spec = lambda w, off: pl.BlockSpec((tq, w), lambda i: (i, off // w))
    out = lambda w: pl.BlockSpec((tq, w), lambda i: (i, 0))
    tile_t = lambda n: pl.BlockSpec((1, n, tq), lambda i: (i, 0, 0))
    return pl.pallas_call(
        _dsa_prep_kernel,
        grid=(T // tq,),
        in_specs=[spec(D, OFF_QC), spec(D, OFF_IQ), spec(ATT_KV_D, OFF_KC), spec(ATT_KV_D, OFF_VC),
                  spec(LANES, OFF_IDX), out(LANES), out(LANES),
                  pl.BlockSpec((1, LANES), lambda i: (0, 0)), pl.BlockSpec((1, LANES), lambda i: (0, 0))],
        out_specs=[out(D), out(D), tile_t(ATT_KV_D), out(VA), tile_t(IDX_DIM), out(LANES)],
        out_shape=[jax.ShapeDtypeStruct((T, D), BF16), jax.ShapeDtypeStruct((T, D), BF16),
                   jax.ShapeDtypeStruct((T // tq, ATT_KV_D, tq), BF16), jax.ShapeDtypeStruct((T, VA), BF16),
                   jax.ShapeDtypeStruct((T // tq, IDX_DIM, tq), BF16), jax.ShapeDtypeStruct((T, LANES), F32)],
        compiler_params=_cparams(("parallel",)),
        name="dsa_prep",
    )(proj, proj, proj, proj, proj, cos, sin, _pad_lanes(k_norm_w), _pad_lanes(k_norm_b))


def _dsa_main_kernel(top_k, seq_len, q_ref, iq_ref, iw_ref, gate_ref, ikt_ref, kt_ref, v_ref, o_ref,
                     keys_ref, iwb_ref, bias_ref, m_ref, acc_ref, tie_ref):
    QB, KT = DSA_QB, DSA_KT
    NB = KT // LANES
    RH = QB // 2
    CH = KT // 2
    qi = pl.program_id(1)
    nk = (qi * QB + QB + KT - 1) // KT
    int_min = jnp.int32(INT_MIN)

    iq = iq_ref[...]
    iw = iw_ref[...]
    for h in range(IDX_HEADS):
        iwb_ref[h] = jnp.broadcast_to(iw[:, h:h + 1], (QB, LANES))
    qpos = qi * QB + lax.broadcasted_iota(I32, (RH, CH), 0)
    kcol = lax.broadcasted_iota(I32, (RH, CH), 1)

    def score_tile(kt, _):
        ikt = ikt_ref[kt]
        for rh in range(2):
            for ch in range(2):
                acc = jnp.zeros((RH, CH), F32)
                for h in range(IDX_HEADS):
                    lg = jnp.dot(iq[rh * RH:(rh + 1) * RH, h * IDX_DIM:(h + 1) * IDX_DIM],
                                 ikt[:, ch * CH:(ch + 1) * CH], preferred_element_type=F32)
                    w = iwb_ref[h, rh * RH:(rh + 1) * RH, :]
                    acc = acc + jnp.maximum(lg, 0.0) * jnp.concatenate([w] * (CH // LANES), axis=1)
                bits = pltpu.bitcast(acc, I32)
                skey = jnp.where(bits < 0, bits ^ jnp.int32(0x7FFFFFFF), bits)
                causal = kcol + (kt * KT + ch * CH) <= qpos + rh * RH
                keys_ref[kt, rh * RH:(rh + 1) * RH, ch * CH:(ch + 1) * CH] = jnp.where(causal, skey, int_min)
        return 0

    lax.fori_loop(0, nk, score_tile, 0)

    ones_b = jnp.ones((LANES, LANES), BF16)
    col0 = lax.broadcasted_iota(I32, (QB, LANES), 1)

    def count(pred):
        def body(kt, acc):
            kk = keys_ref[kt]
            for c in range(NB):
                hit = pred(kk[:, c * LANES:(c + 1) * LANES], col0 + (kt * KT + c * LANES))
                acc = acc + jnp.where(hit, 1.0, 0.0)
            return acc
        acc = lax.fori_loop(0, nk, body, jnp.zeros((QB, LANES), F32))
        return jnp.dot(acc.astype(BF16), ones_b, preferred_element_type=F32)

    def bit_body(i, res):
        cand_u = res | jnp.left_shift(jnp.int32(1), 31 - i)
        cand = cand_u ^ int_min
        cnt = count(lambda k, c: k >= cand)
        return jnp.where(cnt >= float(top_k), cand_u, res)

    res = lax.fori_loop(0, 32, bit_body, jnp.zeros((QB, LANES), I32))
    thr = jnp.maximum(res ^ int_min, jnp.int32(INT_MIN + 1))

    n_ge = count(lambda k, c: k >= thr)
    tie_ref[...] = jnp.full((QB, LANES), seq_len, I32)

    @pl.when(jnp.max(n_ge) > float(top_k))
    def _():
        need = float(top_k) - count(lambda k, c: k > thr)
        n_bits = max(1, (seq_len - 1).bit_length())

        def tie_body(i, x):
            cand = x | jnp.left_shift(jnp.int32(1), n_bits - 1 - i)
            below = count(lambda k, c: (k == thr) & (c < cand))
            return jnp.where(below < need, cand, x)

        tie_ref[...] = lax.fori_loop(0, n_bits, tie_body, jnp.zeros((QB, LANES), I32))

    tie = tie_ref[...]

    q = q_ref[...]
    rep = N_HEADS // ATT_KV_HEADS
    for h in range(N_HEADS):
        m_ref[h] = jnp.full((QB, 1), -1e29, F32)
        acc_ref[h] = jnp.zeros((QB, LANES), F32)

    def att_tile(kt, _):
        off = pl.multiple_of(kt * KT, KT)
        kk = keys_ref[kt]
        for c in range(NB):
            kb = kk[:, c * LANES:(c + 1) * LANES]
            sel = (kb > thr) | ((kb == thr) & (col0 + (kt * KT + c * LANES) <= tie))
            bias_ref[:, c * LANES:(c + 1) * LANES] = jnp.where(sel, 0.0, -1e30)
        for h in range(N_HEADS):
            g = h // rep
            lg = jnp.dot(q[:, h * HEAD_DIM:(h + 1) * HEAD_DIM], kt_ref[kt, g * HEAD_DIM:(g + 1) * HEAD_DIM, :],
                         preferred_element_type=F32) + bias_ref[...]
            m_old = m_ref[h]
            m_new = jnp.maximum(m_old, jnp.max(lg, axis=1, keepdims=True))
            p = jnp.exp(lg - m_new).astype(BF16)
            v_t = v_ref[pl.ds(off, KT), g * LANES:(g + 1) * LANES]
            acc_ref[h] = acc_ref[h] * jnp.exp(m_old - m_new) + jnp.dot(p, v_t, preferred_element_type=F32)
            m_ref[h] = m_new
        return 0

    lax.fori_loop(0, nk, att_tile, 0)

    gate = gate_ref[...]
    low = col0 < HEAD_DIM
    for j in range(N_HEADS // 2):
        a_e = acc_ref[2 * j]
        a_o = acc_ref[2 * j + 1]
        o2 = jnp.where(low, a_e / pltpu.roll(a_e, HEAD_DIM, 1), pltpu.roll(a_o, HEAD_DIM, 1) / a_o)
        gj = gate[:, j * LANES:(j + 1) * LANES]
        o_ref[:, j * LANES:(j + 1) * LANES] = (o2 * (gj * _sigmoid(gj))).astype(BF16)


def _dsa_main(proj, qr, iqr, ktr, va, iktr, iws, B, S, top_k):
    QB, KT = DSA_QB, DSA_KT
    nq = S // QB
    nkt = S // KT
    D = BRANCH_D
    row = lambda b, i: b * nq + i
    return pl.pallas_call(
        functools.partial(_dsa_main_kernel, top_k, S),
        grid=(B, nq),
        in_specs=[pl.BlockSpec((QB, D), lambda b, i: (row(b, i), 0)),
                  pl.BlockSpec((QB, D), lambda b, i: (row(b, i), 0)),
                  pl.BlockSpec((QB, LANES), lambda b, i: (row(b, i), 0)),
                  pl.BlockSpec((QB, D), lambda b, i: (row(b, i), OFF_GC // D)),
                  pl.BlockSpec((nkt, IDX_DIM, KT), lambda b, i: (b, 0, 0)),
                  pl.BlockSpec((nkt, ATT_KV_D, KT), lambda b, i: (b, 0, 0)),
                  pl.BlockSpec((S, 2 * ATT_KV_D), lambda b, i: (b, 0))],
        out_specs=pl.BlockSpec((QB, D), lambda b, i: (row(b, i), 0)),
        out_shape=jax.ShapeDtypeStruct((B * S, D), BF16),
        scratch_shapes=[pltpu.VMEM((nkt, QB, KT), I32),
                        pltpu.VMEM((IDX_HEADS, QB, LANES), F32),
                        pltpu.VMEM((QB, KT), F32),
                        pltpu.VMEM((N_HEADS, QB, 1), F32),
                        pltpu.VMEM((N_HEADS, QB, LANES), F32),
                        pltpu.VMEM((QB, LANES), I32)],
        compiler_params=_cparams(("parallel", "arbitrary")),
        name="dsa_main",
    )(qr, iqr, iws, proj, iktr, ktr, va)


def _fold_rows(x, op):
    parts = x.reshape(x.shape[0] // SUBLANES, SUBLANES, x.shape[1])
    while parts.shape[0] > 1:
        half = parts.shape[0] // 2
        parts = op(parts[:half], parts[half:])
    return parts[0]


def _dsa_prep_t_kernel(q_ref, iq_ref, k_ref, v_ref, idx_ref, cos_ref, sin_ref, nw_ref, nb_ref,
                       qt_ref, iqt_ref, ko_ref, vat_ref, iko_ref, iwt_ref):
    QB = DSA_QB
    cos = cos_ref[...]
    sin = sin_ref[...]
    nblk = q_ref.shape[0] // QB
    qt = (_rope(q_ref[...], cos, sin) * (HEAD_DIM ** -0.5)).T
    iqt = _rope(iq_ref[...], cos, sin).T
    for j in range(nblk):
        qt_ref[j] = qt[:, j * QB:(j + 1) * QB].astype(BF16)
        iqt_ref[j] = iqt[:, j * QB:(j + 1) * QB].astype(BF16)
    ko_ref[...] = _rope(k_ref[...], cos, sin).astype(BF16)
    vt = v_ref[...].T
    ones = jnp.ones((HEAD_DIM, vt.shape[1]), F32)
    vat_ref[0] = jnp.concatenate(
        [t for g in range(ATT_KV_HEADS) for t in (vt[g * HEAD_DIM:(g + 1) * HEAD_DIM], ones)], axis=0).astype(BF16)
    idx = idx_ref[...]
    lane = lax.broadcasted_iota(I32, idx.shape, 1)
    is_k = lane < IDX_DIM
    mu = jnp.sum(jnp.where(is_k, idx, 0.0), axis=-1, keepdims=True) * (1.0 / IDX_DIM)
    dk = jnp.where(is_k, idx - mu, 0.0)
    var = jnp.sum(dk * dk, axis=-1, keepdims=True) * (1.0 / IDX_DIM)
    ikn = dk * lax.rsqrt(var + NORM_EPS) * nw_ref[...] + nb_ref[...]
    iko_ref[...] = _rope(ikn, cos, sin).astype(BF16)
    iwt = (pltpu.roll(idx, LANES - IDX_DIM, 1) * (IDX_HEADS ** -0.5 * IDX_DIM ** -0.5)).T
    for j in range(nblk):
        iwt_ref[j] = iwt[:IDX_HEADS, j * QB:(j + 1) * QB]


def _dsa_prep_t(proj, cos, sin, k_norm_w, k_norm_b):
    T = proj.shape[0]
    tq = DSA_KT
    QB = DSA_QB
    D = BRANCH_D
    VA = 2 * ATT_KV_D
    nb = tq // QB
    spec = lambda w, off: pl.BlockSpec((tq, w), lambda i: (i, off // w))
    out = lambda w: pl.BlockSpec((tq, w), lambda i: (i, 0))
    blk = lambda n: pl.BlockSpec((nb, n, QB), lambda i: (i, 0, 0))
    return pl.pallas_call(
        _dsa_prep_t_kernel,
        grid=(T // tq,),
        in_specs=[spec(D, OFF_QC), spec(D, OFF_IQ), spec(ATT_KV_D, OFF_KC), spec(ATT_KV_D, OFF_VC),
                  spec(LANES, OFF_IDX), out(LANES), out(LANES),
                  pl.BlockSpec((1, LANES), lambda i: (0, 0)), pl.BlockSpec((1, LANES), lambda i: (0, 0))],
        out_specs=[blk(D), blk(D), out(ATT_KV_D), pl.BlockSpec((1, VA, tq), lambda i: (i, 0, 0)), out(LANES),
                   blk(IDX_HEADS)],
        out_shape=[jax.ShapeDtypeStruct((T // QB, D, QB), BF16), jax.ShapeDtypeStruct((T // QB, D, QB), BF16),
                   jax.ShapeDtypeStruct((T, ATT_KV_D), BF16), jax.ShapeDtypeStruct((T // tq, VA, tq), BF16),
                   jax.ShapeDtypeStruct((T, LANES), BF16), jax.ShapeDtypeStruct((T // QB, IDX_HEADS, QB), F32)],
        compiler_params=_cparams(("parallel",)),
        name="dsa_prep",
    )(proj, proj, proj, proj, proj, cos, sin, _pad_lanes(k_norm_w), _pad_lanes(k_norm_b))


def _dsa_main_t_kernel(top_k, seq_len, qt_ref, iqt_ref, iwt_ref, gate_ref, ik_ref, k_ref, vat_ref, o_ref,
                       keys_ref, sacc_ref, bias_ref, m_ref, acc_ref, tie_ref, ot_ref):
    QB, KT = DSA_QB, DSA_KT
    qi = pl.program_id(1)
    nk = (qi * QB + QB + KT - 1) // KT
    int_min = jnp.int32(INT_MIN)
    rep = N_HEADS // ATT_KV_HEADS
    kpos0 = lax.broadcasted_iota(I32, (KT, QB), 0)
    qpos = qi * QB + lax.broadcasted_iota(I32, (KT, QB), 1)

    def score_tile(kt, _):
        off = pl.multiple_of(kt * KT, KT)
        ik_t = ik_ref[pl.ds(off, KT), :][:, :IDX_DIM]
        for h0 in range(0, IDX_HEADS, 4):
            lg = [jnp.dot(ik_t, iqt_ref[h * IDX_DIM:(h + 1) * IDX_DIM, :], preferred_element_type=F32)
                  for h in range(h0, h0 + 4)]
            term = sum(jnp.maximum(l, 0.0) * iwt_ref[h:h + 1, :] for l, h in zip(lg, range(h0, h0 + 4)))
            if h0 == 0:
                sacc_ref[...] = term
            else:
                sacc_ref[...] += term
        bits = pltpu.bitcast(sacc_ref[...], I32)
        skey = jnp.where(bits < 0, bits ^ jnp.int32(0x7FFFFFFF), bits)
        keys_ref[kt] = jnp.where(kpos0 + off <= qpos, skey, int_min)
        return 0

    lax.fori_loop(0, nk, score_tile, 0)

    def count(pred):
        def body(kt, acc):
            hit = pred(keys_ref[kt], kpos0 + kt * KT)
            return acc + _fold_rows(jnp.where(hit, 1.0, 0.0), jnp.add)
        acc = lax.fori_loop(0, nk, body, jnp.zeros((SUBLANES, QB), F32))
        return jnp.sum(acc, axis=0, keepdims=True)

    def bit_body(i, res):
        cand_u = res | jnp.left_shift(jnp.int32(1), 31 - i)
        cand = cand_u ^ int_min
        cnt = count(lambda k, c: k >= cand)
        return jnp.where(cnt >= float(top_k), cand_u, res)

    res = lax.fori_loop(0, 32, bit_body, jnp.zeros((1, QB), I32))
    thr = jnp.maximum(res ^ int_min, jnp.int32(INT_MIN + 1))

    n_ge = count(lambda k, c: k >= thr)
    tie_ref[...] = jnp.full((1, QB), seq_len, I32)

    @pl.when(jnp.max(n_ge) > float(top_k))
    def _():
        need = float(top_k) - count(lambda k, c: k > thr)
        n_bits = max(1, (seq_len - 1).bit_length())

        def tie_body(i, x):
            cand = x | jnp.left_shift(jnp.int32(1), n_bits - 1 - i)
            below = count(lambda k, c: (k == thr) & (c < cand))
            return jnp.where(below < need, cand, x)

        tie_ref[...] = lax.fori_loop(0, n_bits, tie_body, jnp.zeros((1, QB), I32))

    tie = tie_ref[...]

    m_ref[...] = jnp.full(m_ref.shape, -1e29, F32)
    acc_ref[...] = jnp.zeros(acc_ref.shape, F32)

    def att_tile(kt, _):
        off = pl.multiple_of(kt * KT, KT)
        kk = keys_ref[kt]
        sel = (kk > thr) | ((kk == thr) & (kpos0 + off <= tie))
        bias_ref[...] = jnp.where(sel, 0.0, -1e30)
        k_t = k_ref[pl.ds(off, KT), :]
        for g in range(ATT_KV_HEADS):
            k_g = k_t[:, g * HEAD_DIM:(g + 1) * HEAD_DIM]
            va_g = vat_ref[kt, g * LANES:(g + 1) * LANES, :]
            hs = [g * rep + r for r in range(rep)]
            lg = [jnp.dot(k_g, qt_ref[h * HEAD_DIM:(h + 1) * HEAD_DIM, :], preferred_element_type=F32)
                  + bias_ref[...] for h in hs]
            m_old = [m_ref[h] for h in hs]
            m_new = [jnp.maximum(mo, jnp.max(_fold_rows(l, jnp.maximum), axis=0, keepdims=True))
                     for mo, l in zip(m_old, lg)]
            p = [jnp.exp(l - mn).astype(BF16) for l, mn in zip(lg, m_new)]
            pv = [jnp.dot(va_g, pp, preferred_element_type=F32) for pp in p]
            for j, h in enumerate(hs):
                acc_ref[h] = acc_ref[h] * jnp.exp(m_old[j] - m_new[j]) + pv[j]
                m_ref[h] = m_new[j]
        return 0

    lax.fori_loop(0, nk, att_tile, 0)

    for h in range(N_HEADS):
        a = acc_ref[h]
        ot_ref[h * HEAD_DIM:(h + 1) * HEAD_DIM, :] = a[:HEAD_DIM] / a[HEAD_DIM:]
    gate = gate_ref[...]
    o_ref[...] = (ot_ref[...].T * (gate * _sigmoid(gate))).astype(BF16)


def _dsa_main_t(proj, qt, iqt, kr, vat, ikr, iwt, B, S, top_k):
    QB, KT = DSA_QB, DSA_KT
    nq = S // QB
    nkt = S // KT
    D = BRANCH_D
    row = lambda b, i: b * nq + i
    return pl.pallas_call(
        functools.partial(_dsa_main_t_kernel, top_k, S),
        grid=(B, nq),
        in_specs=[pl.BlockSpec((None, D, QB), lambda b, i: (row(b, i), 0, 0)),
                  pl.BlockSpec((None, D, QB), lambda b, i: (row(b, i), 0, 0)),
                  pl.BlockSpec((None, IDX_HEADS, QB), lambda b, i: (row(b, i), 0, 0)),
                  pl.BlockSpec((QB, D), lambda b, i: (row(b, i), OFF_GC // D)),
                  pl.BlockSpec((S, LANES), lambda b, i: (b, 0)),
                  pl.BlockSpec((S, ATT_KV_D), lambda b, i: (b, 0)),
                  pl.BlockSpec((nkt, 2 * ATT_KV_D, KT), lambda b, i: (b, 0, 0))],
        out_specs=pl.BlockSpec((QB, D), lambda b, i: (row(b, i), 0)),
        out_shape=jax.ShapeDtypeStruct((B * S, D), BF16),
        scratch_shapes=[pltpu.VMEM((nkt, KT, QB), I32),
                        pltpu.VMEM((KT, QB), F32),
                        pltpu.VMEM((KT, QB), F32),
                        pltpu.VMEM((N_HEADS, 1, QB), F32),
                        pltpu.VMEM((N_HEADS, LANES, QB), F32),
                        pltpu.VMEM((1, QB), I32),
                        pltpu.VMEM((D, QB), F32)],
        compiler_params=_cparams(("parallel", "arbitrary")),
        name="dsa_main",
    )(qt, iqt, iwt, proj, ikr, kr, vat)


def _dsa_branch(proj, cos, sin, B, S, k_norm_w, k_norm_b, top_k):
    qt, iqt, kr, vat, ikr, iwt = _dsa_prep_t(proj, cos, sin, k_norm_w, k_norm_b)
    return _dsa_main_t(proj, qt, iqt, kr, vat, ikr, iwt, B, S, top_k)


def _merge_kernel(ya_ref, yb_ref, yc_ref, ga_ref, gb_ref, gc_ref, ba_ref, bb_ref, bc_ref,
                  wa_ref, wb_ref, wc_ref, o_ref):
    acc = _sigmoid(ga_ref[...] + ba_ref[...]) * jnp.dot(ya_ref[...], wa_ref[...], preferred_element_type=F32)
    acc += _sigmoid(gb_ref[...] + bb_ref[...]) * jnp.dot(yb_ref[...], wb_ref[...], preferred_element_type=F32)
    acc += _sigmoid(gc_ref[...] + bc_ref[...]) * jnp.dot(yc_ref[...], wc_ref[...], preferred_element_type=F32)
    o_ref[...] = acc.astype(BF16)


def _merge(ya, yb, yc, proj, b_gate, wa, wb, wc):
    T = ya.shape[0]
    D, Db = D_MODEL, BRANCH_D
    tm = min(512, T)
    tn = 1024
    nj = D // tn
    y = pl.BlockSpec((tm, Db), lambda i, j: (i, 0))
    gate = lambda n: pl.BlockSpec((tm, tn), lambda i, j: (i, (OFF_GATES + n * D) // tn + j))
    bias = lambda n: pl.BlockSpec((1, tn), lambda i, j: (0, n * nj + j))
    w = pl.BlockSpec((Db, tn), lambda i, j: (0, j))
    bg = b_gate[None, :]
    return pl.pallas_call(
        _merge_kernel,
        grid=(T // tm, nj),
        in_specs=[y, y, y, gate(0), gate(1), gate(2), bias(0), bias(1), bias(2), w, w, w],
        out_specs=pl.BlockSpec((tm, tn), lambda i, j: (i, j)),
        out_shape=jax.ShapeDtypeStruct((T, D), BF16),
        compiler_params=_cparams(("parallel", "arbitrary")),
        name="gated_merge",
    )(ya, yb, yc, proj, proj, proj, bg, bg, bg, wa, wb, wc)


def _out_proj_kernel(m_ref, w_ref, g_ref, x_ref, o_ref):
    y = jnp.dot(m_ref[...], w_ref[...], preferred_element_type=F32)
    ms = jnp.mean(y * y, axis=-1, keepdims=True)
    o_ref[...] = x_ref[...] + y * lax.rsqrt(ms + NORM_EPS) * g_ref[...]


def _out_proj(merged, w_out, g, x):
    T, D = x.shape
    tm = min(256, T)
    return pl.pallas_call(
        _out_proj_kernel,
        grid=(T // tm,),
        in_specs=[pl.BlockSpec((tm, D), lambda i: (i, 0)), pl.BlockSpec((D, D), lambda i: (0, 0)),
                  pl.BlockSpec((1, D), lambda i: (0, 0)), pl.BlockSpec((tm, D), lambda i: (i, 0))],
        out_specs=pl.BlockSpec((tm, D), lambda i: (i, 0)),
        out_shape=jax.ShapeDtypeStruct((T, D), F32),
        compiler_params=_cparams(("parallel",)),
        name="out_proj_norm_residual",
    )(merged, w_out, g, x)


def _pack_w_in(w_in):
    a0, b0 = 0, A_COLS
    c0 = A_COLS + B_COLS
    g0 = c0 + C_COLS
    D = BRANCH_D
    w_in = w_in.astype(BF16)
    sl = lambda s, n: w_in[:, :, s:s + n]
    zeros = lambda n: jnp.zeros(w_in.shape[:2] + (n,), w_in.dtype)
    pieces = [
        sl(a0 + D, SSM_CONV_DIM),
        sl(a0, D),
        sl(b0, D),
        sl(b0 + D + RWKV_LORA, D),
        sl(b0 + 2 * D + RWKV_LORA, D),
        sl(b0 + RWKV_SHIFT_DIM, D),
        sl(c0, D),
        sl(c0 + D + 2 * ATT_KV_D, D),
        sl(c0 + 2 * D + 2 * ATT_KV_D, D),
        sl(g0, GATE_COLS),
        sl(c0 + D, ATT_KV_D),
        sl(c0 + D + ATT_KV_D, ATT_KV_D),
        sl(a0 + D + SSM_CONV_DIM, N_HEADS), zeros(LANES - N_HEADS),
        sl(b0 + D, RWKV_LORA), sl(b0 + 3 * D + RWKV_LORA, RWKV_LORA),
        sl(c0 + 3 * D + 2 * ATT_KV_D, IDX_DIM + IDX_HEADS), zeros(LANES - IDX_DIM - IDX_HEADS),
    ]
    pieces.append(zeros(NP_COLS - sum(p.shape[-1] for p in pieces)))
    return jnp.concatenate(pieces, axis=-1)


def kernel(x, positions, pre_norm, post_norm, w_in, b_gate, ssm_conv_w, ssm_conv_b, ssm_dt_bias, ssm_a_log,
           ssm_d, ssm_norm, rwkv_mu, rwkv_w0, rwkv_w2, rwkv_a0, rwkv_a2, rwkv_k_k, rwkv_k_a, rwkv_r_k,
           rwkv_ln_w, rwkv_ln_b, idx_k_norm_w, idx_k_norm_b, w_branch_a, w_branch_b, w_branch_c, w_out):
    B, S, D = x.shape
    depth = w_in.shape[0]
    top_k = min(TOPK_MAX, S // 4)
    xt = x.reshape(B * S, D)
    w_in_p = _pack_w_in(w_in)
    wa, wb, wc, wo = (w.astype(BF16) for w in (w_branch_a, w_branch_b, w_branch_c, w_out))
    cos, sin = _rope_tables(positions)
    for i in range(depth):
        proj = _norm_matmul(xt, pre_norm[i][None, :], w_in_p[i])
        ya = _ssd_branch(proj, B, S, ssm_conv_w[i], ssm_conv_b[i], ssm_dt_bias[i], ssm_a_log[i], ssm_d[i], ssm_norm[i])
        yb = _rwkv_branch(proj, B, S, rwkv_mu[i], rwkv_w0[i], rwkv_w2[i], rwkv_a0[i], rwkv_a2[i], rwkv_k_k[i],
                          rwkv_k_a[i], rwkv_r_k[i], rwkv_ln_w[i], rwkv_ln_b[i])
        yc = _dsa_branch(proj, cos, sin, B, S, idx_k_norm_w[i], idx_k_norm_b[i], top_k)
        merged = _merge(ya, yb, yc, proj, b_gate[i], wa[i], wb[i], wc[i])
        xt = _out_proj(merged, wo[i], post_norm[i][None, :], xt)
    return xt.reshape(B, S, D)
```

```python
import functools
import math

import numpy as np
import jax
import jax.numpy as jnp
from jax import lax
from jax.experimental import pallas as pl
from jax.experimental.pallas import tpu as pltpu

F32 = jnp.float32
BF16 = jnp.bfloat16
I32 = jnp.int32
I16 = jnp.int16
HI = lax.Precision.HIGHEST

D_MODEL = 2048
BRANCH_D = D_MODEL // 2
NORM_EPS = 1e-6
HEAD_DIM = 64
N_HEADS = BRANCH_D // HEAD_DIM

SSM_GROUPS = 4
SSM_STATE = 128
SSM_CONV = 4
SSD_CHUNK = 128
SSM_CONV_DIM = BRANCH_D + 2 * SSM_GROUPS * SSM_STATE
A_COLS = BRANCH_D + SSM_CONV_DIM + N_HEADS

RWKV_LORA = 64
RWKV_SHIFT_DIM = 3 * BRANCH_D + 2 * RWKV_LORA
B_COLS = RWKV_SHIFT_DIM + BRANCH_D
RWKV_GN_EPS = 64e-5
RWKV_CHUNK = 64

ATT_KV_HEADS = 4
ATT_KV_D = ATT_KV_HEADS * HEAD_DIM
IDX_HEADS = 16
IDX_DIM = 64
TOPK_MAX = 256
Q_BLOCK = 128
ROPE_THETA = 10000.0
C_COLS = BRANCH_D + 2 * ATT_KV_D + BRANCH_D + IDX_HEADS * IDX_DIM + IDX_DIM + IDX_HEADS
GATE_COLS = 3 * D_MODEL

LANES = 128
SUBLANES = 8
VMEM_LIMIT = 56 * 1024 * 1024

OFF_XBC = 0
OFF_Z = 2048
OFF_RB = 3072
OFF_KB = 4096
OFF_VB = 5120
OFF_GB = 6144
OFF_QC = 7168
OFF_GC = 8192
OFF_IQ = 9216
OFF_GATES = 10240
OFF_KC = 16384
OFF_VC = 16640
OFF_DT = 16896
OFF_LORA = 17024
OFF_IDX = 17152
NP_COLS = 17408

INT_MIN = -2 ** 31


def _cparams(sem):
    return pltpu.CompilerParams(dimension_semantics=sem, vmem_limit_bytes=VMEM_LIMIT)


def _mm(a, b):
    return jnp.dot(a.astype(BF16), b.astype(BF16), preferred_element_type=F32)


def _mm_nt(a, b):
    return lax.dot_general(a.astype(BF16), b.astype(BF16), (((1,), (1,)), ((), ())),
                           preferred_element_type=F32)


def _mm_tn(a, b):
    return lax.dot_general(a.astype(BF16), b.astype(BF16), (((0,), (0,)), ((), ())),
                           preferred_element_type=F32)


def _mm_hi(a, b, exact="b"):
    x, m = (a, b) if exact == "b" else (b, a)
    m = m.astype(BF16)
    acc = None
    for _ in range(3):
        xb = x.astype(BF16)
        part = jnp.dot(xb, m, preferred_element_type=F32) if exact == "b" else jnp.dot(m, xb, preferred_element_type=F32)
        acc = part if acc is None else acc + part
        x = x - xb.astype(F32)
    return acc


def _head_expand():
    r = lax.broadcasted_iota(I32, (LANES, BRANCH_D), 0)
    c = lax.broadcasted_iota(I32, (LANES, BRANCH_D), 1)
    return jnp.where((c >> 6) == r, 1.0, 0.0).astype(F32)


def _head_reduce():
    r = lax.broadcasted_iota(I32, (BRANCH_D, LANES), 0)
    c = lax.broadcasted_iota(I32, (BRANCH_D, LANES), 1)
    return jnp.where((r >> 6) == c, 1.0, 0.0).astype(F32)


def _tril(n, strict=False):
    r = lax.broadcasted_iota(I32, (n, n), 0)
    c = lax.broadcasted_iota(I32, (n, n), 1)
    return (c < r) if strict else (c <= r)


def _softplus(x):
    return jnp.maximum(x, 0.0) + jnp.log1p(jnp.exp(-jnp.abs(x)))


def _sigmoid(x):
    return 1.0 / (1.0 + jnp.exp(-x))


def _shift_rows(x, carry, s):
    rolled = pltpu.roll(x, s, 0)
    rows = lax.broadcasted_iota(I32, (SUBLANES, x.shape[1]), 0)
    top = jnp.where(rows < s, pltpu.roll(carry, s, 0), rolled[0:SUBLANES])
    return jnp.concatenate([top, rolled[SUBLANES:]], axis=0)


def _norm_matmul_kernel(x_ref, g_ref, w_ref, o_ref, xn_ref):
    @pl.when(pl.program_id(1) == 0)
    def _():
        x = x_ref[...]
        ms = jnp.mean(x * x, axis=-1, keepdims=True)
        xn_ref[...] = (x * lax.rsqrt(ms + NORM_EPS) * g_ref[...]).astype(BF16)

    o_ref[...] = jnp.dot(xn_ref[...], w_ref[...], preferred_element_type=F32)


def _norm_matmul(x, g, w, layer):
    T, D = x.shape
    Np = w.shape[2]
    tm = min(1024, T)
    tn = 1024
    return pl.pallas_call(
        _norm_matmul_kernel,
        grid=(T // tm, Np // tn),
        in_specs=[pl.BlockSpec((tm, D), lambda i, j: (i, 0)),
                  pl.BlockSpec((1, D), lambda i, j: (0, 0)),
                  pl.BlockSpec((None, D, tn), lambda i, j: (layer, 0, j))],
        out_specs=pl.BlockSpec((tm, tn), lambda i, j: (i, j)),
        out_shape=jax.ShapeDtypeStruct((T, Np), F32),
        scratch_shapes=[pltpu.VMEM((tm, D), BF16)],
        compiler_params=_cparams(("parallel", "arbitrary")),
        name="norm_in_proj",
    )(x, g, w)


def _ssd_kernel(xbc_ref, z_ref, dt_ref, cw_ref, cb_ref, dtb_ref, alog_ref, dskip_ref, nw_ref,
                o_ref, carry_ref, h_ref, y_ref):
    Q = SSD_CHUNK
    GN = SSM_GROUPS * SSM_STATE
    GP = BRANCH_D // SSM_GROUPS

    @pl.when(pl.program_id(1) == 0)
    def _():
        carry_ref[...] = jnp.zeros_like(carry_ref)
        h_ref[...] = jnp.zeros_like(h_ref)

    x = xbc_ref[...]
    carry = carry_ref[...]
    cw = cw_ref[...]
    acc = x * cw[SSM_CONV - 1:SSM_CONV] + cb_ref[...]
    for s in range(1, SSM_CONV):
        acc = acc + _shift_rows(x, carry, s) * cw[SSM_CONV - 1 - s:SSM_CONV - s]
    carry_ref[...] = x[Q - SUBLANES:Q]
    xc = acc * _sigmoid(acc)
    xs = xc[:, :BRANCH_D]
    bm = xc[:, BRANCH_D:BRANCH_D + GN]
    cm = xc[:, BRANCH_D + GN:]

    dt = _softplus(dt_ref[...] + dtb_ref[...])
    adt = dt * (-jnp.exp(alog_ref[...]))
    a_cs = _mm_hi(_tril(Q).astype(F32), adt, exact="a")
    a_cs_t = a_cs.T
    a_last = a_cs[Q - 1:Q]
    ex = _head_expand()
    dt_e = _mm_hi(dt, ex)
    ecs_e = _mm_hi(jnp.exp(a_cs), ex)
    ds_e = _mm_hi(jnp.exp(a_last - a_cs), ex)
    cd_e = _mm_hi(jnp.exp(a_last), ex)
    xd = xs * dt_e
    xds = xd * ds_e
    causal = _tril(Q)

    G = range(SSM_GROUPS)
    H = range(N_HEADS)
    rep = N_HEADS // SSM_GROUPS
    bm_b, cm_b, xd_b, xds_b = (t.astype(BF16) for t in (bm, cm, xd, xds))
    bm_g = [bm_b[:, g * SSM_STATE:(g + 1) * SSM_STATE] for g in G]
    cm_g = [cm_b[:, g * SSM_STATE:(g + 1) * SSM_STATE] for g in G]
    hg = [h_ref[g] for g in G]
    cb = [_mm_nt(cm_g[g], bm_g[g]) for g in G]
    lmat = [jnp.exp(jnp.where(causal, a_cs[:, h:h + 1] - a_cs_t[h:h + 1, :], -1e30)) for h in H]
    y_diag = [_mm(cb[h // rep] * lmat[h], xd_b[:, h * HEAD_DIM:(h + 1) * HEAD_DIM]) for h in H]
    y_off = [_mm(cm_g[g], hg[g]) * ecs_e[:, g * GP:(g + 1) * GP] for g in G]
    h_new = [hg[g] * cd_e[:, g * GP:(g + 1) * GP] + _mm_tn(bm_g[g], xds_b[:, g * GP:(g + 1) * GP]) for g in G]
    for g in G:
        h_ref[g] = h_new[g]
        y_ref[:, g * GP:(g + 1) * GP] = y_off[g]
    for h in H:
        y_ref[:, h * HEAD_DIM:(h + 1) * HEAD_DIM] += y_diag[h]

    y = y_ref[...] + dskip_ref[...] * xs
    z = z_ref[...]
    yz = y * (z * _sigmoid(z))
    nw = nw_ref[...]
    for g in range(SSM_GROUPS):
        seg = yz[:, g * GP:(g + 1) * GP]
        ms = jnp.mean(seg * seg, axis=-1, keepdims=True)
        o_ref[:, g * GP:(g + 1) * GP] = (seg * lax.rsqrt(ms + NORM_EPS) * nw[:, g * GP:(g + 1) * GP]).astype(BF16)


def _pad_lanes(v, n=LANES):
    return jnp.zeros((1, n), F32).at[0, :v.shape[0]].set(v.astype(F32))


def _ssd_branch(proj, B, S, conv_w, conv_b, dt_bias, a_log, d_skip, norm_w):
    Q = SSD_CHUNK
    nc = S // Q
    row = lambda b, c: b * nc + c
    full = lambda shape: pl.BlockSpec(shape, lambda b, c: (0, 0))
    return pl.pallas_call(
        _ssd_kernel,
        grid=(B, nc),
        in_specs=[pl.BlockSpec((Q, SSM_CONV_DIM), lambda b, c: (row(b, c), OFF_XBC // SSM_CONV_DIM)),
                  pl.BlockSpec((Q, BRANCH_D), lambda b, c: (row(b, c), OFF_Z // BRANCH_D)),
                  pl.BlockSpec((Q, LANES), lambda b, c: (row(b, c), OFF_DT // LANES)),
                  full((SSM_CONV, SSM_CONV_DIM)), full((1, SSM_CONV_DIM)), full((1, LANES)), full((1, LANES)),
                  full((1, BRANCH_D)), full((1, BRANCH_D))],
        out_specs=pl.BlockSpec((Q, BRANCH_D), lambda b, c: (row(b, c), 0)),
        out_shape=jax.ShapeDtypeStruct((B * S, BRANCH_D), BF16),
        scratch_shapes=[pltpu.VMEM((SUBLANES, SSM_CONV_DIM), F32),
                        pltpu.VMEM((SSM_GROUPS, SSM_STATE, BRANCH_D // SSM_GROUPS), F32),
                        pltpu.VMEM((Q, BRANCH_D), F32)],
        compiler_params=_cparams(("parallel", "arbitrary")),
        name="ssd_branch",
    )(proj, proj, proj, conv_w, conv_b[None, :], _pad_lanes(dt_bias), _pad_lanes(a_log),
      jnp.repeat(d_skip, HEAD_DIM)[None, :], norm_w[None, :])


def _mix(x, carry_ref, mu):
    rolled = pltpu.roll(x, 1, 0)
    rows = lax.broadcasted_iota(I32, x.shape, 0)
    prev = jnp.where(rows == 0, carry_ref[SUBLANES - 1:SUBLANES], rolled)
    carry_ref[...] = x[x.shape[0] - SUBLANES:]
    return x + (prev - x) * mu


def _rwkv_kernel(r_ref, k_ref, v_ref, g_ref, lo_ref, mu_r_ref, mu_k_ref, mu_v_ref, mu_lo_ref,
                 w0_ref, w2_ref, a0_ref, a2_ref, kk_ref, ka_ref, rk_ref, lnw_ref, lnb_ref,
                 o_ref, cr_ref, ck_ref, cv_ref, clo_ref, s_ref, y_ref):
    C = RWKV_CHUNK

    @pl.when(pl.program_id(1) == 0)
    def _():
        for ref in (cr_ref, ck_ref, cv_ref, clo_ref, s_ref):
            ref[...] = jnp.zeros_like(ref)

    r = _mix(r_ref[...], cr_ref, mu_r_ref[...])
    k = _mix(k_ref[...], ck_ref, mu_k_ref[...])
    v = _mix(v_ref[...], cv_ref, mu_v_ref[...])
    lo = _mix(lo_ref[...], clo_ref, mu_lo_ref[...])
    wd = lo[:, :RWKV_LORA]
    ad = lo[:, RWKV_LORA:]

    wlog = -_softplus(-(w0_ref[...] + _mm(jnp.tanh(wd), w2_ref[...]))) - 0.5
    lw = -jnp.exp(wlog)
    a = _sigmoid(a0_ref[...] + _mm(ad, a2_ref[...]))

    ex = _head_expand()
    red = _head_reduce()
    kk = k * kk_ref[...]
    nrm = jnp.maximum(jnp.sqrt(_mm_hi(kk * kk, red)), 1e-12)
    kk = kk * _mm_hi(1.0 / nrm, ex)
    kmod = k * (1.0 + (a - 1.0) * ka_ref[...])
    b = kk * a

    cum = _mm_hi(_tril(C).astype(F32), lw, exact="a")
    gam = jnp.exp(cum)
    igam = jnp.exp(-cum)
    at = jnp.exp(cum - lw) * kk
    bt = b * igam
    kt = kmod * igam
    rt = r * gam
    g_last = gam[C - 1:C]

    H = range(N_HEADS)
    P = range(N_HEADS // 2)
    lane = lax.broadcasted_iota(I32, (C, LANES), 1)
    rowi = lax.broadcasted_iota(I32, (C, LANES), 0)
    low = lane < HEAD_DIM
    lane_in = lane & (HEAD_DIM - 1)
    strict2 = lane_in < rowi
    incl2 = lane_in <= rowi
    eye2 = jnp.where(lane_in == rowi, 1.0, 0.0)
    low2 = jnp.concatenate([low, low], axis=0)
    zeros_b = jnp.zeros((C, LANES), BF16)
    cols = lambda t: [t[:, p * LANES:(p + 1) * LANES] for p in P]
    at_c, bt_c, kt_c, rt_c, v_c = (cols(t.astype(BF16)) for t in (at, bt, kt, rt, v))
    rt_f = cols(rt)
    nat = lambda h, a, b: jnp.where(low, a, b) if h % 2 == 0 else jnp.where(low, b, a)
    nat2 = lambda h, a, b: jnp.where(low2, a, b) if h % 2 == 0 else jnp.where(low2, b, a)
    rows = lambda top, bot: jnp.concatenate([top, bot], axis=0)
    nat_rows = lambda h, t: rows(t, zeros_b) if h % 2 == 0 else rows(zeros_b, t)
    oth_rows = lambda h, t: rows(zeros_b, t) if h % 2 == 0 else rows(t, zeros_b)

    p1 = [_mm_nt(rows(nat(h, at_c[h // 2], zeros_b), nat(h, rt_c[h // 2], zeros_b)),
                 rows(bt_c[h // 2], kt_c[h // 2]) if h % 2 == 0 else rows(kt_c[h // 2], bt_c[h // 2])) for h in H]
    top = [jnp.where(strict2, p[:C], 0.0) for p in p1]
    bot = [jnp.where(incl2, p[C:], 0.0) for p in p1]
    tp = [nat(h, -top[h], eye2) for h in H]
    for _ in range(int(math.log2(C))):
        tp_b = [t.astype(BF16) for t in tp]
        out = [_mm(tp_b[h], nat_rows(h, tp_b[h])) for h in H]
        tp = [out[h] + nat(h, 0.0, tp[h]) for h in H]
    tp_b = [t.astype(BF16) for t in tp]
    top_b = [t.astype(BF16) for t in top]
    bot_b = [t.astype(BF16) for t in bot]
    xz = [_mm(tp_b[h], oth_rows(h, nat(h, at_c[h // 2], top_b[h]))) for h in H]
    xz_b = [t.astype(BF16) for t in xz]
    mxz = [_mm(bot_b[h], nat_rows(h, xz_b[h])) for h in H]
    ra = [nat(h, rt_f[h // 2], bot[h]) - mxz[h] for h in H]
    stack = [rows(ra[h], xz[h]).astype(BF16) for h in H]
    s_b = [s_ref[p].astype(BF16) for p in P]
    yw = [_mm_nt(nat2(h, stack[h], jnp.zeros_like(stack[h])), s_b[h // 2]) + _mm(stack[h], oth_rows(h, v_c[h // 2]))
          for h in H]
    yw_p = [jnp.where(low2, yw[2 * p], yw[2 * p + 1]) for p in P]
    blockdiag = (lax.broadcasted_iota(I32, (LANES, LANES), 0) < HEAD_DIM) == (
        lax.broadcasted_iota(I32, (LANES, LANES), 1) < HEAD_DIM)
    inc = [_mm_tn(rows(v_c[p], (-yw_p[p][C:]).astype(BF16)), rows(kt_c[p], bt_c[p])) for p in P]
    for p in P:
        y_ref[:, p * LANES:(p + 1) * LANES] = yw_p[p][:C]
        s_ref[p] = (s_ref[p] + jnp.where(blockdiag, inc[p], 0.0)) * g_last[:, p * LANES:(p + 1) * LANES]

    y = y_ref[...]
    inv_n = 1.0 / HEAD_DIM
    mean = _mm_hi(_mm_hi(y, red) * inv_n, ex)
    yc = y - mean
    var = _mm_hi(yc * yc, red) * inv_n
    yn = yc * _mm_hi(lax.rsqrt(var + RWKV_GN_EPS), ex) * lnw_ref[...] + lnb_ref[...]
    bonus = _mm_hi(_mm_hi(r * kmod * rk_ref[...], red), ex) * v
    gate = g_ref[...]
    o_ref[...] = ((yn + bonus) * (gate * _sigmoid(gate))).astype(BF16)


def _rwkv_branch(proj, B, S, mu, w0, w2, a0, a2, k_k, k_a, r_k, ln_w, ln_b):
    C = RWKV_CHUNK
    nc = S // C
    D = BRANCH_D
    row = lambda b, c: b * nc + c
    full = lambda shape: pl.BlockSpec(shape, lambda b, c: (0, 0))
    slab = lambda off: pl.BlockSpec((C, D), lambda b, c: (row(b, c), off // D))
    mu_r, mu_wd, mu_k, mu_v, mu_ad = (mu[:D], mu[D:D + RWKV_LORA], mu[D + RWKV_LORA:2 * D + RWKV_LORA],
                                      mu[2 * D + RWKV_LORA:3 * D + RWKV_LORA], mu[3 * D + RWKV_LORA:])
    r1 = lambda t: t.reshape(1, -1).astype(F32)
    return pl.pallas_call(
        _rwkv_kernel,
        grid=(B, nc),
        in_specs=[slab(OFF_RB), slab(OFF_KB), slab(OFF_VB), slab(OFF_GB),
                  pl.BlockSpec((C, LANES), lambda b, c: (row(b, c), OFF_LORA // LANES)),
                  full((1, D)), full((1, D)), full((1, D)), full((1, LANES)),
                  full((1, D)), full((RWKV_LORA, D)), full((1, D)), full((RWKV_LORA, D)),
                  full((1, D)), full((1, D)), full((1, D)), full((1, D)), full((1, D))],
        out_specs=pl.BlockSpec((C, D), lambda b, c: (row(b, c), 0)),
        out_shape=jax.ShapeDtypeStruct((B * S, D), BF16),
        scratch_shapes=[pltpu.VMEM((SUBLANES, D), F32), pltpu.VMEM((SUBLANES, D), F32),
                        pltpu.VMEM((SUBLANES, D), F32), pltpu.VMEM((SUBLANES, LANES), F32),
                        pltpu.VMEM((N_HEADS // 2, LANES, LANES), F32),
                        pltpu.VMEM((C, D), F32)],
        compiler_params=_cparams(("parallel", "arbitrary")),
        name="rwkv_branch",
    )(proj, proj, proj, proj, proj, r1(mu_r), r1(mu_k), r1(mu_v), r1(jnp.concatenate([mu_wd, mu_ad])),
      r1(w0), w2, r1(a0), a2, r1(k_k), r1(k_a), r1(r_k), r1(ln_w), r1(ln_b))


def _rope_table_kernel(pos_ref, inv_ref, cos_ref, sin_ref):
    ang = pos_ref[...] * inv_ref[...]
    lane = lax.broadcasted_iota(I32, ang.shape, 1)
    cos_ref[...] = jnp.cos(ang)
    s = jnp.sin(ang)
    sin_ref[...] = jnp.where((lane & (HEAD_DIM - 1)) < HEAD_DIM // 2, -s, s)


def _rope_tables(positions):
    T = positions.size
    tq = min(512, T)
    pos = positions.reshape(T, 1).astype(F32)
    inv = ROPE_THETA ** (-(jnp.arange(HEAD_DIM // 2, dtype=F32) * 2.0 / HEAD_DIM))
    inv = jnp.tile(inv, LANES // (HEAD_DIM // 2))[None, :]
    return pl.pallas_call(
        _rope_table_kernel,
        grid=(T // tq,),
        in_specs=[pl.BlockSpec((tq, 1), lambda i: (i, 0)), pl.BlockSpec((1, LANES), lambda i: (0, 0))],
        out_specs=[pl.BlockSpec((tq, LANES), lambda i: (i, 0))] * 2,
        out_shape=[jax.ShapeDtypeStruct((T, LANES), F32)] * 2,
        compiler_params=_cparams(("parallel",)),
        name="rope_tables",
    )(pos, inv)


def _rope(x, cos, sin_signed):
    lane = lax.broadcasted_iota(I32, (1, LANES), 1)
    first = (lane & (HEAD_DIM - 1)) < HEAD_DIM // 2
    outs = []
    for c in range(x.shape[1] // LANES):
        xb = x[:, c * LANES:(c + 1) * LANES]
        partner = jnp.where(first, pltpu.roll(xb, LANES - HEAD_DIM // 2, 1), pltpu.roll(xb, HEAD_DIM // 2, 1))
        outs.append(xb * cos + partner * sin_signed)
    return outs[0] if len(outs) == 1 else jnp.concatenate(outs, axis=1)


DSA_QB = 256
DSA_KT = 512


def _dsa_prep_kernel(q_ref, iq_ref, k_ref, v_ref, idx_ref, cos_ref, sin_ref, nw_ref, nb_ref,
                     qo_ref, iqo_ref, kto_ref, vo_ref, ikto_ref, iwo_ref):
    cos = cos_ref[...]
    sin = sin_ref[...]
    qo_ref[...] = (_rope(q_ref[...], cos, sin) * (HEAD_DIM ** -0.5)).astype(BF16)
    iqo_ref[...] = _rope(iq_ref[...], cos, sin).astype(BF16)
    kto_ref[0] = _rope(k_ref[...], cos, sin).T.astype(BF16)
    v = v_ref[...]
    ones = jnp.ones((v.shape[0], HEAD_DIM), F32)
    vo_ref[...] = jnp.concatenate(
        [t for g in range(ATT_KV_HEADS) for t in (v[:, g * HEAD_DIM:(g + 1) * HEAD_DIM], ones)], axis=1).astype(BF16)
    idx = idx_ref[...]
    lane = lax.broadcasted_iota(I32, idx.shape, 1)
    is_k = lane < IDX_DIM
    mu = jnp.sum(jnp.where(is_k, idx, 0.0), axis=-1, keepdims=True) * (1.0 / IDX_DIM)
    dk = jnp.where(is_k, idx - mu, 0.0)
    var = jnp.sum(dk * dk, axis=-1, keepdims=True) * (1.0 / IDX_DIM)
    ikn = dk * lax.rsqrt(var + NORM_EPS) * nw_ref[...] + nb_ref[...]
    ikto_ref[0] = _rope(ikn, cos, sin).T[:IDX_DIM].astype(BF16)
    iw = pltpu.roll(idx, LANES - IDX_DIM, 1) * (IDX_HEADS ** -0.5 * IDX_DIM ** -0.5)
    iwo_ref[...] = jnp.where(lane < IDX_HEADS, iw, 0.0)


def _dsa_prep(proj, cos, sin, k_norm_w, k_norm_b):
    T = proj.shape[0]
    tq = DSA_KT
    D = BRANCH_D
    VA = 2 * ATT_KV_D
    spec = lambda w, off: pl.BlockSpec((tq, w), lambda i: (i, off // w))
    out = lambda w: pl.BlockSpec((tq, w), lambda i: (i, 0))
    tile_t = lambda n: pl.BlockSpec((1, n, tq), lambda i: (i, 0, 0))
    return pl.pallas_call(
        _dsa_prep_kernel,
        grid=(T // tq,),
        in_specs=[spec(D, OFF_QC), spec(D, OFF_IQ), spec(ATT_KV_D, OFF_KC), spec(ATT_KV_D, OFF_VC),
                  spec(LANES, OFF_IDX), out(LANES), out(LANES),
                  pl.BlockSpec((1, LANES), lambda i: (0, 0)), pl.BlockSpec((1, LANES), lambda i: (0, 0))],
        out_specs=[out(D), out(D), tile_t(ATT_KV_D), out(VA), tile_t(IDX_DIM), out(LANES)],
        out_shape=[jax.ShapeDtypeStruct((T, D), BF16), jax.ShapeDtypeStruct((T, D), BF16),
                   jax.ShapeDtypeStruct((T // tq, ATT_KV_D, tq), BF16), jax.ShapeDtypeStruct((T, VA), BF16),
                   jax.ShapeDtypeStruct((T // tq, IDX_DIM, tq), BF16), jax.ShapeDtypeStruct((T, LANES), F32)],
        compiler_params=_cparams(("parallel",)),
        name="dsa_prep",
    )(proj, proj, proj, proj, proj, cos, sin, _pad_lanes(k_norm_w), _pad_lanes(k_norm_b))


def _dsa_main_kernel(top_k, seq_len, q_ref, iq_ref, iw_ref, gate_ref, ikt_ref, kt_ref, v_ref, o_ref,
                     keys_ref, iwb_ref, bias_ref, m_ref, acc_ref, tie_ref):
    QB, KT = DSA_QB, DSA_KT
    NB = KT // LANES
    RH = QB // 2
    CH = KT // 2
    qi = pl.program_id(1)
    nk = (qi * QB + QB + KT - 1) // KT
    int_min = jnp.int32(INT_MIN)

    iq = iq_ref[...]
    iw = iw_ref[...]
    for h in range(IDX_HEADS):
        iwb_ref[h] = jnp.broadcast_to(iw[:, h:h + 1], (QB, LANES))
    qpos = qi * QB + lax.broadcasted_iota(I32, (RH, CH), 0)
    kcol = lax.broadcasted_iota(I32, (RH, CH), 1)

    def score_tile(kt, _):
        ikt = ikt_ref[kt]
        for rh in range(2):
            for ch in range(2):
                acc = jnp.zeros((RH, CH), F32)
                for h in range(IDX_HEADS):
                    lg = jnp.dot(iq[rh * RH:(rh + 1) * RH, h * IDX_DIM:(h + 1) * IDX_DIM],
                                 ikt[:, ch * CH:(ch + 1) * CH], preferred_element_type=F32)
                    w = iwb_ref[h, rh * RH:(rh + 1) * RH, :]
                    acc = acc + jnp.maximum(lg, 0.0) * jnp.concatenate([w] * (CH // LANES), axis=1)
                bits = pltpu.bitcast(acc, I32)
                skey = jnp.where(bits < 0, bits ^ jnp.int32(0x7FFFFFFF), bits)
                causal = kcol + (kt * KT + ch * CH) <= qpos + rh * RH
                keys_ref[kt, rh * RH:(rh + 1) * RH, ch * CH:(ch + 1) * CH] = jnp.where(causal, skey, int_min)
        return 0

    lax.fori_loop(0, nk, score_tile, 0)

    ones_b = jnp.ones((LANES, LANES), BF16)
    col0 = lax.broadcasted_iota(I32, (QB, LANES), 1)

    def count(pred):
        def body(kt, acc):
            kk = keys_ref[kt]
            for c in range(NB):
                hit = pred(kk[:, c * LANES:(c + 1) * LANES], col0 + (kt * KT + c * LANES))
                acc = acc + jnp.where(hit, 1.0, 0.0)
            return acc
        acc = lax.fori_loop(0, nk, body, jnp.zeros((QB, LANES), F32))
        return jnp.dot(acc.astype(BF16), ones_b, preferred_element_type=F32)

    def bit_body(i, res):
        cand_u = res | jnp.left_shift(jnp.int32(1), 31 - i)
        cand = cand_u ^ int_min
        cnt = count(lambda k, c: k >= cand)
        return jnp.where(cnt >= float(top_k), cand_u, res)

    res = lax.fori_loop(0, 32, bit_body, jnp.zeros((QB, LANES), I32))
    thr = jnp.maximum(res ^ int_min, jnp.int32(INT_MIN + 1))

    n_ge = count(lambda k, c: k >= thr)
    tie_ref[...] = jnp.full((QB, LANES), seq_len, I32)

    @pl.when(jnp.max(n_ge) > float(top_k))
    def _():
        need = float(top_k) - count(lambda k, c: k > thr)
        n_bits = max(1, (seq_len - 1).bit_length())

        def tie_body(i, x):
            cand = x | jnp.left_shift(jnp.int32(1), n_bits - 1 - i)
            below = count(lambda k, c: (k == thr) & (c < cand))
            return jnp.where(below < need, cand, x)

        tie_ref[...] = lax.fori_loop(0, n_bits, tie_body, jnp.zeros((QB, LANES), I32))

    tie = tie_ref[...]

    q = q_ref[...]
    rep = N_HEADS // ATT_KV_HEADS
    for h in range(N_HEADS):
        m_ref[h] = jnp.full((QB, 1), -1e29, F32)
        acc_ref[h] = jnp.zeros((QB, LANES), F32)

    def att_tile(kt, _):
        off = pl.multiple_of(kt * KT, KT)
        kk = keys_ref[kt]
        for c in range(NB):
            kb = kk[:, c * LANES:(c + 1) * LANES]
            sel = (kb > thr) | ((kb == thr) & (col0 + (kt * KT + c * LANES) <= tie))
            bias_ref[:, c * LANES:(c + 1) * LANES] = jnp.where(sel, 0.0, -1e30)
        for h in range(N_HEADS):
            g = h // rep
            lg = jnp.dot(q[:, h * HEAD_DIM:(h + 1) * HEAD_DIM], kt_ref[kt, g * HEAD_DIM:(g + 1) * HEAD_DIM, :],
                         preferred_element_type=F32) + bias_ref[...]
            m_old = m_ref[h]
            m_new = jnp.maximum(m_old, jnp.max(lg, axis=1, keepdims=True))
            p = jnp.exp(lg - m_new).astype(BF16)
            v_t = v_ref[pl.ds(off, KT), g * LANES:(g + 1) * LANES]
            acc_ref[h] = acc_ref[h] * jnp.exp(m_old - m_new) + jnp.dot(p, v_t, preferred_element_type=F32)
            m_ref[h] = m_new
        return 0

    lax.fori_loop(0, nk, att_tile, 0)

    gate = gate_ref[...]
    low = col0 < HEAD_DIM
    for j in range(N_HEADS // 2):
        a_e = acc_ref[2 * j]
        a_o = acc_ref[2 * j + 1]
        o2 = jnp.where(low, a_e / pltpu.roll(a_e, HEAD_DIM, 1), pltpu.roll(a_o, HEAD_DIM, 1) / a_o)
        gj = gate[:, j * LANES:(j + 1) * LANES]
        o_ref[:, j * LANES:(j + 1) * LANES] = (o2 * (gj * _sigmoid(gj))).astype(BF16)


def _dsa_main(proj, qr, iqr, ktr, va, iktr, iws, B, S, top_k):
    QB, KT = DSA_QB, DSA_KT
    nq = S // QB
    nkt = S // KT
    D = BRANCH_D
    row = lambda b, i: b * nq + i
    return pl.pallas_call(
        functools.partial(_dsa_main_kernel, top_k, S),
        grid=(B, nq),
        in_specs=[pl.BlockSpec((QB, D), lambda b, i: (row(b, i), 0)),
                  pl.BlockSpec((QB, D), lambda b, i: (row(b, i), 0)),
                  pl.BlockSpec((QB, LANES), lambda b, i: (row(b, i), 0)),
                  pl.BlockSpec((QB, D), lambda b, i: (row(b, i), OFF_GC // D)),
                  pl.BlockSpec((nkt, IDX_DIM, KT), lambda b, i: (b, 0, 0)),
                  pl.BlockSpec((nkt, ATT_KV_D, KT), lambda b, i: (b, 0, 0)),
                  pl.BlockSpec((S, 2 * ATT_KV_D), lambda b, i: (b, 0))],
        out_specs=pl.BlockSpec((QB, D), lambda b, i: (row(b, i), 0)),
        out_shape=jax.ShapeDtypeStruct((B * S, D), BF16),
        scratch_shapes=[pltpu.VMEM((nkt, QB, KT), I32),
                        pltpu.VMEM((IDX_HEADS, QB, LANES), F32),
                        pltpu.VMEM((QB, KT), F32),
                        pltpu.VMEM((N_HEADS, QB, 1), F32),
                        pltpu.VMEM((N_HEADS, QB, LANES), F32),
                        pltpu.VMEM((QB, LANES), I32)],
        compiler_params=_cparams(("parallel", "arbitrary")),
        name="dsa_main",
    )(qr, iqr, iws, proj, iktr, ktr, va)


def _fold_rows(x, op, group=SUBLANES):
    parts = x.reshape(x.shape[0] // group, group, x.shape[1])
    while parts.shape[0] > 1:
        half = parts.shape[0] // 2
        parts = op(parts[:half], parts[half:])
    return parts[0]


def _dsa_prep_t_kernel(q_ref, iq_ref, k_ref, v_ref, idx_ref, cos_ref, sin_ref, nw_ref, nb_ref,
                       qt_ref, iqt_ref, ko_ref, vat_ref, iko_ref, iwt_ref):
    QB = DSA_QB
    cos = cos_ref[...]
    sin = sin_ref[...]
    nblk = q_ref.shape[0] // QB
    qt = (_rope(q_ref[...], cos, sin) * (HEAD_DIM ** -0.5 * math.log2(math.e))).T
    iqt = _rope(iq_ref[...], cos, sin).T
    for j in range(nblk):
        qt_ref[j] = qt[:, j * QB:(j + 1) * QB].astype(BF16)
        iqt_ref[j] = iqt[:, j * QB:(j + 1) * QB].astype(BF16)
    ko_ref[...] = _rope(k_ref[...], cos, sin).astype(BF16)
    vt = v_ref[...].T
    ones = jnp.ones((HEAD_DIM, vt.shape[1]), F32)
    vat_ref[0] = jnp.concatenate(
        [t for g in range(ATT_KV_HEADS) for t in (vt[g * HEAD_DIM:(g + 1) * HEAD_DIM], ones)], axis=0).astype(BF16)
    idx = idx_ref[...]
    lane = lax.broadcasted_iota(I32, idx.shape, 1)
    is_k = lane < IDX_DIM
    mu = jnp.sum(jnp.where(is_k, idx, 0.0), axis=-1, keepdims=True) * (1.0 / IDX_DIM)
    dk = jnp.where(is_k, idx - mu, 0.0)
    var = jnp.sum(dk * dk, axis=-1, keepdims=True) * (1.0 / IDX_DIM)
    ikn = dk * lax.rsqrt(var + NORM_EPS) * nw_ref[...] + nb_ref[...]
    iko_ref[...] = _rope(ikn, cos, sin).astype(BF16)
    iwt = (pltpu.roll(idx, LANES - IDX_DIM, 1) * (IDX_HEADS ** -0.5 * IDX_DIM ** -0.5)).T
    for j in range(nblk):
        iwt_ref[j] = iwt[:IDX_HEADS, j * QB:(j + 1) * QB]


def _dsa_prep_t(proj, cos, sin, k_norm_w, k_norm_b):
    T = proj.shape[0]
    tq = DSA_KT
    QB = DSA_QB
    D = BRANCH_D
    VA = 2 * ATT_KV_D
    nb = tq // QB
    spec = lambda w, off: pl.BlockSpec((tq, w), lambda i: (i, off // w))
    out = lambda w: pl.BlockSpec((tq, w), lambda i: (i, 0))
    blk = lambda n: pl.BlockSpec((nb, n, QB), lambda i: (i, 0, 0))
    return pl.pallas_call(
        _dsa_prep_t_kernel,
        grid=(T // tq,),
        in_specs=[spec(D, OFF_QC), spec(D, OFF_IQ), spec(ATT_KV_D, OFF_KC), spec(ATT_KV_D, OFF_VC),
                  spec(LANES, OFF_IDX), out(LANES), out(LANES),
                  pl.BlockSpec((1, LANES), lambda i: (0, 0)), pl.BlockSpec((1, LANES), lambda i: (0, 0))],
        out_specs=[blk(D), blk(D), out(ATT_KV_D), pl.BlockSpec((1, VA, tq), lambda i: (i, 0, 0)), out(LANES),
                   blk(IDX_HEADS)],
        out_shape=[jax.ShapeDtypeStruct((T // QB, D, QB), BF16), jax.ShapeDtypeStruct((T // QB, D, QB), BF16),
                   jax.ShapeDtypeStruct((T, ATT_KV_D), BF16), jax.ShapeDtypeStruct((T // tq, VA, tq), BF16),
                   jax.ShapeDtypeStruct((T, LANES), BF16), jax.ShapeDtypeStruct((T // QB, IDX_HEADS, QB), F32)],
        compiler_params=_cparams(("parallel",)),
        name="dsa_prep",
    )(proj, proj, proj, proj, proj, cos, sin, _pad_lanes(k_norm_w), _pad_lanes(k_norm_b))


def _dsa_main_t_kernel(top_k, seq_len, qt_ref, iqt_ref, iwt_ref, gate_ref, ik_ref, k_ref, vat_ref, o_ref,
                       keys_ref, sacc_ref, bias_ref, m_ref, acc_ref, tie_ref, ot_ref, khi_ref, klo_ref):
    QB, KT = DSA_QB, DSA_KT
    qi = pl.program_id(1)
    nk = (qi * QB + QB + KT - 1) // KT
    int_min = jnp.int32(INT_MIN)
    rep = N_HEADS // ATT_KV_HEADS
    kpos0 = lax.broadcasted_iota(I32, (KT, QB), 0)
    qpos = qi * QB + lax.broadcasted_iota(I32, (KT, QB), 1)

    def score_tile(kt, _):
        off = pl.multiple_of(kt * KT, KT)
        ik_t = ik_ref[pl.ds(off, KT), :][:, :IDX_DIM]
        for h0 in range(0, IDX_HEADS, 4):
            lg = [jnp.dot(ik_t, iqt_ref[h * IDX_DIM:(h + 1) * IDX_DIM, :], preferred_element_type=F32)
                  for h in range(h0, h0 + 4)]
            term = sum(jnp.maximum(l, 0.0) * iwt_ref[h:h + 1, :] for l, h in zip(lg, range(h0, h0 + 4)))
            if h0 == 0:
                sacc_ref[...] = term
            else:
                sacc_ref[...] += term
        bits = pltpu.bitcast(sacc_ref[...], I32)
        skey = jnp.where(bits < 0, bits ^ jnp.int32(0x7FFFFFFF), bits)
        key = jnp.where(kpos0 + off <= qpos, skey, int_min)
        keys_ref[kt] = key
        khi_ref[kt] = (key >> 16).astype(I16)
        return 0

    lax.fori_loop(0, nk, score_tile, 0)

    def count(pred):
        def body(kt, acc):
            hit = pred(keys_ref[kt], kpos0 + kt * KT)
            return acc + _fold_rows(jnp.where(hit, 1.0, 0.0), jnp.add)
        acc = lax.fori_loop(0, nk, body, jnp.zeros((SUBLANES, QB), F32))
        return jnp.sum(acc, axis=0, keepdims=True)

    def count16(ref, cand):
        def body(kt, acc):
            return acc + _fold_rows(jnp.where(ref[kt] >= cand, jnp.int16(1), jnp.int16(0)), jnp.add, 2 * SUBLANES)
        acc = lax.fori_loop(0, nk, body, jnp.zeros((2 * SUBLANES, QB), I16))
        return jnp.sum(acc.astype(I32).astype(F32), axis=0, keepdims=True)

    def search16(ref, need):
        def bit_body(i, res):
            cand_u = res | jnp.left_shift(jnp.int32(1), 15 - i)
            cnt = count16(ref, (cand_u - 32768).astype(I16))
            return jnp.where(cnt >= need, cand_u, res)
        return lax.fori_loop(0, 16, bit_body, jnp.zeros((1, QB), I32))

    hi_s = search16(khi_ref, float(top_k)) - 32768
    hi_16 = hi_s.astype(I16)
    n_above = jnp.where(hi_s >= 32767, 0.0, count16(khi_ref, jnp.minimum(hi_s + 1, 32767).astype(I16)))

    def low_tile(kt, _):
        lo = ((keys_ref[kt] << 16) ^ int_min) >> 16
        klo_ref[kt] = jnp.where(khi_ref[kt] == hi_16, lo.astype(I16), jnp.int16(-32768))
        return 0

    lax.fori_loop(0, nk, low_tile, 0)
    lo_u = search16(klo_ref, float(top_k) - n_above)
    thr = jnp.maximum((hi_s << 16) | lo_u, jnp.int32(INT_MIN + 1))

    n_ge = count(lambda k, c: k >= thr)
    tie_ref[...] = jnp.full((1, QB), seq_len, I32)

    @pl.when(jnp.max(n_ge) > float(top_k))
    def _():
        need = float(top_k) - count(lambda k, c: k > thr)
        n_bits = max(1, (seq_len - 1).bit_length())

        def tie_body(i, x):
            cand = x | jnp.left_shift(jnp.int32(1), n_bits - 1 - i)
            below = count(lambda k, c: (k == thr) & (c < cand))
            return jnp.where(below < need, cand, x)

        tie_ref[...] = lax.fori_loop(0, n_bits, tie_body, jnp.zeros((1, QB), I32))

    tie = tie_ref[...]

    m_ref[...] = jnp.full(m_ref.shape, -1e29, F32)
    acc_ref[...] = jnp.zeros(acc_ref.shape, F32)

    def att_tile(kt, _):
        off = pl.multiple_of(kt * KT, KT)
        kk = keys_ref[kt]
        sel = (kk > thr) | ((kk == thr) & (kpos0 + off <= tie))
        bias_ref[...] = jnp.where(sel, 0.0, -1e30)
        k_t = k_ref[pl.ds(off, KT), :]
        for g in range(ATT_KV_HEADS):
            k_g = k_t[:, g * HEAD_DIM:(g + 1) * HEAD_DIM]
            va_g = vat_ref[kt, g * LANES:(g + 1) * LANES, :]
            hs = [g * rep + r for r in range(rep)]
            lg = [jnp.dot(k_g, qt_ref[h * HEAD_DIM:(h + 1) * HEAD_DIM, :], preferred_element_type=F32)
                  + bias_ref[...] for h in hs]
            m_old = [m_ref[h] for h in hs]
            m_new = [jnp.maximum(mo, jnp.max(_fold_rows(l, jnp.maximum), axis=0, keepdims=True))
                     for mo, l in zip(m_old, lg)]
            p = [jnp.exp2(l - mn).astype(BF16) for l, mn in zip(lg, m_new)]
            pv = [jnp.dot(va_g, pp, preferred_element_type=F32) for pp in p]
            for j, h in enumerate(hs):
                acc_ref[h] = acc_ref[h] * jnp.exp2(m_old[j] - m_new[j]) + pv[j]
                m_ref[h] = m_new[j]
        return 0

    lax.fori_loop(0, nk, att_tile, 0)

    for h in range(N_HEADS):
        a = acc_ref[h]
        ot_ref[h * HEAD_DIM:(h + 1) * HEAD_DIM, :] = a[:HEAD_DIM] / a[HEAD_DIM:]
    gate = gate_ref[...]
    o_ref[...] = (ot_ref[...].T * (gate * _sigmoid(gate))).astype(BF16)


def _dsa_main_t(proj, qt, iqt, kr, vat, ikr, iwt, B, S, top_k):
    QB, KT = DSA_QB, DSA_KT
    nq = S // QB
    nkt = S // KT
    D = BRANCH_D
    row = lambda b, i: b * nq + i
    return pl.pallas_call(
        functools.partial(_dsa_main_t_kernel, top_k, S),
        grid=(B, nq),
        in_specs=[pl.BlockSpec((None, D, QB), lambda b, i: (row(b, i), 0, 0)),
                  pl.BlockSpec((None, D, QB), lambda b, i: (row(b, i), 0, 0)),
                  pl.BlockSpec((None, IDX_HEADS, QB), lambda b, i: (row(b, i), 0, 0)),
                  pl.BlockSpec((QB, D), lambda b, i: (row(b, i), OFF_GC // D)),
                  pl.BlockSpec((S, LANES), lambda b, i: (b, 0)),
                  pl.BlockSpec((S, ATT_KV_D), lambda b, i: (b, 0)),
                  pl.BlockSpec((nkt, 2 * ATT_KV_D, KT), lambda b, i: (b, 0, 0))],
        out_specs=pl.BlockSpec((QB, D), lambda b, i: (row(b, i), 0)),
        out_shape=jax.ShapeDtypeStruct((B * S, D), BF16),
        scratch_shapes=[pltpu.VMEM((nkt, KT, QB), I32),
                        pltpu.VMEM((KT, QB), F32),
                        pltpu.VMEM((KT, QB), F32),
                        pltpu.VMEM((N_HEADS, 1, QB), F32),
                        pltpu.VMEM((N_HEADS, LANES, QB), F32),
                        pltpu.VMEM((1, QB), I32),
                        pltpu.VMEM((D, QB), F32),
                        pltpu.VMEM((nkt, KT, QB), I16),
                        pltpu.VMEM((nkt, KT, QB), I16)],
        compiler_params=_cparams(("parallel", "arbitrary")),
        name="dsa_main",
    )(qt, iqt, iwt, proj, ikr, kr, vat)


def _dsa_branch(proj, cos, sin, B, S, k_norm_w, k_norm_b, top_k):
    qt, iqt, kr, vat, ikr, iwt = _dsa_prep_t(proj, cos, sin, k_norm_w, k_norm_b)
    return _dsa_main_t(proj, qt, iqt, kr, vat, ikr, iwt, B, S, top_k)


def _merge_kernel(ya_ref, yb_ref, yc_ref, ga_ref, gb_ref, gc_ref, ba_ref, bb_ref, bc_ref,
                  wa_ref, wb_ref, wc_ref, o_ref):
    acc = _sigmoid(ga_ref[...] + ba_ref[...]) * jnp.dot(ya_ref[...], wa_ref[...], preferred_element_type=F32)
    acc += _sigmoid(gb_ref[...] + bb_ref[...]) * jnp.dot(yb_ref[...], wb_ref[...], preferred_element_type=F32)
    acc += _sigmoid(gc_ref[...] + bc_ref[...]) * jnp.dot(yc_ref[...], wc_ref[...], preferred_element_type=F32)
    o_ref[...] = acc.astype(BF16)


def _merge(ya, yb, yc, proj, b_gate, wa, wb, wc, layer):
    T = ya.shape[0]
    D, Db = D_MODEL, BRANCH_D
    tm = min(256, T)
    tn = D
    nj = D // tn
    y = pl.BlockSpec((tm, Db), lambda i, j: (i, 0))
    gate = lambda n: pl.BlockSpec((tm, tn), lambda i, j: (i, (OFF_GATES + n * D) // tn + j))
    bias = lambda n: pl.BlockSpec((1, tn), lambda i, j: (0, n * nj + j))
    w = pl.BlockSpec((None, Db, tn), lambda i, j: (layer, 0, j))
    bg = b_gate[None, :]
    return pl.pallas_call(
        _merge_kernel,
        grid=(T // tm, nj),
        in_specs=[y, y, y, gate(0), gate(1), gate(2), bias(0), bias(1), bias(2), w, w, w],
        out_specs=pl.BlockSpec((tm, tn), lambda i, j: (i, j)),
        out_shape=jax.ShapeDtypeStruct((T, D), BF16),
        compiler_params=_cparams(("parallel", "arbitrary")),
        name="gated_merge",
    )(ya, yb, yc, proj, proj, proj, bg, bg, bg, wa, wb, wc)


def _out_proj_kernel(m_ref, w_ref, g_ref, x_ref, o_ref):
    y = jnp.dot(m_ref[...], w_ref[...], preferred_element_type=F32)
    ms = jnp.mean(y * y, axis=-1, keepdims=True)
    o_ref[...] = x_ref[...] + y * lax.rsqrt(ms + NORM_EPS) * g_ref[...]


def _out_proj(merged, w_out, g, x, layer):
    T, D = x.shape
    tm = min(256, T)
    return pl.pallas_call(
        _out_proj_kernel,
        grid=(T // tm,),
        in_specs=[pl.BlockSpec((tm, D), lambda i: (i, 0)), pl.BlockSpec((None, D, D), lambda i: (layer, 0, 0)),
                  pl.BlockSpec((1, D), lambda i: (0, 0)), pl.BlockSpec((tm, D), lambda i: (i, 0))],
        out_specs=pl.BlockSpec((tm, D), lambda i: (i, 0)),
        out_shape=jax.ShapeDtypeStruct((T, D), F32),
        compiler_params=_cparams(("parallel",)),
        name="out_proj_norm_residual",
    )(merged, w_out, g, x)


def _pack_w_in(w_in):
    a0, b0 = 0, A_COLS
    c0 = A_COLS + B_COLS
    g0 = c0 + C_COLS
    D = BRANCH_D
    w_in = w_in.astype(BF16)
    sl = lambda s, n: w_in[:, :, s:s + n]
    zeros = lambda n: jnp.zeros(w_in.shape[:2] + (n,), w_in.dtype)
    pieces = [
        sl(a0 + D, SSM_CONV_DIM),
        sl(a0, D),
        sl(b0, D),
        sl(b0 + D + RWKV_LORA, D),
        sl(b0 + 2 * D + RWKV_LORA, D),
        sl(b0 + RWKV_SHIFT_DIM, D),
        sl(c0, D),
        sl(c0 + D + 2 * ATT_KV_D, D),
        sl(c0 + 2 * D + 2 * ATT_KV_D, D),
        sl(g0, GATE_COLS),
        sl(c0 + D, ATT_KV_D),
        sl(c0 + D + ATT_KV_D, ATT_KV_D),
        sl(a0 + D + SSM_CONV_DIM, N_HEADS), zeros(LANES - N_HEADS),
        sl(b0 + D, RWKV_LORA), sl(b0 + 3 * D + RWKV_LORA, RWKV_LORA),
        sl(c0 + 3 * D + 2 * ATT_KV_D, IDX_DIM + IDX_HEADS), zeros(LANES - IDX_DIM - IDX_HEADS),
    ]
    pieces.append(zeros(NP_COLS - sum(p.shape[-1] for p in pieces)))
    return jnp.concatenate(pieces, axis=-1)


def kernel(x, positions, pre_norm, post_norm, w_in, b_gate, ssm_conv_w, ssm_conv_b, ssm_dt_bias, ssm_a_log,
           ssm_d, ssm_norm, rwkv_mu, rwkv_w0, rwkv_w2, rwkv_a0, rwkv_a2, rwkv_k_k, rwkv_k_a, rwkv_r_k,
           rwkv_ln_w, rwkv_ln_b, idx_k_norm_w, idx_k_norm_b, w_branch_a, w_branch_b, w_branch_c, w_out):
    B, S, D = x.shape
    depth = w_in.shape[0]
    top_k = min(TOPK_MAX, S // 4)
    xt = x.reshape(B * S, D)
    w_in_p = _pack_w_in(w_in)
    wa, wb, wc, wo = (w.astype(BF16) for w in (w_branch_a, w_branch_b, w_branch_c, w_out))
    cos, sin = _rope_tables(positions)
    for i in range(depth):
        proj = _norm_matmul(xt, pre_norm[i][None, :], w_in_p, i)
        ya = _ssd_branch(proj, B, S, ssm_conv_w[i], ssm_conv_b[i], ssm_dt_bias[i], ssm_a_log[i], ssm_d[i], ssm_norm[i])
        yb = _rwkv_branch(proj, B, S, rwkv_mu[i], rwkv_w0[i], rwkv_w2[i], rwkv_a0[i], rwkv_a2[i], rwkv_k_k[i],
                          rwkv_k_a[i], rwkv_r_k[i], rwkv_ln_w[i], rwkv_ln_b[i])
        yc = _dsa_branch(proj, cos, sin, B, S, idx_k_norm_w[i], idx_k_norm_b[i], top_k)
        merged = _merge(ya, yb, yc, proj, b_gate[i], wa, wb, wc, i)
        xt = _out_proj(merged, wo, post_norm[i][None, :], xt, i)
    return xt.reshape(B, S, D)
```

```python
import functools
import math

import numpy as np
import jax
import jax.numpy as jnp
from jax import lax
from jax.experimental import pallas as pl
from jax.experimental.pallas import tpu as pltpu

F32 = jnp.float32
BF16 = jnp.bfloat16
I32 = jnp.int32
I16 = jnp.int16
HI = lax.Precision.HIGHEST

D_MODEL = 2048
BRANCH_D = D_MODEL // 2
NORM_EPS = 1e-6
HEAD_DIM = 64
N_HEADS = BRANCH_D // HEAD_DIM

SSM_GROUPS = 4
SSM_STATE = 128
SSM_CONV = 4
SSD_CHUNK = 128
SSM_CONV_DIM = BRANCH_D + 2 * SSM_GROUPS * SSM_STATE
A_COLS = BRANCH_D + SSM_CONV_DIM + N_HEADS

RWKV_LORA = 64
RWKV_SHIFT_DIM = 3 * BRANCH_D + 2 * RWKV_LORA
B_COLS = RWKV_SHIFT_DIM + BRANCH_D
RWKV_GN_EPS = 64e-5
RWKV_CHUNK = 64

ATT_KV_HEADS = 4
ATT_KV_D = ATT_KV_HEADS * HEAD_DIM
IDX_HEADS = 16
IDX_DIM = 64
TOPK_MAX = 256
Q_BLOCK = 128
ROPE_THETA = 10000.0
C_COLS = BRANCH_D + 2 * ATT_KV_D + BRANCH_D + IDX_HEADS * IDX_DIM + IDX_DIM + IDX_HEADS
GATE_COLS = 3 * D_MODEL

LANES = 128
SUBLANES = 8
VMEM_LIMIT = 56 * 1024 * 1024

OFF_XBC = 0
OFF_Z = 2048
OFF_RB = 3072
OFF_KB = 4096
OFF_VB = 5120
OFF_GB = 6144
OFF_QC = 7168
OFF_GC = 8192
OFF_IQ = 9216
OFF_GATES = 10240
OFF_KC = 16384
OFF_VC = 16640
OFF_DT = 16896
OFF_LORA = 17024
OFF_IDX = 17152
NP_COLS = 17408

INT_MIN = -2 ** 31


def _cparams(sem):
    return pltpu.CompilerParams(dimension_semantics=sem, vmem_limit_bytes=VMEM_LIMIT)


def _mm(a, b):
    return jnp.dot(a.astype(BF16), b.astype(BF16), preferred_element_type=F32)


def _mm_nt(a, b):
    return lax.dot_general(a.astype(BF16), b.astype(BF16), (((1,), (1,)), ((), ())),
                           preferred_element_type=F32)


def _mm_tn(a, b):
    return lax.dot_general(a.astype(BF16), b.astype(BF16), (((0,), (0,)), ((), ())),
                           preferred_element_type=F32)


def _mm_hi(a, b, exact="b"):
    x, m = (a, b) if exact == "b" else (b, a)
    m = m.astype(BF16)
    acc = None
    for _ in range(3):
        xb = x.astype(BF16)
        part = jnp.dot(xb, m, preferred_element_type=F32) if exact == "b" else jnp.dot(m, xb, preferred_element_type=F32)
        acc = part if acc is None else acc + part
        x = x - xb.astype(F32)
    return acc


def _head_expand():
    r = lax.broadcasted_iota(I32, (LANES, BRANCH_D), 0)
    c = lax.broadcasted_iota(I32, (LANES, BRANCH_D), 1)
    return jnp.where((c >> 6) == r, 1.0, 0.0).astype(F32)


def _head_reduce():
    r = lax.broadcasted_iota(I32, (BRANCH_D, LANES), 0)
    c = lax.broadcasted_iota(I32, (BRANCH_D, LANES), 1)
    return jnp.where((r >> 6) == c, 1.0, 0.0).astype(F32)


def _tril(n, strict=False):
    r = lax.broadcasted_iota(I32, (n, n), 0)
    c = lax.broadcasted_iota(I32, (n, n), 1)
    return (c < r) if strict else (c <= r)


def _softplus(x):
    return jnp.maximum(x, 0.0) + jnp.log1p(jnp.exp(-jnp.abs(x)))


def _sigmoid(x):
    return 1.0 / (1.0 + jnp.exp(-x))


def _shift_rows(x, carry, s):
    rolled = pltpu.roll(x, s, 0)
    rows = lax.broadcasted_iota(I32, (SUBLANES, x.shape[1]), 0)
    top = jnp.where(rows < s, pltpu.roll(carry, s, 0), rolled[0:SUBLANES])
    return jnp.concatenate([top, rolled[SUBLANES:]], axis=0)


def _norm_matmul_kernel(x_ref, g_ref, w_ref, o_ref, xn_ref):
    @pl.when(pl.program_id(1) == 0)
    def _():
        x = x_ref[...]
        ms = jnp.mean(x * x, axis=-1, keepdims=True)
        xn_ref[...] = (x * lax.rsqrt(ms + NORM_EPS) * g_ref[...]).astype(BF16)

    o_ref[...] = jnp.dot(xn_ref[...], w_ref[...], preferred_element_type=F32)


def _norm_matmul(x, g, w, layer):
    T, D = x.shape
    Np = w.shape[2]
    tm = min(1024, T)
    tn = 1024
    return pl.pallas_call(
        _norm_matmul_kernel,
        grid=(T // tm, Np // tn),
        in_specs=[pl.BlockSpec((tm, D), lambda i, j: (i, 0)),
                  pl.BlockSpec((1, D), lambda i, j: (0, 0)),
                  pl.BlockSpec((None, D, tn), lambda i, j: (layer, 0, j))],
        out_specs=pl.BlockSpec((tm, tn), lambda i, j: (i, j)),
        out_shape=jax.ShapeDtypeStruct((T, Np), F32),
        scratch_shapes=[pltpu.VMEM((tm, D), BF16)],
        compiler_params=_cparams(("parallel", "arbitrary")),
        name="norm_in_proj",
    )(x, g, w)


def _ssd_kernel(xbc_ref, z_ref, dt_ref, cw_ref, cb_ref, dtb_ref, alog_ref, dskip_ref, nw_ref,
                o_ref, carry_ref, h_ref, y_ref):
    Q = SSD_CHUNK
    GN = SSM_GROUPS * SSM_STATE
    GP = BRANCH_D // SSM_GROUPS

    @pl.when(pl.program_id(1) == 0)
    def _():
        carry_ref[...] = jnp.zeros_like(carry_ref)
        h_ref[...] = jnp.zeros_like(h_ref)

    x = xbc_ref[...]
    carry = carry_ref[...]
    cw = cw_ref[...]
    acc = x * cw[SSM_CONV - 1:SSM_CONV] + cb_ref[...]
    for s in range(1, SSM_CONV):
        acc = acc + _shift_rows(x, carry, s) * cw[SSM_CONV - 1 - s:SSM_CONV - s]
    carry_ref[...] = x[Q - SUBLANES:Q]
    xc = acc * _sigmoid(acc)
    xs = xc[:, :BRANCH_D]
    bm = xc[:, BRANCH_D:BRANCH_D + GN]
    cm = xc[:, BRANCH_D + GN:]

    dt = _softplus(dt_ref[...] + dtb_ref[...])
    adt = dt * (-jnp.exp(alog_ref[...]))
    a_cs = _mm_hi(_tril(Q).astype(F32), adt, exact="a")
    a_cs_t = a_cs.T
    a_last = a_cs[Q - 1:Q]
    ex = _head_expand()
    dt_e = _mm_hi(dt, ex)
    ecs_e = _mm_hi(jnp.exp(a_cs), ex)
    ds_e = _mm_hi(jnp.exp(a_last - a_cs), ex)
    cd_e = _mm_hi(jnp.exp(a_last), ex)
    xd = xs * dt_e
    xds = xd * ds_e
    causal = _tril(Q)

    G = range(SSM_GROUPS)
    H = range(N_HEADS)
    rep = N_HEADS // SSM_GROUPS
    bm_b, cm_b, xd_b, xds_b = (t.astype(BF16) for t in (bm, cm, xd, xds))
    bm_g = [bm_b[:, g * SSM_STATE:(g + 1) * SSM_STATE] for g in G]
    cm_g = [cm_b[:, g * SSM_STATE:(g + 1) * SSM_STATE] for g in G]
    hg = [h_ref[g] for g in G]
    cb = [_mm_nt(cm_g[g], bm_g[g]) for g in G]
    lmat = [jnp.exp(jnp.where(causal, a_cs[:, h:h + 1] - a_cs_t[h:h + 1, :], -1e30)) for h in H]
    y_diag = [_mm(cb[h // rep] * lmat[h], xd_b[:, h * HEAD_DIM:(h + 1) * HEAD_DIM]) for h in H]
    y_off = [_mm(cm_g[g], hg[g]) * ecs_e[:, g * GP:(g + 1) * GP] for g in G]
    h_new = [hg[g] * cd_e[:, g * GP:(g + 1) * GP] + _mm_tn(bm_g[g], xds_b[:, g * GP:(g + 1) * GP]) for g in G]
    for g in G:
        h_ref[g] = h_new[g]
        y_ref[:, g * GP:(g + 1) * GP] = y_off[g]
    for h in H:
        y_ref[:, h * HEAD_DIM:(h + 1) * HEAD_DIM] += y_diag[h]

    y = y_ref[...] + dskip_ref[...] * xs
    z = z_ref[...]
    yz = y * (z * _sigmoid(z))
    nw = nw_ref[...]
    for g in range(SSM_GROUPS):
        seg = yz[:, g * GP:(g + 1) * GP]
        ms = jnp.mean(seg * seg, axis=-1, keepdims=True)
        o_ref[:, g * GP:(g + 1) * GP] = (seg * lax.rsqrt(ms + NORM_EPS) * nw[:, g * GP:(g + 1) * GP]).astype(BF16)


def _pad_lanes(v, n=LANES):
    return jnp.zeros((1, n), F32).at[0, :v.shape[0]].set(v.astype(F32))


def _ssd_branch(proj, B, S, conv_w, conv_b, dt_bias, a_log, d_skip, norm_w):
    Q = SSD_CHUNK
    nc = S // Q
    row = lambda b, c: b * nc + c
    full = lambda shape: pl.BlockSpec(shape, lambda b, c: (0, 0))
    return pl.pallas_call(
        _ssd_kernel,
        grid=(B, nc),
        in_specs=[pl.BlockSpec((Q, SSM_CONV_DIM), lambda b, c: (row(b, c), OFF_XBC // SSM_CONV_DIM)),
                  pl.BlockSpec((Q, BRANCH_D), lambda b, c: (row(b, c), OFF_Z // BRANCH_D)),
                  pl.BlockSpec((Q, LANES), lambda b, c: (row(b, c), OFF_DT // LANES)),
                  full((SSM_CONV, SSM_CONV_DIM)), full((1, SSM_CONV_DIM)), full((1, LANES)), full((1, LANES)),
                  full((1, BRANCH_D)), full((1, BRANCH_D))],
        out_specs=pl.BlockSpec((Q, BRANCH_D), lambda b, c: (row(b, c), 0)),
        out_shape=jax.ShapeDtypeStruct((B * S, BRANCH_D), BF16),
        scratch_shapes=[pltpu.VMEM((SUBLANES, SSM_CONV_DIM), F32),
                        pltpu.VMEM((SSM_GROUPS, SSM_STATE, BRANCH_D // SSM_GROUPS), F32),
                        pltpu.VMEM((Q, BRANCH_D), F32)],
        compiler_params=_cparams(("parallel", "arbitrary")),
        name="ssd_branch",
    )(proj, proj, proj, conv_w, conv_b[None, :], _pad_lanes(dt_bias), _pad_lanes(a_log),
      jnp.repeat(d_skip, HEAD_DIM)[None, :], norm_w[None, :])


def _mix(x, carry_ref, mu):
    rolled = pltpu.roll(x, 1, 0)
    rows = lax.broadcasted_iota(I32, x.shape, 0)
    prev = jnp.where(rows == 0, carry_ref[SUBLANES - 1:SUBLANES], rolled)
    carry_ref[...] = x[x.shape[0] - SUBLANES:]
    return x + (prev - x) * mu


def _rwkv_kernel(r_ref, k_ref, v_ref, g_ref, lo_ref, mu_r_ref, mu_k_ref, mu_v_ref, mu_lo_ref,
                 w0_ref, w2_ref, a0_ref, a2_ref, kk_ref, ka_ref, rk_ref, lnw_ref, lnb_ref,
                 o_ref, cr_ref, ck_ref, cv_ref, clo_ref, s_ref, y_ref):
    C = RWKV_CHUNK

    @pl.when(pl.program_id(1) == 0)
    def _():
        for ref in (cr_ref, ck_ref, cv_ref, clo_ref, s_ref):
            ref[...] = jnp.zeros_like(ref)

    r = _mix(r_ref[...], cr_ref, mu_r_ref[...])
    k = _mix(k_ref[...], ck_ref, mu_k_ref[...])
    v = _mix(v_ref[...], cv_ref, mu_v_ref[...])
    lo = _mix(lo_ref[...], clo_ref, mu_lo_ref[...])
    wd = lo[:, :RWKV_LORA]
    ad = lo[:, RWKV_LORA:]

    wlog = -_softplus(-(w0_ref[...] + _mm(jnp.tanh(wd), w2_ref[...]))) - 0.5
    lw = -jnp.exp(wlog)
    a = _sigmoid(a0_ref[...] + _mm(ad, a2_ref[...]))

    ex = _head_expand()
    red = _head_reduce()
    kk = k * kk_ref[...]
    nrm = jnp.maximum(jnp.sqrt(_mm_hi(kk * kk, red)), 1e-12)
    kk = kk * _mm_hi(1.0 / nrm, ex)
    kmod = k * (1.0 + (a - 1.0) * ka_ref[...])
    b = kk * a

    cum = _mm_hi(_tril(C).astype(F32), lw, exact="a")
    gam = jnp.exp(cum)
    igam = jnp.exp(-cum)
    at = jnp.exp(cum - lw) * kk
    bt = b * igam
    kt = kmod * igam
    rt = r * gam
    g_last = gam[C - 1:C]

    H = range(N_HEADS)
    P = range(N_HEADS // 2)
    lane = lax.broadcasted_iota(I32, (C, LANES), 1)
    rowi = lax.broadcasted_iota(I32, (C, LANES), 0)
    low = lane < HEAD_DIM
    lane_in = lane & (HEAD_DIM - 1)
    strict2 = lane_in < rowi
    incl2 = lane_in <= rowi
    eye2 = jnp.where(lane_in == rowi, 1.0, 0.0)
    low2 = jnp.concatenate([low, low], axis=0)
    zeros_b = jnp.zeros((C, LANES), BF16)
    cols = lambda t: [t[:, p * LANES:(p + 1) * LANES] for p in P]
    at_c, bt_c, kt_c, rt_c, v_c = (cols(t.astype(BF16)) for t in (at, bt, kt, rt, v))
    rt_f = cols(rt)
    nat = lambda h, a, b: jnp.where(low, a, b) if h % 2 == 0 else jnp.where(low, b, a)
    nat2 = lambda h, a, b: jnp.where(low2, a, b) if h % 2 == 0 else jnp.where(low2, b, a)
    rows = lambda top, bot: jnp.concatenate([top, bot], axis=0)
    nat_rows = lambda h, t: rows(t, zeros_b) if h % 2 == 0 else rows(zeros_b, t)
    oth_rows = lambda h, t: rows(zeros_b, t) if h % 2 == 0 else rows(t, zeros_b)

    p1 = [_mm_nt(rows(nat(h, at_c[h // 2], zeros_b), nat(h, rt_c[h // 2], zeros_b)),
                 rows(bt_c[h // 2], kt_c[h // 2]) if h % 2 == 0 else rows(kt_c[h // 2], bt_c[h // 2])) for h in H]
    top = [jnp.where(strict2, p[:C], 0.0) for p in p1]
    bot = [jnp.where(incl2, p[C:], 0.0) for p in p1]
    tp = [nat(h, -top[h], eye2) for h in H]
    for _ in range(int(math.log2(C))):
        tp_b = [t.astype(BF16) for t in tp]
        out = [_mm(tp_b[h], nat_rows(h, tp_b[h])) for h in H]
        tp = [out[h] + nat(h, 0.0, tp[h]) for h in H]
    tp_b = [t.astype(BF16) for t in tp]
    top_b = [t.astype(BF16) for t in top]
    bot_b = [t.astype(BF16) for t in bot]
    xz = [_mm(tp_b[h], oth_rows(h, nat(h, at_c[h // 2], top_b[h]))) for h in H]
    xz_b = [t.astype(BF16) for t in xz]
    mxz = [_mm(bot_b[h], nat_rows(h, xz_b[h])) for h in H]
    ra = [nat(h, rt_f[h // 2], bot[h]) - mxz[h] for h in H]
    stack = [rows(ra[h], xz[h]).astype(BF16) for h in H]
    s_b = [s_ref[p].astype(BF16) for p in P]
    yw = [_mm_nt(nat2(h, stack[h], jnp.zeros_like(stack[h])), s_b[h // 2]) + _mm(stack[h], oth_rows(h, v_c[h // 2]))
          for h in H]
    yw_p = [jnp.where(low2, yw[2 * p], yw[2 * p + 1]) for p in P]
    blockdiag = (lax.broadcasted_iota(I32, (LANES, LANES), 0) < HEAD_DIM) == (
        lax.broadcasted_iota(I32, (LANES, LANES), 1) < HEAD_DIM)
    inc = [_mm_tn(rows(v_c[p], (-yw_p[p][C:]).astype(BF16)), rows(kt_c[p], bt_c[p])) for p in P]
    for p in P:
        y_ref[:, p * LANES:(p + 1) * LANES] = yw_p[p][:C]
        s_ref[p] = (s_ref[p] + jnp.where(blockdiag, inc[p], 0.0)) * g_last[:, p * LANES:(p + 1) * LANES]

    y = y_ref[...]
    inv_n = 1.0 / HEAD_DIM
    mean = _mm_hi(_mm_hi(y, red) * inv_n, ex)
    yc = y - mean
    var = _mm_hi(yc * yc, red) * inv_n
    yn = yc * _mm_hi(lax.rsqrt(var + RWKV_GN_EPS), ex) * lnw_ref[...] + lnb_ref[...]
    bonus = _mm_hi(_mm_hi(r * kmod * rk_ref[...], red), ex) * v
    gate = g_ref[...]
    o_ref[...] = ((yn + bonus) * (gate * _sigmoid(gate))).astype(BF16)


def _rwkv_branch(proj, B, S, mu, w0, w2, a0, a2, k_k, k_a, r_k, ln_w, ln_b):
    C = RWKV_CHUNK
    nc = S // C
    D = BRANCH_D
    row = lambda b, c: b * nc + c
    full = lambda shape: pl.BlockSpec(shape, lambda b, c: (0, 0))
    slab = lambda off: pl.BlockSpec((C, D), lambda b, c: (row(b, c), off // D))
    mu_r, mu_wd, mu_k, mu_v, mu_ad = (mu[:D], mu[D:D + RWKV_LORA], mu[D + RWKV_LORA:2 * D + RWKV_LORA],
                                      mu[2 * D + RWKV_LORA:3 * D + RWKV_LORA], mu[3 * D + RWKV_LORA:])
    r1 = lambda t: t.reshape(1, -1).astype(F32)
    return pl.pallas_call(
        _rwkv_kernel,
        grid=(B, nc),
        in_specs=[slab(OFF_RB), slab(OFF_KB), slab(OFF_VB), slab(OFF_GB),
                  pl.BlockSpec((C, LANES), lambda b, c: (row(b, c), OFF_LORA // LANES)),
                  full((1, D)), full((1, D)), full((1, D)), full((1, LANES)),
                  full((1, D)), full((RWKV_LORA, D)), full((1, D)), full((RWKV_LORA, D)),
                  full((1, D)), full((1, D)), full((1, D)), full((1, D)), full((1, D))],
        out_specs=pl.BlockSpec((C, D), lambda b, c: (row(b, c), 0)),
        out_shape=jax.ShapeDtypeStruct((B * S, D), BF16),
        scratch_shapes=[pltpu.VMEM((SUBLANES, D), F32), pltpu.VMEM((SUBLANES, D), F32),
                        pltpu.VMEM((SUBLANES, D), F32), pltpu.VMEM((SUBLANES, LANES), F32),
                        pltpu.VMEM((N_HEADS // 2, LANES, LANES), F32),
                        pltpu.VMEM((C, D), F32)],
        compiler_params=_cparams(("parallel", "arbitrary")),
        name="rwkv_branch",
    )(proj, proj, proj, proj, proj, r1(mu_r), r1(mu_k), r1(mu_v), r1(jnp.concatenate([mu_wd, mu_ad])),
      r1(w0), w2, r1(a0), a2, r1(k_k), r1(k_a), r1(r_k), r1(ln_w), r1(ln_b))


def _rope_table_kernel(pos_ref, inv_ref, cos_ref, sin_ref):
    ang = pos_ref[...] * inv_ref[...]
    lane = lax.broadcasted_iota(I32, ang.shape, 1)
    cos_ref[...] = jnp.cos(ang)
    s = jnp.sin(ang)
    sin_ref[...] = jnp.where((lane & (HEAD_DIM - 1)) < HEAD_DIM // 2, -s, s)


def _rope_tables(positions):
    T = positions.size
    tq = min(512, T)
    pos = positions.reshape(T, 1).astype(F32)
    inv = ROPE_THETA ** (-(jnp.arange(HEAD_DIM // 2, dtype=F32) * 2.0 / HEAD_DIM))
    inv = jnp.tile(inv, LANES // (HEAD_DIM // 2))[None, :]
    return pl.pallas_call(
        _rope_table_kernel,
        grid=(T // tq,),
        in_specs=[pl.BlockSpec((tq, 1), lambda i: (i, 0)), pl.BlockSpec((1, LANES), lambda i: (0, 0))],
        out_specs=[pl.BlockSpec((tq, LANES), lambda i: (i, 0))] * 2,
        out_shape=[jax.ShapeDtypeStruct((T, LANES), F32)] * 2,
        compiler_params=_cparams(("parallel",)),
        name="rope_tables",
    )(pos, inv)


def _rope(x, cos, sin_signed):
    lane = lax.broadcasted_iota(I32, (1, LANES), 1)
    first = (lane & (HEAD_DIM - 1)) < HEAD_DIM // 2
    outs = []
    for c in range(x.shape[1] // LANES):
        xb = x[:, c * LANES:(c + 1) * LANES]
        partner = jnp.where(first, pltpu.roll(xb, LANES - HEAD_DIM // 2, 1), pltpu.roll(xb, HEAD_DIM // 2, 1))
        outs.append(xb * cos + partner * sin_signed)
    return outs[0] if len(outs) == 1 else jnp.concatenate(outs, axis=1)


DSA_QB = 256
DSA_KT = 512


def _dsa_prep_kernel(q_ref, iq_ref, k_ref, v_ref, idx_ref, cos_ref, sin_ref, nw_ref, nb_ref,
                     qo_ref, iqo_ref, kto_ref, vo_ref, ikto_ref, iwo_ref):
    cos = cos_ref[...]
    sin = sin_ref[...]
    qo_ref[...] = (_rope(q_ref[...], cos, sin) * (HEAD_DIM ** -0.5)).astype(BF16)
    iqo_ref[...] = _rope(iq_ref[...], cos, sin).astype(BF16)
    kto_ref[0] = _rope(k_ref[...], cos, sin).T.astype(BF16)
    v = v_ref[...]
    ones = jnp.ones((v.shape[0], HEAD_DIM), F32)
    vo_ref[...] = jnp.concatenate(
        [t for g in range(ATT_KV_HEADS) for t in (v[:, g * HEAD_DIM:(g + 1) * HEAD_DIM], ones)], axis=1).astype(BF16)
    idx = idx_ref[...]
    lane = lax.broadcasted_iota(I32, idx.shape, 1)
    is_k = lane < IDX_DIM
    mu = jnp.sum(jnp.where(is_k, idx, 0.0), axis=-1, keepdims=True) * (1.0 / IDX_DIM)
    dk = jnp.where(is_k, idx - mu, 0.0)
    var = jnp.sum(dk * dk, axis=-1, keepdims=True) * (1.0 / IDX_DIM)
    ikn = dk * lax.rsqrt(var + NORM_EPS) * nw_ref[...] + nb_ref[...]
    ikto_ref[0] = _rope(ikn, cos, sin).T[:IDX_DIM].astype(BF16)
    iw = pltpu.roll(idx, LANES - IDX_DIM, 1) * (IDX_HEADS ** -0.5 * IDX_DIM ** -0.5)
    iwo_ref[...] = jnp.where(lane < IDX_HEADS, iw, 0.0)


def _dsa_prep(proj, cos, sin, k_norm_w, k_norm_b):
    T = proj.shape[0]
    tq = DSA_KT
    D = BRANCH_D
    VA = 2 * ATT_KV_D
    spec = lambda w, off: pl.BlockSpec((tq, w), lambda i: (i, off // w))
    out = lambda w: pl.BlockSpec((tq, w), lambda i: (i, 0))
    tile_t = lambda n: pl.BlockSpec((1, n, tq), lambda i: (i, 0, 0))
    return pl.pallas_call(
        _dsa_prep_kernel,
        grid=(T // tq,),
        in_specs=[spec(D, OFF_QC), spec(D, OFF_IQ), spec(ATT_KV_D, OFF_KC), spec(ATT_KV_D, OFF_VC),
                  spec(LANES, OFF_IDX), out(LANES), out(LANES),
                  pl.BlockSpec((1, LANES), lambda i: (0, 0)), pl.BlockSpec((1, LANES), lambda i: (0, 0))],
        out_specs=[out(D), out(D), tile_t(ATT_KV_D), out(VA), tile_t(IDX_DIM), out(LANES)],
        out_shape=[jax.ShapeDtypeStruct((T, D), BF16), jax.ShapeDtypeStruct((T, D), BF16),
                   jax.ShapeDtypeStruct((T // tq, ATT_KV_D, tq), BF16), jax.ShapeDtypeStruct((T, VA), BF16),
                   jax.ShapeDtypeStruct((T // tq, IDX_DIM, tq), BF16), jax.ShapeDtypeStruct((T, LANES), F32)],
        compiler_params=_cparams(("parallel",)),
        name="dsa_prep",
    )(proj, proj, proj, proj, proj, cos, sin, _pad_lanes(k_norm_w), _pad_lanes(k_norm_b))


def _dsa_main_kernel(top_k, seq_len, q_ref, iq_ref, iw_ref, gate_ref, ikt_ref, kt_ref, v_ref, o_ref,
                     keys_ref, iwb_ref, bias_ref, m_ref, acc_ref, tie_ref):
    QB, KT = DSA_QB, DSA_KT
    NB = KT // LANES
    RH = QB // 2
    CH = KT // 2
    qi = pl.program_id(1)
    nk = (qi * QB + QB + KT - 1) // KT
    int_min = jnp.int32(INT_MIN)

    iq = iq_ref[...]
    iw = iw_ref[...]
    for h in range(IDX_HEADS):
        iwb_ref[h] = jnp.broadcast_to(iw[:, h:h + 1], (QB, LANES))
    qpos = qi * QB + lax.broadcasted_iota(I32, (RH, CH), 0)
    kcol = lax.broadcasted_iota(I32, (RH, CH), 1)

    def score_tile(kt, _):
        ikt = ikt_ref[kt]
        for rh in range(2):
            for ch in range(2):
                acc = jnp.zeros((RH, CH), F32)
                for h in range(IDX_HEADS):
                    lg = jnp.dot(iq[rh * RH:(rh + 1) * RH, h * IDX_DIM:(h + 1) * IDX_DIM],
                                 ikt[:, ch * CH:(ch + 1) * CH], preferred_element_type=F32)
                    w = iwb_ref[h, rh * RH:(rh + 1) * RH, :]
                    acc = acc + jnp.maximum(lg, 0.0) * jnp.concatenate([w] * (CH // LANES), axis=1)
                bits = pltpu.bitcast(acc, I32)
                skey = jnp.where(bits < 0, bits ^ jnp.int32(0x7FFFFFFF), bits)
                causal = kcol + (kt * KT + ch * CH) <= qpos + rh * RH
                keys_ref[kt, rh * RH:(rh + 1) * RH, ch * CH:(ch + 1) * CH] = jnp.where(causal, skey, int_min)
        return 0

    lax.fori_loop(0, nk, score_tile, 0)

    ones_b = jnp.ones((LANES, LANES), BF16)
    col0 = lax.broadcasted_iota(I32, (QB, LANES), 1)

    def count(pred):
        def body(kt, acc):
            kk = keys_ref[kt]
            for c in range(NB):
                hit = pred(kk[:, c * LANES:(c + 1) * LANES], col0 + (kt * KT + c * LANES))
                acc = acc + jnp.where(hit, 1.0, 0.0)
            return acc
        acc = lax.fori_loop(0, nk, body, jnp.zeros((QB, LANES), F32))
        return jnp.dot(acc.astype(BF16), ones_b, preferred_element_type=F32)

    def bit_body(i, res):
        cand_u = res | jnp.left_shift(jnp.int32(1), 31 - i)
        cand = cand_u ^ int_min
        cnt = count(lambda k, c: k >= cand)
        return jnp.where(cnt >= float(top_k), cand_u, res)

    res = lax.fori_loop(0, 32, bit_body, jnp.zeros((QB, LANES), I32))
    thr = jnp.maximum(res ^ int_min, jnp.int32(INT_MIN + 1))

    n_ge = count(lambda k, c: k >= thr)
    tie_ref[...] = jnp.full((QB, LANES), seq_len, I32)

    @pl.when(jnp.max(n_ge) > float(top_k))
    def _():
        need = float(top_k) - count(lambda k, c: k > thr)
        n_bits = max(1, (seq_len - 1).bit_length())

        def tie_body(i, x):
            cand = x | jnp.left_shift(jnp.int32(1), n_bits - 1 - i)
            below = count(lambda k, c: (k == thr) & (c < cand))
            return jnp.where(below < need, cand, x)

        tie_ref[...] = lax.fori_loop(0, n_bits, tie_body, jnp.zeros((QB, LANES), I32))

    tie = tie_ref[...]

    q = q_ref[...]
    rep = N_HEADS // ATT_KV_HEADS
    for h in range(N_HEADS):
        m_ref[h] = jnp.full((QB, 1), -1e29, F32)
        acc_ref[h] = jnp.zeros((QB, LANES), F32)

    def att_tile(kt, _):
        off = pl.multiple_of(kt * KT, KT)
        kk = keys_ref[kt]
        for c in range(NB):
            kb = kk[:, c * LANES:(c + 1) * LANES]
            sel = (kb > thr) | ((kb == thr) & (col0 + (kt * KT + c * LANES) <= tie))
            bias_ref[:, c * LANES:(c + 1) * LANES] = jnp.where(sel, 0.0, -1e30)
        for h in range(N_HEADS):
            g = h // rep
            lg = jnp.dot(q[:, h * HEAD_DIM:(h + 1) * HEAD_DIM], kt_ref[kt, g * HEAD_DIM:(g + 1) * HEAD_DIM, :],
                         preferred_element_type=F32) + bias_ref[...]
            m_old = m_ref[h]
            m_new = jnp.maximum(m_old, jnp.max(lg, axis=1, keepdims=True))
            p = jnp.exp(lg - m_new).astype(BF16)
            v_t = v_ref[pl.ds(off, KT), g * LANES:(g + 1) * LANES]
            acc_ref[h] = acc_ref[h] * jnp.exp(m_old - m_new) + jnp.dot(p, v_t, preferred_element_type=F32)
            m_ref[h] = m_new
        return 0

    lax.fori_loop(0, nk, att_tile, 0)

    gate = gate_ref[...]
    low = col0 < HEAD_DIM
    for j in range(N_HEADS // 2):
        a_e = acc_ref[2 * j]
        a_o = acc_ref[2 * j + 1]
        o2 = jnp.where(low, a_e / pltpu.roll(a_e, HEAD_DIM, 1), pltpu.roll(a_o, HEAD_DIM, 1) / a_o)
        gj = gate[:, j * LANES:(j + 1) * LANES]
        o_ref[:, j * LANES:(j + 1) * LANES] = (o2 * (gj * _sigmoid(gj))).astype(BF16)


def _dsa_main(proj, qr, iqr, ktr, va, iktr, iws, B, S, top_k):
    QB, KT = DSA_QB, DSA_KT
    nq = S // QB
    nkt = S // KT
    D = BRANCH_D
    row = lambda b, i: b * nq + i
    return pl.pallas_call(
        functools.partial(_dsa_main_kernel, top_k, S),
        grid=(B, nq),
        in_specs=[pl.BlockSpec((QB, D), lambda b, i: (row(b, i), 0)),
                  pl.BlockSpec((QB, D), lambda b, i: (row(b, i), 0)),
                  pl.BlockSpec((QB, LANES), lambda b, i: (row(b, i), 0)),
                  pl.BlockSpec((QB, D), lambda b, i: (row(b, i), OFF_GC // D)),
                  pl.BlockSpec((nkt, IDX_DIM, KT), lambda b, i: (b, 0, 0)),
                  pl.BlockSpec((nkt, ATT_KV_D, KT), lambda b, i: (b, 0, 0)),
                  pl.BlockSpec((S, 2 * ATT_KV_D), lambda b, i: (b, 0))],
        out_specs=pl.BlockSpec((QB, D), lambda b, i: (row(b, i), 0)),
        out_shape=jax.ShapeDtypeStruct((B * S, D), BF16),
        scratch_shapes=[pltpu.VMEM((nkt, QB, KT), I32),
                        pltpu.VMEM((IDX_HEADS, QB, LANES), F32),
                        pltpu.VMEM((QB, KT), F32),
                        pltpu.VMEM((N_HEADS, QB, 1), F32),
                        pltpu.VMEM((N_HEADS, QB, LANES), F32),
                        pltpu.VMEM((QB, LANES), I32)],
        compiler_params=_cparams(("parallel", "arbitrary")),
        name="dsa_main",
    )(qr, iqr, iws, proj, iktr, ktr, va)


def _fold_rows(x, op, group=SUBLANES):
    parts = x.reshape(x.shape[0] // group, group, x.shape[1])
    while parts.shape[0] > 1:
        half = parts.shape[0] // 2
        parts = op(parts[:half], parts[half:])
    return parts[0]


def _dsa_prep_t_kernel(q_ref, iq_ref, k_ref, v_ref, idx_ref, cos_ref, sin_ref, nw_ref, nb_ref,
                       qt_ref, iqt_ref, ko_ref, vat_ref, iko_ref, iwt_ref):
    QB = DSA_QB
    cos = cos_ref[...]
    sin = sin_ref[...]
    nblk = q_ref.shape[0] // QB
    qt = (_rope(q_ref[...], cos, sin) * (HEAD_DIM ** -0.5 * math.log2(math.e))).T
    iqt = _rope(iq_ref[...], cos, sin).T
    for j in range(nblk):
        qt_ref[j] = qt[:, j * QB:(j + 1) * QB].astype(BF16)
        iqt_ref[j] = iqt[:, j * QB:(j + 1) * QB].astype(BF16)
    ko_ref[...] = _rope(k_ref[...], cos, sin).astype(BF16)
    vt = v_ref[...].T
    ones = jnp.ones((HEAD_DIM, vt.shape[1]), F32)
    vat_ref[0] = jnp.concatenate(
        [t for g in range(ATT_KV_HEADS) for t in (vt[g * HEAD_DIM:(g + 1) * HEAD_DIM], ones)], axis=0).astype(BF16)
    idx = idx_ref[...]
    lane = lax.broadcasted_iota(I32, idx.shape, 1)
    is_k = lane < IDX_DIM
    mu = jnp.sum(jnp.where(is_k, idx, 0.0), axis=-1, keepdims=True) * (1.0 / IDX_DIM)
    dk = jnp.where(is_k, idx - mu, 0.0)
    var = jnp.sum(dk * dk, axis=-1, keepdims=True) * (1.0 / IDX_DIM)
    ikn = dk * lax.rsqrt(var + NORM_EPS) * nw_ref[...] + nb_ref[...]
    iko_ref[...] = _rope(ikn, cos, sin).astype(BF16)
    iwt = (pltpu.roll(idx, LANES - IDX_DIM, 1) * (IDX_HEADS ** -0.5 * IDX_DIM ** -0.5)).T
    for j in range(nblk):
        iwt_ref[j] = iwt[:IDX_HEADS, j * QB:(j + 1) * QB]


def _dsa_prep_t(proj, cos, sin, k_norm_w, k_norm_b):
    T = proj.shape[0]
    tq = DSA_KT
    QB = DSA_QB
    D = BRANCH_D
    VA = 2 * ATT_KV_D
    nb = tq // QB
    spec = lambda w, off: pl.BlockSpec((tq, w), lambda i: (i, off // w))
    out = lambda w: pl.BlockSpec((tq, w), lambda i: (i, 0))
    blk = lambda n: pl.BlockSpec((nb, n, QB), lambda i: (i, 0, 0))
    return pl.pallas_call(
        _dsa_prep_t_kernel,
        grid=(T // tq,),
        in_specs=[spec(D, OFF_QC), spec(D, OFF_IQ), spec(ATT_KV_D, OFF_KC), spec(ATT_KV_D, OFF_VC),
                  spec(LANES, OFF_IDX), out(LANES), out(LANES),
                  pl.BlockSpec((1, LANES), lambda i: (0, 0)), pl.BlockSpec((1, LANES), lambda i: (0, 0))],
        out_specs=[blk(D), blk(D), out(ATT_KV_D), pl.BlockSpec((1, VA, tq), lambda i: (i, 0, 0)), out(LANES),
                   blk(IDX_HEADS)],
        out_shape=[jax.ShapeDtypeStruct((T // QB, D, QB), BF16), jax.ShapeDtypeStruct((T // QB, D, QB), BF16),
                   jax.ShapeDtypeStruct((T, ATT_KV_D), BF16), jax.ShapeDtypeStruct((T // tq, VA, tq), BF16),
                   jax.ShapeDtypeStruct((T, LANES), BF16), jax.ShapeDtypeStruct((T // QB, IDX_HEADS, QB), F32)],
        compiler_params=_cparams(("parallel",)),
        name="dsa_prep",
    )(proj, proj, proj, proj, proj, cos, sin, _pad_lanes(k_norm_w), _pad_lanes(k_norm_b))


def _dsa_main_t_kernel(top_k, seq_len, qt_ref, iqt_ref, iwt_ref, gate_ref, ik_ref, k_ref, vat_ref, o_ref,
                       keys_ref, sacc_ref, bias_ref, m_ref, acc_ref, tie_ref, ot_ref, khi_ref, klo_ref):
    QB, KT = DSA_QB, DSA_KT
    qi = pl.program_id(1)
    nk = (qi * QB + QB + KT - 1) // KT
    int_min = jnp.int32(INT_MIN)
    rep = N_HEADS // ATT_KV_HEADS
    kpos0 = lax.broadcasted_iota(I32, (KT, QB), 0)
    qpos = qi * QB + lax.broadcasted_iota(I32, (KT, QB), 1)

    def score_tile(kt, _):
        off = pl.multiple_of(kt * KT, KT)
        ik_t = ik_ref[pl.ds(off, KT), :][:, :IDX_DIM]
        for h0 in range(0, IDX_HEADS, 4):
            lg = [jnp.dot(ik_t, iqt_ref[h * IDX_DIM:(h + 1) * IDX_DIM, :], preferred_element_type=F32)
                  for h in range(h0, h0 + 4)]
            term = sum(jnp.maximum(l, 0.0) * iwt_ref[h:h + 1, :] for l, h in zip(lg, range(h0, h0 + 4)))
            if h0 == 0:
                sacc_ref[...] = term
            else:
                sacc_ref[...] += term
        bits = pltpu.bitcast(sacc_ref[...], I32)
        skey = jnp.where(bits < 0, bits ^ jnp.int32(0x7FFFFFFF), bits)
        key = jnp.where(kpos0 + off <= qpos, skey, int_min)
        keys_ref[kt] = key
        khi_ref[kt] = (key >> 16).astype(I16)
        return 0

    lax.fori_loop(0, nk, score_tile, 0)

    def count(pred):
        def body(kt, acc):
            hit = pred(keys_ref[kt], kpos0 + kt * KT)
            return acc + _fold_rows(jnp.where(hit, 1.0, 0.0), jnp.add)
        acc = lax.fori_loop(0, nk, body, jnp.zeros((SUBLANES, QB), F32))
        return jnp.sum(acc, axis=0, keepdims=True)

    def count16(ref, cand):
        def body(kt, acc):
            return acc + _fold_rows(jnp.where(ref[kt] >= cand, jnp.int16(1), jnp.int16(0)), jnp.add, 2 * SUBLANES)
        acc = lax.fori_loop(0, nk, body, jnp.zeros((2 * SUBLANES, QB), I16))
        return jnp.sum(acc.astype(I32).astype(F32), axis=0, keepdims=True)

    def search16(ref, need):
        def bit_body(i, res):
            cand_u = res | jnp.left_shift(jnp.int32(1), 15 - i)
            cnt = count16(ref, (cand_u - 32768).astype(I16))
            return jnp.where(cnt >= need, cand_u, res)
        return lax.fori_loop(0, 16, bit_body, jnp.zeros((1, QB), I32))

    hi_s = search16(khi_ref, float(top_k)) - 32768
    hi_16 = hi_s.astype(I16)
    n_above = jnp.where(hi_s >= 32767, 0.0, count16(khi_ref, jnp.minimum(hi_s + 1, 32767).astype(I16)))

    def low_tile(kt, _):
        lo = ((keys_ref[kt] << 16) ^ int_min) >> 16
        klo_ref[kt] = jnp.where(khi_ref[kt] == hi_16, lo.astype(I16), jnp.int16(-32768))
        return 0

    lax.fori_loop(0, nk, low_tile, 0)
    lo_u = search16(klo_ref, float(top_k) - n_above)
    thr = jnp.maximum((hi_s << 16) | lo_u, jnp.int32(INT_MIN + 1))

    n_ge = count(lambda k, c: k >= thr)
    tie_ref[...] = jnp.full((1, QB), seq_len, I32)

    @pl.when(jnp.max(n_ge) > float(top_k))
    def _():
        need = float(top_k) - count(lambda k, c: k > thr)
        n_bits = max(1, (seq_len - 1).bit_length())

        def tie_body(i, x):
            cand = x | jnp.left_shift(jnp.int32(1), n_bits - 1 - i)
            below = count(lambda k, c: (k == thr) & (c < cand))
            return jnp.where(below < need, cand, x)

        tie_ref[...] = lax.fori_loop(0, n_bits, tie_body, jnp.zeros((1, QB), I32))

    tie = tie_ref[...]

    m_ref[...] = jnp.full(m_ref.shape, -1e29, F32)
    acc_ref[...] = jnp.zeros(acc_ref.shape, F32)

    def att_tile(kt, _):
        off = pl.multiple_of(kt * KT, KT)
        kk = keys_ref[kt]
        sel = (kk > thr) | ((kk == thr) & (kpos0 + off <= tie))
        bias_ref[...] = jnp.where(sel, 0.0, -1e30)
        k_t = k_ref[pl.ds(off, KT), :]
        hs = range(N_HEADS)
        k_g = [k_t[:, g * HEAD_DIM:(g + 1) * HEAD_DIM] for g in range(ATT_KV_HEADS)]
        va_g = [vat_ref[kt, g * LANES:(g + 1) * LANES, :] for g in range(ATT_KV_HEADS)]
        m_old = [m_ref[h] for h in hs]
        a_old = [acc_ref[h] for h in hs]
        lg = [jnp.dot(k_g[h // rep], qt_ref[h * HEAD_DIM:(h + 1) * HEAD_DIM, :], preferred_element_type=F32)
              + bias_ref[...] for h in hs]
        m_new = [jnp.maximum(mo, jnp.max(_fold_rows(l, jnp.maximum), axis=0, keepdims=True))
                 for mo, l in zip(m_old, lg)]
        p = [jnp.exp2(l - mn).astype(BF16) for l, mn in zip(lg, m_new)]
        pv = [jnp.dot(va_g[h // rep], p[h], preferred_element_type=F32) for h in hs]
        for h in hs:
            acc_ref[h] = a_old[h] * jnp.exp2(m_old[h] - m_new[h]) + pv[h]
            m_ref[h] = m_new[h]
        return 0

    lax.fori_loop(0, nk, att_tile, 0)

    for h in range(N_HEADS):
        a = acc_ref[h]
        ot_ref[h * HEAD_DIM:(h + 1) * HEAD_DIM, :] = a[:HEAD_DIM] / a[HEAD_DIM:]
    gate = gate_ref[...]
    o_ref[...] = (ot_ref[...].T * (gate * _sigmoid(gate))).astype(BF16)


def _dsa_main_t(proj, qt, iqt, kr, vat, ikr, iwt, B, S, top_k):
    QB, KT = DSA_QB, DSA_KT
    nq = S // QB
    nkt = S // KT
    D = BRANCH_D
    row = lambda b, i: b * nq + i
    return pl.pallas_call(
        functools.partial(_dsa_main_t_kernel, top_k, S),
        grid=(B, nq),
        in_specs=[pl.BlockSpec((None, D, QB), lambda b, i: (row(b, i), 0, 0)),
                  pl.BlockSpec((None, D, QB), lambda b, i: (row(b, i), 0, 0)),
                  pl.BlockSpec((None, IDX_HEADS, QB), lambda b, i: (row(b, i), 0, 0)),
                  pl.BlockSpec((QB, D), lambda b, i: (row(b, i), OFF_GC // D)),
                  pl.BlockSpec((S, LANES), lambda b, i: (b, 0)),
                  pl.BlockSpec((S, ATT_KV_D), lambda b, i: (b, 0)),
                  pl.BlockSpec((nkt, 2 * ATT_KV_D, KT), lambda b, i: (b, 0, 0))],
        out_specs=pl.BlockSpec((QB, D), lambda b, i: (row(b, i), 0)),
        out_shape=jax.ShapeDtypeStruct((B * S, D), BF16),
        scratch_shapes=[pltpu.VMEM((nkt, KT, QB), I32),
                        pltpu.VMEM((KT, QB), F32),
                        pltpu.VMEM((KT, QB), F32),
                        pltpu.VMEM((N_HEADS, 1, QB), F32),
                        pltpu.VMEM((N_HEADS, LANES, QB), F32),
                        pltpu.VMEM((1, QB), I32),
                        pltpu.VMEM((D, QB), F32),
                        pltpu.VMEM((nkt, KT, QB), I16),
                        pltpu.VMEM((nkt, KT, QB), I16)],
        compiler_params=_cparams(("parallel", "arbitrary")),
        name="dsa_main",
    )(qt, iqt, iwt, proj, ikr, kr, vat)


def _dsa_branch(proj, cos, sin, B, S, k_norm_w, k_norm_b, top_k):
    qt, iqt, kr, vat, ikr, iwt = _dsa_prep_t(proj, cos, sin, k_norm_w, k_norm_b)
    return _dsa_main_t(proj, qt, iqt, kr, vat, ikr, iwt, B, S, top_k)


def _merge_kernel(ya_ref, yb_ref, yc_ref, ga_ref, gb_ref, gc_ref, ba_ref, bb_ref, bc_ref,
                  wa_ref, wb_ref, wc_ref, o_ref):
    acc = _sigmoid(ga_ref[...] + ba_ref[...]) * jnp.dot(ya_ref[...], wa_ref[...], preferred_element_type=F32)
    acc += _sigmoid(gb_ref[...] + bb_ref[...]) * jnp.dot(yb_ref[...], wb_ref[...], preferred_element_type=F32)
    acc += _sigmoid(gc_ref[...] + bc_ref[...]) * jnp.dot(yc_ref[...], wc_ref[...], preferred_element_type=F32)
    o_ref[...] = acc.astype(BF16)


def _merge(ya, yb, yc, proj, b_gate, wa, wb, wc, layer):
    T = ya.shape[0]
    D, Db = D_MODEL, BRANCH_D
    tm = min(256, T)
    tn = D
    nj = D // tn
    y = pl.BlockSpec((tm, Db), lambda i, j: (i, 0))
    gate = lambda n: pl.BlockSpec((tm, tn), lambda i, j: (i, (OFF_GATES + n * D) // tn + j))
    bias = lambda n: pl.BlockSpec((1, tn), lambda i, j: (0, n * nj + j))
    w = pl.BlockSpec((None, Db, tn), lambda i, j: (layer, 0, j))
    bg = b_gate[None, :]
    return pl.pallas_call(
        _merge_kernel,
        grid=(T // tm, nj),
        in_specs=[y, y, y, gate(0), gate(1), gate(2), bias(0), bias(1), bias(2), w, w, w],
        out_specs=pl.BlockSpec((tm, tn), lambda i, j: (i, j)),
        out_shape=jax.ShapeDtypeStruct((T, D), BF16),
        compiler_params=_cparams(("parallel", "arbitrary")),
        name="gated_merge",
    )(ya, yb, yc, proj, proj, proj, bg, bg, bg, wa, wb, wc)


def _out_proj_kernel(m_ref, w_ref, g_ref, x_ref, o_ref):
    y = jnp.dot(m_ref[...], w_ref[...], preferred_element_type=F32)
    ms = jnp.mean(y * y, axis=-1, keepdims=True)
    o_ref[...] = x_ref[...] + y * lax.rsqrt(ms + NORM_EPS) * g_ref[...]


def _out_proj(merged, w_out, g, x, layer):
    T, D = x.shape
    tm = min(256, T)
    return pl.pallas_call(
        _out_proj_kernel,
        grid=(T // tm,),
        in_specs=[pl.BlockSpec((tm, D), lambda i: (i, 0)), pl.BlockSpec((None, D, D), lambda i: (layer, 0, 0)),
                  pl.BlockSpec((1, D), lambda i: (0, 0)), pl.BlockSpec((tm, D), lambda i: (i, 0))],
        out_specs=pl.BlockSpec((tm, D), lambda i: (i, 0)),
        out_shape=jax.ShapeDtypeStruct((T, D), F32),
        compiler_params=_cparams(("parallel",)),
        name="out_proj_norm_residual",
    )(merged, w_out, g, x)


def _w_in_pieces():
    a0, b0 = 0, A_COLS
    c0 = A_COLS + B_COLS
    g0 = c0 + C_COLS
    D = BRANCH_D
    return [
        (a0 + D, SSM_CONV_DIM, OFF_XBC),
        (a0, D, OFF_Z),
        (b0, D, OFF_RB),
        (b0 + D + RWKV_LORA, D, OFF_KB),
        (b0 + 2 * D + RWKV_LORA, D, OFF_VB),
        (b0 + RWKV_SHIFT_DIM, D, OFF_GB),
        (c0, D, OFF_QC),
        (c0 + D + 2 * ATT_KV_D, D, OFF_GC),
        (c0 + 2 * D + 2 * ATT_KV_D, D, OFF_IQ),
        (g0, GATE_COLS, OFF_GATES),
        (c0 + D, ATT_KV_D, OFF_KC),
        (c0 + D + ATT_KV_D, ATT_KV_D, OFF_VC),
        (a0 + D + SSM_CONV_DIM, N_HEADS, OFF_DT),
        (b0 + D, RWKV_LORA, OFF_LORA),
        (b0 + 3 * D + RWKV_LORA, RWKV_LORA, OFF_LORA + RWKV_LORA),
        (c0 + 3 * D + 2 * ATT_KV_D, IDX_DIM + IDX_HEADS, OFF_IDX),
    ]


def _pack_w_in_kernel(w_ref, o_ref):
    o_ref[...] = jnp.zeros(o_ref.shape, BF16)
    for src, width, dst in _w_in_pieces():
        o_ref[:, dst:dst + width] = w_ref[:, src:src + width].astype(BF16)


def _pack_w_in(w_in):
    L, D, n_in = w_in.shape
    tr = 128
    return pl.pallas_call(
        _pack_w_in_kernel,
        grid=(L, D // tr),
        in_specs=[pl.BlockSpec((None, tr, n_in), lambda l, i: (l, i, 0))],
        out_specs=pl.BlockSpec((None, tr, NP_COLS), lambda l, i: (l, i, 0)),
        out_shape=jax.ShapeDtypeStruct((L, D, NP_COLS), BF16),
        compiler_params=_cparams(("parallel", "parallel")),
        name="pack_w_in",
    )(w_in)


def kernel(x, positions, pre_norm, post_norm, w_in, b_gate, ssm_conv_w, ssm_conv_b, ssm_dt_bias, ssm_a_log,
           ssm_d, ssm_norm, rwkv_mu, rwkv_w0, rwkv_w2, rwkv_a0, rwkv_a2, rwkv_k_k, rwkv_k_a, rwkv_r_k,
           rwkv_ln_w, rwkv_ln_b, idx_k_norm_w, idx_k_norm_b, w_branch_a, w_branch_b, w_branch_c, w_out):
    B, S, D = x.shape
    depth = w_in.shape[0]
    top_k = min(TOPK_MAX, S // 4)
    xt = x.reshape(B * S, D)
    w_in_p = _pack_w_in(w_in)
    wa, wb, wc, wo = (w.astype(BF16) for w in (w_branch_a, w_branch_b, w_branch_c, w_out))
    cos, sin = _rope_tables(positions)
    for i in range(depth):
        proj = _norm_matmul(xt, pre_norm[i][None, :], w_in_p, i)
        ya = _ssd_branch(proj, B, S, ssm_conv_w[i], ssm_conv_b[i], ssm_dt_bias[i], ssm_a_log[i], ssm_d[i], ssm_norm[i])
        yb = _rwkv_branch(proj, B, S, rwkv_mu[i], rwkv_w0[i], rwkv_w2[i], rwkv_a0[i], rwkv_a2[i], rwkv_k_k[i],
                          rwkv_k_a[i], rwkv_r_k[i], rwkv_ln_w[i], rwkv_ln_b[i])
        yc = _dsa_branch(proj, cos, sin, B, S, idx_k_norm_w[i], idx_k_norm_b[i], top_k)
        merged = _merge(ya, yb, yc, proj, b_gate[i], wa, wb, wc, i)
        xt = _out_proj(merged, wo, post_norm[i][None, :], xt, i)
    return xt.reshape(B, S, D)
```

```python
import functools
import math

import jax
import jax.numpy as jnp
from jax import lax
from jax.experimental import pallas as pl
from jax.experimental.pallas import tpu as pltpu

F32 = jnp.float32
BF16 = jnp.bfloat16
I32 = jnp.int32
I16 = jnp.int16

D_MODEL = 2048
BRANCH_D = D_MODEL // 2
NORM_EPS = 1e-6
HEAD_DIM = 64
N_HEADS = BRANCH_D // HEAD_DIM

SSM_GROUPS = 4
SSM_STATE = 128
SSM_CONV = 4
SSD_CHUNK = 128
SSM_CONV_DIM = BRANCH_D + 2 * SSM_GROUPS * SSM_STATE
A_COLS = BRANCH_D + SSM_CONV_DIM + N_HEADS

RWKV_LORA = 64
RWKV_SHIFT_DIM = 3 * BRANCH_D + 2 * RWKV_LORA
B_COLS = RWKV_SHIFT_DIM + BRANCH_D
RWKV_GN_EPS = 64e-5
RWKV_CHUNK = 64
RWKV_ROWS = 256

ATT_KV_HEADS = 4
ATT_KV_D = ATT_KV_HEADS * HEAD_DIM
IDX_HEADS = 16
IDX_DIM = 64
TOPK_MAX = 256
ROPE_THETA = 10000.0
C_COLS = BRANCH_D + 2 * ATT_KV_D + BRANCH_D + IDX_HEADS * IDX_DIM + IDX_DIM + IDX_HEADS
GATE_COLS = 3 * D_MODEL

DSA_QB = 256
DSA_KT = 512

LANES = 128
SUBLANES = 8
VMEM_LIMIT = 56 * 1024 * 1024

OFF_XBC = 0
OFF_Z = 2048
OFF_RB = 3072
OFF_KB = 4096
OFF_VB = 5120
OFF_GB = 6144
OFF_QC = 7168
OFF_GC = 8192
OFF_IQ = 9216
OFF_GATES = 10240
OFF_KC = 16384
OFF_VC = 16640
OFF_DT = 16896
OFF_LORA = 17024
OFF_IDX = 17152
NP_COLS = 17408

INT_MIN = -2 ** 31


def _cparams(sem):
    return pltpu.CompilerParams(dimension_semantics=sem, vmem_limit_bytes=VMEM_LIMIT)


def _mm(a, b):
    return jnp.dot(a.astype(BF16), b.astype(BF16), preferred_element_type=F32)


def _mm_nt(a, b):
    return lax.dot_general(a.astype(BF16), b.astype(BF16), (((1,), (1,)), ((), ())),
                           preferred_element_type=F32)


def _mm_tn(a, b):
    return lax.dot_general(a.astype(BF16), b.astype(BF16), (((0,), (0,)), ((), ())),
                           preferred_element_type=F32)


def _mm_hi(a, b, exact="b"):
    x, m = (a, b) if exact == "b" else (b, a)
    m = m.astype(BF16)
    acc = None
    for _ in range(3):
        xb = x.astype(BF16)
        part = jnp.dot(xb, m, preferred_element_type=F32) if exact == "b" else jnp.dot(m, xb, preferred_element_type=F32)
        acc = part if acc is None else acc + part
        x = x - xb.astype(F32)
    return acc


def _head_expand():
    r = lax.broadcasted_iota(I32, (LANES, BRANCH_D), 0)
    c = lax.broadcasted_iota(I32, (LANES, BRANCH_D), 1)
    return jnp.where((c >> 6) == r, 1.0, 0.0).astype(F32)


def _head_reduce():
    r = lax.broadcasted_iota(I32, (BRANCH_D, LANES), 0)
    c = lax.broadcasted_iota(I32, (BRANCH_D, LANES), 1)
    return jnp.where((r >> 6) == c, 1.0, 0.0).astype(F32)


def _tril(n):
    r = lax.broadcasted_iota(I32, (n, n), 0)
    c = lax.broadcasted_iota(I32, (n, n), 1)
    return c <= r


def _softplus(x):
    return jnp.maximum(x, 0.0) + jnp.log1p(jnp.exp(-jnp.abs(x)))


def _sigmoid(x):
    return 1.0 / (1.0 + jnp.exp(-x))


def _shift_rows(x, carry, s):
    rolled = pltpu.roll(x, s, 0)
    rows = lax.broadcasted_iota(I32, (SUBLANES, x.shape[1]), 0)
    top = jnp.where(rows < s, pltpu.roll(carry, s, 0), rolled[0:SUBLANES])
    return jnp.concatenate([top, rolled[SUBLANES:]], axis=0)


def _fold_rows(x, op, group=SUBLANES):
    parts = x.reshape(x.shape[0] // group, group, x.shape[1])
    while parts.shape[0] > 1:
        half = parts.shape[0] // 2
        parts = op(parts[:half], parts[half:])
    return parts[0]


def _pad_lanes(v, n=LANES):
    return jnp.zeros((1, n), F32).at[0, :v.shape[0]].set(v.astype(F32))


def _norm_matmul_kernel(x_ref, g_ref, w_ref, o_ref, xn_ref):
    @pl.when(pl.program_id(1) == 0)
    def _():
        x = x_ref[...]
        ms = jnp.mean(x * x, axis=-1, keepdims=True)
        xn_ref[...] = (x * lax.rsqrt(ms + NORM_EPS) * g_ref[...]).astype(BF16)

    o_ref[...] = jnp.dot(xn_ref[...], w_ref[...], preferred_element_type=F32)


def _norm_matmul(x, g, w, layer):
    T, D = x.shape
    Np = w.shape[2]
    tm = min(1024, T)
    tn = 1024
    return pl.pallas_call(
        _norm_matmul_kernel,
        grid=(T // tm, Np // tn),
        in_specs=[pl.BlockSpec((tm, D), lambda i, j: (i, 0)),
                  pl.BlockSpec((1, D), lambda i, j: (0, 0)),
                  pl.BlockSpec((None, D, tn), lambda i, j: (layer, 0, j))],
        out_specs=pl.BlockSpec((tm, tn), lambda i, j: (i, j)),
        out_shape=jax.ShapeDtypeStruct((T, Np), F32),
        scratch_shapes=[pltpu.VMEM((tm, D), BF16)],
        compiler_params=_cparams(("parallel", "arbitrary")),
        name="norm_in_proj",
    )(x, g, w)


def _ssd_kernel(xbc_ref, z_ref, dt_ref, cw_ref, cb_ref, dtb_ref, alog_ref, dskip_ref, nw_ref,
                o_ref, carry_ref, h_ref, y_ref):
    Q = SSD_CHUNK
    GN = SSM_GROUPS * SSM_STATE
    GP = BRANCH_D // SSM_GROUPS

    @pl.when(pl.program_id(1) == 0)
    def _():
        carry_ref[...] = jnp.zeros_like(carry_ref)
        h_ref[...] = jnp.zeros_like(h_ref)

    x = xbc_ref[...]
    carry = carry_ref[...]
    cw = cw_ref[...]
    acc = x * cw[SSM_CONV - 1:SSM_CONV] + cb_ref[...]
    for s in range(1, SSM_CONV):
        acc = acc + _shift_rows(x, carry, s) * cw[SSM_CONV - 1 - s:SSM_CONV - s]
    carry_ref[...] = x[Q - SUBLANES:Q]
    xc = acc * _sigmoid(acc)
    xs = xc[:, :BRANCH_D]
    bm = xc[:, BRANCH_D:BRANCH_D + GN]
    cm = xc[:, BRANCH_D + GN:]

    dt = _softplus(dt_ref[...] + dtb_ref[...])
    adt = dt * (-jnp.exp(alog_ref[...]))
    a_cs = _mm_hi(_tril(Q).astype(F32), adt, exact="a")
    a_cs_t = a_cs.T
    a_last = a_cs[Q - 1:Q]
    ex = _head_expand()
    dt_e = _mm_hi(dt, ex)
    ecs_e = _mm_hi(jnp.exp(a_cs), ex)
    ds_e = _mm_hi(jnp.exp(a_last - a_cs), ex)
    cd_e = _mm_hi(jnp.exp(a_last), ex)
    xd = xs * dt_e
    xds = xd * ds_e
    causal = _tril(Q)

    G = range(SSM_GROUPS)
    H = range(N_HEADS)
    rep = N_HEADS // SSM_GROUPS
    bm_b, cm_b, xd_b, xds_b = (t.astype(BF16) for t in (bm, cm, xd, xds))
    bm_g = [bm_b[:, g * SSM_STATE:(g + 1) * SSM_STATE] for g in G]
    cm_g = [cm_b[:, g * SSM_STATE:(g + 1) * SSM_STATE] for g in G]
    hg = [h_ref[g] for g in G]
    cb = [_mm_nt(cm_g[g], bm_g[g]) for g in G]
    lmat = [jnp.exp(jnp.where(causal, a_cs[:, h:h + 1] - a_cs_t[h:h + 1, :], -1e30)) for h in H]
    y_diag = [_mm(cb[h // rep] * lmat[h], xd_b[:, h * HEAD_DIM:(h + 1) * HEAD_DIM]) for h in H]
    y_off = [_mm(cm_g[g], hg[g]) * ecs_e[:, g * GP:(g + 1) * GP] for g in G]
    h_new = [hg[g] * cd_e[:, g * GP:(g + 1) * GP] + _mm_tn(bm_g[g], xds_b[:, g * GP:(g + 1) * GP]) for g in G]
    for g in G:
        h_ref[g] = h_new[g]
        y_ref[:, g * GP:(g + 1) * GP] = y_off[g]
    for h in H:
        y_ref[:, h * HEAD_DIM:(h + 1) * HEAD_DIM] += y_diag[h]

    y = y_ref[...] + dskip_ref[...] * xs
    z = z_ref[...]
    yz = y * (z * _sigmoid(z))
    nw = nw_ref[...]
    for g in range(SSM_GROUPS):
        seg = yz[:, g * GP:(g + 1) * GP]
        ms = jnp.mean(seg * seg, axis=-1, keepdims=True)
        o_ref[:, g * GP:(g + 1) * GP] = (seg * lax.rsqrt(ms + NORM_EPS) * nw[:, g * GP:(g + 1) * GP]).astype(BF16)


def _ssd_branch(proj, B, S, conv_w, conv_b, dt_bias, a_log, d_skip, norm_w):
    Q = SSD_CHUNK
    nc = S // Q
    row = lambda b, c: b * nc + c
    full = lambda shape: pl.BlockSpec(shape, lambda b, c: (0, 0))
    return pl.pallas_call(
        _ssd_kernel,
        grid=(B, nc),
        in_specs=[pl.BlockSpec((Q, SSM_CONV_DIM), lambda b, c: (row(b, c), OFF_XBC // SSM_CONV_DIM)),
                  pl.BlockSpec((Q, BRANCH_D), lambda b, c: (row(b, c), OFF_Z // BRANCH_D)),
                  pl.BlockSpec((Q, LANES), lambda b, c: (row(b, c), OFF_DT // LANES)),
                  full((SSM_CONV, SSM_CONV_DIM)), full((1, SSM_CONV_DIM)), full((1, LANES)), full((1, LANES)),
                  full((1, BRANCH_D)), full((1, BRANCH_D))],
        out_specs=pl.BlockSpec((Q, BRANCH_D), lambda b, c: (row(b, c), 0)),
        out_shape=jax.ShapeDtypeStruct((B * S, BRANCH_D), BF16),
        scratch_shapes=[pltpu.VMEM((SUBLANES, SSM_CONV_DIM), F32),
                        pltpu.VMEM((SSM_GROUPS, SSM_STATE, BRANCH_D // SSM_GROUPS), F32),
                        pltpu.VMEM((Q, BRANCH_D), F32)],
        compiler_params=_cparams(("parallel", "arbitrary")),
        name="ssd_branch",
    )(proj, proj, proj, conv_w, conv_b[None, :], _pad_lanes(dt_bias), _pad_lanes(a_log),
      jnp.repeat(d_skip, HEAD_DIM)[None, :], norm_w[None, :])


def _mix(x, carry_ref, mu):
    rolled = pltpu.roll(x, 1, 0)
    rows = lax.broadcasted_iota(I32, x.shape, 0)
    prev = jnp.where(rows == 0, carry_ref[SUBLANES - 1:SUBLANES], rolled)
    carry_ref[...] = x[x.shape[0] - SUBLANES:]
    return x + (prev - x) * mu


def _rwkv_kernel(r_ref, k_ref, v_ref, g_ref, lo_ref, mu_r_ref, mu_k_ref, mu_v_ref, mu_lo_ref,
                 w0_ref, w2_ref, a0_ref, a2_ref, kk_ref, ka_ref, rk_ref, lnw_ref, lnb_ref,
                 o_ref, cr_ref, ck_ref, cv_ref, clo_ref, s_ref, y_ref):
    C = RWKV_CHUNK
    R = RWKV_ROWS
    n_sub = R // C

    @pl.when(pl.program_id(1) == 0)
    def _():
        for ref in (cr_ref, ck_ref, cv_ref, clo_ref, s_ref):
            ref[...] = jnp.zeros_like(ref)

    r = _mix(r_ref[...], cr_ref, mu_r_ref[...])
    k = _mix(k_ref[...], ck_ref, mu_k_ref[...])
    v = _mix(v_ref[...], cv_ref, mu_v_ref[...])
    lo = _mix(lo_ref[...], clo_ref, mu_lo_ref[...])
    wd = lo[:, :RWKV_LORA]
    ad = lo[:, RWKV_LORA:]

    wlog = -_softplus(-(w0_ref[...] + _mm(jnp.tanh(wd), w2_ref[...]))) - 0.5
    lw = -jnp.exp(wlog)
    a = _sigmoid(a0_ref[...] + _mm(ad, a2_ref[...]))

    ex = _head_expand()
    red = _head_reduce()
    kk = k * kk_ref[...]
    nrm = jnp.maximum(jnp.sqrt(_mm_hi(kk * kk, red)), 1e-12)
    kk = kk * _mm_hi(1.0 / nrm, ex)
    kmod = k * (1.0 + (a - 1.0) * ka_ref[...])
    b = kk * a

    ri = lax.broadcasted_iota(I32, (R, R), 0)
    ci = lax.broadcasted_iota(I32, (R, R), 1)
    chunk_tril = jnp.where((ci <= ri) & ((ri // C) == (ci // C)), 1.0, 0.0)
    cum = _mm_hi(chunk_tril, lw, exact="a")
    gam = jnp.exp(cum)
    igam = jnp.exp(-cum)
    at = jnp.exp(cum - lw) * kk
    bt = b * igam
    kt = kmod * igam
    rt = r * gam
    at_b, bt_b, kt_b, rt_b, v_b = (t.astype(BF16) for t in (at, bt, kt, rt, v))

    H = range(N_HEADS)
    P = range(N_HEADS // 2)
    lane = lax.broadcasted_iota(I32, (C, LANES), 1)
    rowi = lax.broadcasted_iota(I32, (C, LANES), 0)
    low = lane < HEAD_DIM
    lane_in = lane & (HEAD_DIM - 1)
    strict2 = lane_in < rowi
    incl2 = lane_in <= rowi
    eye2 = jnp.where(lane_in == rowi, 1.0, 0.0)
    low2 = jnp.concatenate([low, low], axis=0)
    zeros_b = jnp.zeros((C, LANES), BF16)
    blockdiag = (lax.broadcasted_iota(I32, (LANES, LANES), 0) < HEAD_DIM) == (
        lax.broadcasted_iota(I32, (LANES, LANES), 1) < HEAD_DIM)
    nat = lambda h, a, b: jnp.where(low, a, b) if h % 2 == 0 else jnp.where(low, b, a)
    nat2 = lambda h, a, b: jnp.where(low2, a, b) if h % 2 == 0 else jnp.where(low2, b, a)
    rows = lambda top, bot: jnp.concatenate([top, bot], axis=0)
    nat_rows = lambda h, t: rows(t, zeros_b) if h % 2 == 0 else rows(zeros_b, t)
    oth_rows = lambda h, t: rows(zeros_b, t) if h % 2 == 0 else rows(t, zeros_b)

    def front(sub):
        cols = lambda t: [t[sub * C:(sub + 1) * C, p * LANES:(p + 1) * LANES] for p in P]
        at_c, bt_c, kt_c, rt_c, rt_f = cols(at_b), cols(bt_b), cols(kt_b), cols(rt_b), cols(rt)
        p1 = [_mm_nt(rows(nat(h, at_c[h // 2], zeros_b), nat(h, rt_c[h // 2], zeros_b)),
                     rows(bt_c[h // 2], kt_c[h // 2]) if h % 2 == 0 else rows(kt_c[h // 2], bt_c[h // 2])) for h in H]
        top = [jnp.where(strict2, p[:C], 0.0) for p in p1]
        bot = [jnp.where(incl2, p[C:], 0.0) for p in p1]
        tp = [nat(h, -top[h], eye2) for h in H]
        for _ in range(int(math.log2(C))):
            tp_b = [t.astype(BF16) for t in tp]
            out = [_mm(tp_b[h], nat_rows(h, tp_b[h])) for h in H]
            tp = [out[h] + nat(h, 0.0, tp[h]) for h in H]
        tp_b = [t.astype(BF16) for t in tp]
        top_b = [t.astype(BF16) for t in top]
        bot_b = [t.astype(BF16) for t in bot]
        xz = [_mm(tp_b[h], oth_rows(h, nat(h, at_c[h // 2], top_b[h]))) for h in H]
        xz_b = [t.astype(BF16) for t in xz]
        mxz = [_mm(bot_b[h], nat_rows(h, xz_b[h])) for h in H]
        ra = [nat(h, rt_f[h // 2], bot[h]) - mxz[h] for h in H]
        return [rows(ra[h], xz[h]).astype(BF16) for h in H]

    def tail(sub, stack):
        cols = lambda t: [t[sub * C:(sub + 1) * C, p * LANES:(p + 1) * LANES] for p in P]
        bt_c, kt_c, v_c = cols(bt_b), cols(kt_b), cols(v_b)
        g_last = gam[(sub + 1) * C - 1:(sub + 1) * C]
        s_b = [s_ref[p].astype(BF16) for p in P]
        yw = [_mm_nt(nat2(h, stack[h], jnp.zeros_like(stack[h])), s_b[h // 2]) + _mm(stack[h], oth_rows(h, v_c[h // 2]))
              for h in H]
        yw_p = [jnp.where(low2, yw[2 * p], yw[2 * p + 1]) for p in P]
        inc = [_mm_tn(rows(v_c[p], (-yw_p[p][C:]).astype(BF16)), rows(kt_c[p], bt_c[p])) for p in P]
        for p in P:
            y_ref[sub * C:(sub + 1) * C, p * LANES:(p + 1) * LANES] = yw_p[p][:C]
            s_ref[p] = (s_ref[p] + jnp.where(blockdiag, inc[p], 0.0)) * g_last[:, p * LANES:(p + 1) * LANES]

    stacks = [front(sub) for sub in range(n_sub)]
    for sub in range(n_sub):
        tail(sub, stacks[sub])

    y = y_ref[...]
    inv_n = 1.0 / HEAD_DIM
    mean = _mm_hi(_mm_hi(y, red) * inv_n, ex)
    yc = y - mean
    var = _mm_hi(yc * yc, red) * inv_n
    yn = yc * _mm_hi(lax.rsqrt(var + RWKV_GN_EPS), ex) * lnw_ref[...] + lnb_ref[...]
    bonus = _mm_hi(_mm_hi(r * kmod * rk_ref[...], red), ex) * v
    gate = g_ref[...]
    o_ref[...] = ((yn + bonus) * (gate * _sigmoid(gate))).astype(BF16)


def _rwkv_branch(proj, B, S, mu, w0, w2, a0, a2, k_k, k_a, r_k, ln_w, ln_b):
    R = RWKV_ROWS
    nc = S // R
    D = BRANCH_D
    row = lambda b, c: b * nc + c
    full = lambda shape: pl.BlockSpec(shape, lambda b, c: (0, 0))
    slab = lambda off: pl.BlockSpec((R, D), lambda b, c: (row(b, c), off // D))
    mu_r, mu_wd, mu_k, mu_v, mu_ad = (mu[:D], mu[D:D + RWKV_LORA], mu[D + RWKV_LORA:2 * D + RWKV_LORA],
                                      mu[2 * D + RWKV_LORA:3 * D + RWKV_LORA], mu[3 * D + RWKV_LORA:])
    r1 = lambda t: t.reshape(1, -1).astype(F32)
    return pl.pallas_call(
        _rwkv_kernel,
        grid=(B, nc),
        in_specs=[slab(OFF_RB), slab(OFF_KB), slab(OFF_VB), slab(OFF_GB),
                  pl.BlockSpec((R, LANES), lambda b, c: (row(b, c), OFF_LORA // LANES)),
                  full((1, D)), full((1, D)), full((1, D)), full((1, LANES)),
                  full((1, D)), full((RWKV_LORA, D)), full((1, D)), full((RWKV_LORA, D)),
                  full((1, D)), full((1, D)), full((1, D)), full((1, D)), full((1, D))],
        out_specs=pl.BlockSpec((R, D), lambda b, c: (row(b, c), 0)),
        out_shape=jax.ShapeDtypeStruct((B * S, D), BF16),
        scratch_shapes=[pltpu.VMEM((SUBLANES, D), F32), pltpu.VMEM((SUBLANES, D), F32),
                        pltpu.VMEM((SUBLANES, D), F32), pltpu.VMEM((SUBLANES, LANES), F32),
                        pltpu.VMEM((N_HEADS // 2, LANES, LANES), F32),
                        pltpu.VMEM((R, D), F32)],
        compiler_params=_cparams(("parallel", "arbitrary")),
        name="rwkv_branch",
    )(proj, proj, proj, proj, proj, r1(mu_r), r1(mu_k), r1(mu_v), r1(jnp.concatenate([mu_wd, mu_ad])),
      r1(w0), w2, r1(a0), a2, r1(k_k), r1(k_a), r1(r_k), r1(ln_w), r1(ln_b))


def _rope_table_kernel(pos_ref, inv_ref, cos_ref, sin_ref):
    ang = pos_ref[...] * inv_ref[...]
    lane = lax.broadcasted_iota(I32, ang.shape, 1)
    cos_ref[...] = jnp.cos(ang)
    s = jnp.sin(ang)
    sin_ref[...] = jnp.where((lane & (HEAD_DIM - 1)) < HEAD_DIM // 2, -s, s)


def _rope_tables(positions):
    T = positions.size
    tq = min(512, T)
    pos = positions.reshape(T, 1).astype(F32)
    inv = ROPE_THETA ** (-(jnp.arange(HEAD_DIM // 2, dtype=F32) * 2.0 / HEAD_DIM))
    inv = jnp.tile(inv, LANES // (HEAD_DIM // 2))[None, :]
    return pl.pallas_call(
        _rope_table_kernel,
        grid=(T // tq,),
        in_specs=[pl.BlockSpec((tq, 1), lambda i: (i, 0)), pl.BlockSpec((1, LANES), lambda i: (0, 0))],
        out_specs=[pl.BlockSpec((tq, LANES), lambda i: (i, 0))] * 2,
        out_shape=[jax.ShapeDtypeStruct((T, LANES), F32)] * 2,
        compiler_params=_cparams(("parallel",)),
        name="rope_tables",
    )(pos, inv)


def _rope(x, cos, sin_signed):
    lane = lax.broadcasted_iota(I32, (1, LANES), 1)
    first = (lane & (HEAD_DIM - 1)) < HEAD_DIM // 2
    outs = []
    for c in range(x.shape[1] // LANES):
        xb = x[:, c * LANES:(c + 1) * LANES]
        partner = jnp.where(first, pltpu.roll(xb, LANES - HEAD_DIM // 2, 1), pltpu.roll(xb, HEAD_DIM // 2, 1))
        outs.append(xb * cos + partner * sin_signed)
    return outs[0] if len(outs) == 1 else jnp.concatenate(outs, axis=1)


def _dsa_prep_kernel(q_ref, iq_ref, k_ref, v_ref, idx_ref, cos_ref, sin_ref, nw_ref, nb_ref,
                     qt_ref, iqt_ref, ko_ref, vat_ref, iko_ref, iwt_ref):
    QB = DSA_QB
    cos = cos_ref[...]
    sin = sin_ref[...]
    nblk = q_ref.shape[0] // QB
    qt = (_rope(q_ref[...], cos, sin) * (HEAD_DIM ** -0.5 * math.log2(math.e))).T
    iqt = _rope(iq_ref[...], cos, sin).T
    for j in range(nblk):
        qt_ref[j] = qt[:, j * QB:(j + 1) * QB].astype(BF16)
        iqt_ref[j] = iqt[:, j * QB:(j + 1) * QB].astype(BF16)
    ko_ref[...] = _rope(k_ref[...], cos, sin).astype(BF16)
    vt = v_ref[...].T
    ones = jnp.ones((HEAD_DIM, vt.shape[1]), F32)
    vat_ref[0] = jnp.concatenate(
        [t for g in range(ATT_KV_HEADS) for t in (vt[g * HEAD_DIM:(g + 1) * HEAD_DIM], ones)], axis=0).astype(BF16)
    idx = idx_ref[...]
    lane = lax.broadcasted_iota(I32, idx.shape, 1)
    is_k = lane < IDX_DIM
    mu = jnp.sum(jnp.where(is_k, idx, 0.0), axis=-1, keepdims=True) * (1.0 / IDX_DIM)
    dk = jnp.where(is_k, idx - mu, 0.0)
    var = jnp.sum(dk * dk, axis=-1, keepdims=True) * (1.0 / IDX_DIM)
    ikn = dk * lax.rsqrt(var + NORM_EPS) * nw_ref[...] + nb_ref[...]
    iko_ref[...] = _rope(ikn, cos, sin).astype(BF16)
    iwt = (pltpu.roll(idx, LANES - IDX_DIM, 1) * (IDX_HEADS ** -0.5 * IDX_DIM ** -0.5)).T
    for j in range(nblk):
        iwt_ref[j] = iwt[:IDX_HEADS, j * QB:(j + 1) * QB]


def _dsa_prep(proj, cos, sin, k_norm_w, k_norm_b):
    T = proj.shape[0]
    tq = DSA_KT
    QB = DSA_QB
    D = BRANCH_D
    VA = 2 * ATT_KV_D
    nb = tq // QB
    spec = lambda w, off: pl.BlockSpec((tq, w), lambda i: (i, off // w))
    out = lambda w: pl.BlockSpec((tq, w), lambda i: (i, 0))
    blk = lambda n: pl.BlockSpec((nb, n, QB), lambda i: (i, 0, 0))
    return pl.pallas_call(
        _dsa_prep_kernel,
        grid=(T // tq,),
        in_specs=[spec(D, OFF_QC), spec(D, OFF_IQ), spec(ATT_KV_D, OFF_KC), spec(ATT_KV_D, OFF_VC),
                  spec(LANES, OFF_IDX), out(LANES), out(LANES),
                  pl.BlockSpec((1, LANES), lambda i: (0, 0)), pl.BlockSpec((1, LANES), lambda i: (0, 0))],
        out_specs=[blk(D), blk(D), out(ATT_KV_D), pl.BlockSpec((1, VA, tq), lambda i: (i, 0, 0)), out(LANES),
                   blk(IDX_HEADS)],
        out_shape=[jax.ShapeDtypeStruct((T // QB, D, QB), BF16), jax.ShapeDtypeStruct((T // QB, D, QB), BF16),
                   jax.ShapeDtypeStruct((T, ATT_KV_D), BF16), jax.ShapeDtypeStruct((T // tq, VA, tq), BF16),
                   jax.ShapeDtypeStruct((T, LANES), BF16), jax.ShapeDtypeStruct((T // QB, IDX_HEADS, QB), F32)],
        compiler_params=_cparams(("parallel",)),
        name="dsa_prep",
    )(proj, proj, proj, proj, proj, cos, sin, _pad_lanes(k_norm_w), _pad_lanes(k_norm_b))


def _dsa_main_kernel(top_k, seq_len, qt_ref, iqt_ref, iwt_ref, gate_ref, ik_ref, k_ref, vat_ref, o_ref,
                     keys_ref, sacc_ref, bias_ref, m_ref, acc_ref, tie_ref, ot_ref, khi_ref, klo_ref):
    QB, KT = DSA_QB, DSA_KT
    qi = pl.program_id(1)
    nk = (qi * QB + QB + KT - 1) // KT
    int_min = jnp.int32(INT_MIN)
    rep = N_HEADS // ATT_KV_HEADS
    kpos0 = lax.broadcasted_iota(I32, (KT, QB), 0)
    qpos = qi * QB + lax.broadcasted_iota(I32, (KT, QB), 1)

    def score_tile(kt, _):
        off = pl.multiple_of(kt * KT, KT)
        ik_t = ik_ref[pl.ds(off, KT), :][:, :IDX_DIM]
        for h0 in range(0, IDX_HEADS, 4):
            lg = [jnp.dot(ik_t, iqt_ref[h * IDX_DIM:(h + 1) * IDX_DIM, :], preferred_element_type=F32)
                  for h in range(h0, h0 + 4)]
            term = sum(jnp.maximum(l, 0.0) * iwt_ref[h:h + 1, :] for l, h in zip(lg, range(h0, h0 + 4)))
            if h0 == 0:
                sacc_ref[...] = term
            else:
                sacc_ref[...] += term
        bits = pltpu.bitcast(sacc_ref[...], I32)
        skey = jnp.where(bits < 0, bits ^ jnp.int32(0x7FFFFFFF), bits)
        key = jnp.where(kpos0 + off <= qpos, skey, int_min)
        keys_ref[kt] = key
        khi_ref[kt] = (key >> 16).astype(I16)
        return 0

    lax.fori_loop(0, nk, score_tile, 0)

    def count(pred):
        def body(kt, acc):
            hit = pred(keys_ref[kt], kpos0 + kt * KT)
            return acc + _fold_rows(jnp.where(hit, 1.0, 0.0), jnp.add)
        acc = lax.fori_loop(0, nk, body, jnp.zeros((SUBLANES, QB), F32))
        return jnp.sum(acc, axis=0, keepdims=True)

    def count16(ref, cand):
        def body(kt, acc):
            return acc + _fold_rows(jnp.where(ref[kt] >= cand, jnp.int16(1), jnp.int16(0)), jnp.add, 2 * SUBLANES)
        acc = lax.fori_loop(0, nk, body, jnp.zeros((2 * SUBLANES, QB), I16))
        return jnp.sum(acc.astype(I32).astype(F32), axis=0, keepdims=True)

    def search16(ref, need):
        def bit_body(i, res):
            cand_u = res | jnp.left_shift(jnp.int32(1), 15 - i)
            cnt = count16(ref, (cand_u - 32768).astype(I16))
            return jnp.where(cnt >= need, cand_u, res)
        return lax.fori_loop(0, 16, bit_body, jnp.zeros((1, QB), I32))

    hi_s = search16(khi_ref, float(top_k)) - 32768
    hi_16 = hi_s.astype(I16)
    n_above = jnp.where(hi_s >= 32767, 0.0, count16(khi_ref, jnp.minimum(hi_s + 1, 32767).astype(I16)))

    def low_tile(kt, _):
        lo = ((keys_ref[kt] << 16) ^ int_min) >> 16
        klo_ref[kt] = jnp.where(khi_ref[kt] == hi_16, lo.astype(I16), jnp.int16(-32768))
        return 0

    lax.fori_loop(0, nk, low_tile, 0)
    lo_u = search16(klo_ref, float(top_k) - n_above)
    thr = jnp.maximum((hi_s << 16) | lo_u, jnp.int32(INT_MIN + 1))

    n_ge = count(lambda k, c: k >= thr)
    tie_ref[...] = jnp.full((1, QB), seq_len, I32)

    @pl.when(jnp.max(n_ge) > float(top_k))
    def _():
        need = float(top_k) - count(lambda k, c: k > thr)
        n_bits = max(1, (seq_len - 1).bit_length())

        def tie_body(i, x):
            cand = x | jnp.left_shift(jnp.int32(1), n_bits - 1 - i)
            below = count(lambda k, c: (k == thr) & (c < cand))
            return jnp.where(below < need, cand, x)

        tie_ref[...] = lax.fori_loop(0, n_bits, tie_body, jnp.zeros((1, QB), I32))

    tie = tie_ref[...]

    m_ref[...] = jnp.full(m_ref.shape, -1e29, F32)
    acc_ref[...] = jnp.zeros(acc_ref.shape, F32)

    def att_tile(kt, _):
        off = pl.multiple_of(kt * KT, KT)
        kk = keys_ref[kt]
        sel = (kk > thr) | ((kk == thr) & (kpos0 + off <= tie))
        bias_ref[...] = jnp.where(sel, 0.0, -1e30)
        k_t = k_ref[pl.ds(off, KT), :]
        hs = range(N_HEADS)
        k_g = [k_t[:, g * HEAD_DIM:(g + 1) * HEAD_DIM] for g in range(ATT_KV_HEADS)]
        va_g = [vat_ref[kt, g * LANES:(g + 1) * LANES, :] for g in range(ATT_KV_HEADS)]
        m_old = [m_ref[h] for h in hs]
        a_old = [acc_ref[h] for h in hs]
        lg = [jnp.dot(k_g[h // rep], qt_ref[h * HEAD_DIM:(h + 1) * HEAD_DIM, :], preferred_element_type=F32)
              + bias_ref[...] for h in hs]
        m_new = [jnp.maximum(mo, jnp.max(_fold_rows(l, jnp.maximum), axis=0, keepdims=True))
                 for mo, l in zip(m_old, lg)]
        p = [jnp.exp2(l - mn).astype(BF16) for l, mn in zip(lg, m_new)]
        pv = [jnp.dot(va_g[h // rep], p[h], preferred_element_type=F32) for h in hs]
        for h in hs:
            acc_ref[h] = a_old[h] * jnp.exp2(m_old[h] - m_new[h]) + pv[h]
            m_ref[h] = m_new[h]
        return 0

    lax.fori_loop(0, nk, att_tile, 0)

    for h in range(N_HEADS):
        a = acc_ref[h]
        ot_ref[h * HEAD_DIM:(h + 1) * HEAD_DIM, :] = a[:HEAD_DIM] / a[HEAD_DIM:]
    gate = gate_ref[...]
    o_ref[...] = (ot_ref[...].T * (gate * _sigmoid(gate))).astype(BF16)


def _dsa_main(proj, qt, iqt, kr, vat, ikr, iwt, B, S, top_k):
    QB, KT = DSA_QB, DSA_KT
    nq = S // QB
    nkt = S // KT
    D = BRANCH_D
    row = lambda b, i: b * nq + i
    return pl.pallas_call(
        functools.partial(_dsa_main_kernel, top_k, S),
        grid=(B, nq),
        in_specs=[pl.BlockSpec((None, D, QB), lambda b, i: (row(b, i), 0, 0)),
                  pl.BlockSpec((None, D, QB), lambda b, i: (row(b, i), 0, 0)),
                  pl.BlockSpec((None, IDX_HEADS, QB), lambda b, i: (row(b, i), 0, 0)),
                  pl.BlockSpec((QB, D), lambda b, i: (row(b, i), OFF_GC // D)),
                  pl.BlockSpec((S, LANES), lambda b, i: (b, 0)),
                  pl.BlockSpec((S, ATT_KV_D), lambda b, i: (b, 0)),
                  pl.BlockSpec((nkt, 2 * ATT_KV_D, KT), lambda b, i: (b, 0, 0))],
        out_specs=pl.BlockSpec((QB, D), lambda b, i: (row(b, i), 0)),
        out_shape=jax.ShapeDtypeStruct((B * S, D), BF16),
        scratch_shapes=[pltpu.VMEM((nkt, KT, QB), I32),
                        pltpu.VMEM((KT, QB), F32),
                        pltpu.VMEM((KT, QB), F32),
                        pltpu.VMEM((N_HEADS, 1, QB), F32),
                        pltpu.VMEM((N_HEADS, LANES, QB), F32),
                        pltpu.VMEM((1, QB), I32),
                        pltpu.VMEM((D, QB), F32),
                        pltpu.VMEM((nkt, KT, QB), I16),
                        pltpu.VMEM((nkt, KT, QB), I16)],
        compiler_params=_cparams(("parallel", "arbitrary")),
        name="dsa_main",
    )(qt, iqt, iwt, proj, ikr, kr, vat)


def _dsa_branch(proj, cos, sin, B, S, k_norm_w, k_norm_b, top_k):
    qt, iqt, kr, vat, ikr, iwt = _dsa_prep(proj, cos, sin, k_norm_w, k_norm_b)
    return _dsa_main(proj, qt, iqt, kr, vat, ikr, iwt, B, S, top_k)


def _merge_kernel(ya_ref, yb_ref, yc_ref, ga_ref, gb_ref, gc_ref, ba_ref, bb_ref, bc_ref,
                  wa_ref, wb_ref, wc_ref, o_ref):
    acc = _sigmoid(ga_ref[...] + ba_ref[...]) * jnp.dot(ya_ref[...], wa_ref[...], preferred_element_type=F32)
    acc += _sigmoid(gb_ref[...] + bb_ref[...]) * jnp.dot(yb_ref[...], wb_ref[...], preferred_element_type=F32)
    acc += _sigmoid(gc_ref[...] + bc_ref[...]) * jnp.dot(yc_ref[...], wc_ref[...], preferred_element_type=F32)
    o_ref[...] = acc.astype(BF16)


def _merge(ya, yb, yc, proj, b_gate, wa, wb, wc, layer):
    T = ya.shape[0]
    D, Db = D_MODEL, BRANCH_D
    tm = min(256, T)
    y = pl.BlockSpec((tm, Db), lambda i: (i, 0))
    gate = lambda n: pl.BlockSpec((tm, D), lambda i: (i, OFF_GATES // D + n))
    bias = lambda n: pl.BlockSpec((1, D), lambda i: (0, n))
    w = pl.BlockSpec((None, Db, D), lambda i: (layer, 0, 0))
    bg = b_gate[None, :]
    return pl.pallas_call(
        _merge_kernel,
        grid=(T // tm,),
        in_specs=[y, y, y, gate(0), gate(1), gate(2), bias(0), bias(1), bias(2), w, w, w],
        out_specs=pl.BlockSpec((tm, D), lambda i: (i, 0)),
        out_shape=jax.ShapeDtypeStruct((T, D), BF16),
        compiler_params=_cparams(("parallel",)),
        name="gated_merge",
    )(ya, yb, yc, proj, proj, proj, bg, bg, bg, wa, wb, wc)


def _out_proj_kernel(m_ref, w_ref, g_ref, x_ref, o_ref):
    y = jnp.dot(m_ref[...], w_ref[...], preferred_element_type=F32)
    ms = jnp.mean(y * y, axis=-1, keepdims=True)
    o_ref[...] = x_ref[...] + y * lax.rsqrt(ms + NORM_EPS) * g_ref[...]


def _out_proj(merged, w_out, g, x, layer):
    T, D = x.shape
    tm = min(256, T)
    return pl.pallas_call(
        _out_proj_kernel,
        grid=(T // tm,),
        in_specs=[pl.BlockSpec((tm, D), lambda i: (i, 0)), pl.BlockSpec((None, D, D), lambda i: (layer, 0, 0)),
                  pl.BlockSpec((1, D), lambda i: (0, 0)), pl.BlockSpec((tm, D), lambda i: (i, 0))],
        out_specs=pl.BlockSpec((tm, D), lambda i: (i, 0)),
        out_shape=jax.ShapeDtypeStruct((T, D), F32),
        compiler_params=_cparams(("parallel",)),
        name="out_proj_norm_residual",
    )(merged, w_out, g, x)


def _w_in_pieces():
    a0, b0 = 0, A_COLS
    c0 = A_COLS + B_COLS
    g0 = c0 + C_COLS
    D = BRANCH_D
    return [
        (a0 + D, SSM_CONV_DIM, OFF_XBC),
        (a0, D, OFF_Z),
        (b0, D, OFF_RB),
        (b0 + D + RWKV_LORA, D, OFF_KB),
        (b0 + 2 * D + RWKV_LORA, D, OFF_VB),
        (b0 + RWKV_SHIFT_DIM, D, OFF_GB),
        (c0, D, OFF_QC),
        (c0 + D + 2 * ATT_KV_D, D, OFF_GC),
        (c0 + 2 * D + 2 * ATT_KV_D, D, OFF_IQ),
        (g0, GATE_COLS, OFF_GATES),
        (c0 + D, ATT_KV_D, OFF_KC),
        (c0 + D + ATT_KV_D, ATT_KV_D, OFF_VC),
        (a0 + D + SSM_CONV_DIM, N_HEADS, OFF_DT),
        (b0 + D, RWKV_LORA, OFF_LORA),
        (b0 + 3 * D + RWKV_LORA, RWKV_LORA, OFF_LORA + RWKV_LORA),
        (c0 + 3 * D + 2 * ATT_KV_D, IDX_DIM + IDX_HEADS, OFF_IDX),
    ]


def _pack_w_in_kernel(w_ref, o_ref):
    o_ref[...] = jnp.zeros(o_ref.shape, BF16)
    for src, width, dst in _w_in_pieces():
        o_ref[:, dst:dst + width] = w_ref[:, src:src + width].astype(BF16)


def _pack_w_in(w_in):
    L, D, n_in = w_in.shape
    tr = 128
    nr = D // tr
    return pl.pallas_call(
        _pack_w_in_kernel,
        grid=(L, nr),
        in_specs=[pl.BlockSpec((tr, n_in), lambda l, i: (l * nr + i, 0))],
        out_specs=pl.BlockSpec((None, tr, NP_COLS), lambda l, i: (l, i, 0)),
        out_shape=jax.ShapeDtypeStruct((L, D, NP_COLS), BF16),
        compiler_params=_cparams(("parallel", "parallel")),
        name="pack_w_in",
    )(w_in.reshape(L * D, n_in))


def kernel(x, positions, pre_norm, post_norm, w_in, b_gate, ssm_conv_w, ssm_conv_b, ssm_dt_bias, ssm_a_log,
           ssm_d, ssm_norm, rwkv_mu, rwkv_w0, rwkv_w2, rwkv_a0, rwkv_a2, rwkv_k_k, rwkv_k_a, rwkv_r_k,
           rwkv_ln_w, rwkv_ln_b, idx_k_norm_w, idx_k_norm_b, w_branch_a, w_branch_b, w_branch_c, w_out):
    B, S, D = x.shape
    depth = w_in.shape[0]
    top_k = min(TOPK_MAX, S // 4)
    xt = x.reshape(B * S, D)
    w_in_p = _pack_w_in(w_in)
    wa, wb, wc, wo = (w.astype(BF16) for w in (w_branch_a, w_branch_b, w_branch_c, w_out))
    cos, sin = _rope_tables(positions)
    for i in range(depth):
        proj = _norm_matmul(xt, pre_norm[i][None, :], w_in_p, i)
        ya = _ssd_branch(proj, B, S, ssm_conv_w[i], ssm_conv_b[i], ssm_dt_bias[i], ssm_a_log[i], ssm_d[i], ssm_norm[i])
        yb = _rwkv_branch(proj, B, S, rwkv_mu[i], rwkv_w0[i], rwkv_w2[i], rwkv_a0[i], rwkv_a2[i], rwkv_k_k[i],
                          rwkv_k_a[i], rwkv_r_k[i], rwkv_ln_w[i], rwkv_ln_b[i])
        yc = _dsa_branch(proj, cos, sin, B, S, idx_k_norm_w[i], idx_k_norm_b[i], top_k)
        merged = _merge(ya, yb, yc, proj, b_gate[i], wa, wb, wc, i)
        xt = _out_proj(merged, wo, post_norm[i][None, :], xt, i)
    return xt.reshape(B, S, D)
```

```python
import functools
import math

import jax
import jax.numpy as jnp
from jax import lax
from jax.experimental import pallas as pl
from jax.experimental.pallas import tpu as pltpu

F32 = jnp.float32
BF16 = jnp.bfloat16
I32 = jnp.int32
I16 = jnp.int16

D_MODEL = 2048
BRANCH_D = D_MODEL // 2
NORM_EPS = 1e-6
HEAD_DIM = 64
N_HEADS = BRANCH_D // HEAD_DIM

SSM_GROUPS = 4
SSM_STATE = 128
SSM_CONV = 4
SSD_CHUNK = 128
SSM_CONV_DIM = BRANCH_D + 2 * SSM_GROUPS * SSM_STATE
A_COLS = BRANCH_D + SSM_CONV_DIM + N_HEADS

RWKV_LORA = 64
RWKV_SHIFT_DIM = 3 * BRANCH_D + 2 * RWKV_LORA
B_COLS = RWKV_SHIFT_DIM + BRANCH_D
RWKV_GN_EPS = 64e-5
RWKV_CHUNK = 64
RWKV_ROWS = 256

ATT_KV_HEADS = 4
ATT_KV_D = ATT_KV_HEADS * HEAD_DIM
IDX_HEADS = 16
IDX_DIM = 64
TOPK_MAX = 256
ROPE_THETA = 10000.0
C_COLS = BRANCH_D + 2 * ATT_KV_D + BRANCH_D + IDX_HEADS * IDX_DIM + IDX_DIM + IDX_HEADS
GATE_COLS = 3 * D_MODEL

DSA_QB = 256
DSA_KT = 512

LANES = 128
SUBLANES = 8
VMEM_LIMIT = 56 * 1024 * 1024

OFF_XBC = 0
OFF_Z = 2048
OFF_RB = 3072
OFF_KB = 4096
OFF_VB = 5120
OFF_GB = 6144
OFF_QC = 7168
OFF_GC = 8192
OFF_IQ = 9216
OFF_GATES = 10240
OFF_KC = 16384
OFF_VC = 16640
OFF_DT = 16896
OFF_LORA = 17024
OFF_IDX = 17152
NP_COLS = 17408

INT_MIN = -2 ** 31


def _cparams(sem):
    return pltpu.CompilerParams(dimension_semantics=sem, vmem_limit_bytes=VMEM_LIMIT)


def _mm(a, b):
    return jnp.dot(a.astype(BF16), b.astype(BF16), preferred_element_type=F32)


def _mm_nt(a, b):
    return lax.dot_general(a.astype(BF16), b.astype(BF16), (((1,), (1,)), ((), ())),
                           preferred_element_type=F32)


def _mm_tn(a, b):
    return lax.dot_general(a.astype(BF16), b.astype(BF16), (((0,), (0,)), ((), ())),
                           preferred_element_type=F32)


def _mm_hi(a, b, exact="b"):
    x, m = (a, b) if exact == "b" else (b, a)
    m = m.astype(BF16)
    acc = None
    for _ in range(3):
        xb = x.astype(BF16)
        part = jnp.dot(xb, m, preferred_element_type=F32) if exact == "b" else jnp.dot(m, xb, preferred_element_type=F32)
        acc = part if acc is None else acc + part
        x = x - xb.astype(F32)
    return acc


def _head_expand():
    r = lax.broadcasted_iota(I32, (LANES, BRANCH_D), 0)
    c = lax.broadcasted_iota(I32, (LANES, BRANCH_D), 1)
    return jnp.where((c >> 6) == r, 1.0, 0.0).astype(F32)


def _head_reduce():
    r = lax.broadcasted_iota(I32, (BRANCH_D, LANES), 0)
    c = lax.broadcasted_iota(I32, (BRANCH_D, LANES), 1)
    return jnp.where((r >> 6) == c, 1.0, 0.0).astype(F32)


def _tril(n):
    r = lax.broadcasted_iota(I32, (n, n), 0)
    c = lax.broadcasted_iota(I32, (n, n), 1)
    return c <= r


def _softplus(x):
    return jnp.maximum(x, 0.0) + jnp.log1p(jnp.exp(-jnp.abs(x)))


def _sigmoid(x):
    return 1.0 / (1.0 + jnp.exp(-x))


def _shift_rows(x, carry, s):
    rolled = pltpu.roll(x, s, 0)
    rows = lax.broadcasted_iota(I32, (SUBLANES, x.shape[1]), 0)
    top = jnp.where(rows < s, pltpu.roll(carry, s, 0), rolled[0:SUBLANES])
    return jnp.concatenate([top, rolled[SUBLANES:]], axis=0)


def _fold_rows(x, op, group=SUBLANES):
    parts = x.reshape(x.shape[0] // group, group, x.shape[1])
    while parts.shape[0] > 1:
        half = parts.shape[0] // 2
        parts = op(parts[:half], parts[half:])
    return parts[0]


def _pad_lanes(v, n=LANES):
    return jnp.zeros((1, n), F32).at[0, :v.shape[0]].set(v.astype(F32))


def _norm_matmul_kernel(x_ref, g_ref, w_ref, o_ref, xn_ref):
    @pl.when(pl.program_id(1) == 0)
    def _():
        x = x_ref[...]
        ms = jnp.mean(x * x, axis=-1, keepdims=True)
        xn_ref[...] = (x * lax.rsqrt(ms + NORM_EPS) * g_ref[...]).astype(BF16)

    o_ref[...] = jnp.dot(xn_ref[...], w_ref[...], preferred_element_type=F32)


def _norm_matmul(x, g, w, layer):
    T, D = x.shape
    Np = w.shape[2]
    tm = min(1024, T)
    tn = 1024
    return pl.pallas_call(
        _norm_matmul_kernel,
        grid=(T // tm, Np // tn),
        in_specs=[pl.BlockSpec((tm, D), lambda i, j: (i, 0)),
                  pl.BlockSpec((1, D), lambda i, j: (0, 0)),
                  pl.BlockSpec((None, D, tn), lambda i, j: (layer, 0, j))],
        out_specs=pl.BlockSpec((tm, tn), lambda i, j: (i, j)),
        out_shape=jax.ShapeDtypeStruct((T, Np), F32),
        scratch_shapes=[pltpu.VMEM((tm, D), BF16)],
        compiler_params=_cparams(("parallel", "arbitrary")),
        name="norm_in_proj",
    )(x, g, w)


def _ssd_chunk(r0, xbc_ref, z_ref, dt_ref, cw_ref, cb_ref, dtb_ref, alog_ref, dskip_ref, nw_ref,
               o_ref, carry_ref, h_ref, y_ref):
    Q = SSD_CHUNK
    GN = SSM_GROUPS * SSM_STATE
    GP = BRANCH_D // SSM_GROUPS

    x = xbc_ref[r0:r0 + Q, :]
    carry = carry_ref[...]
    cw = cw_ref[...]
    acc = x * cw[SSM_CONV - 1:SSM_CONV] + cb_ref[...]
    for s in range(1, SSM_CONV):
        acc = acc + _shift_rows(x, carry, s) * cw[SSM_CONV - 1 - s:SSM_CONV - s]
    carry_ref[...] = x[Q - SUBLANES:Q]
    xc = acc * _sigmoid(acc)
    xs = xc[:, :BRANCH_D]
    bm = xc[:, BRANCH_D:BRANCH_D + GN]
    cm = xc[:, BRANCH_D + GN:]

    dt = _softplus(dt_ref[r0:r0 + Q, :] + dtb_ref[...])
    adt = dt * (-jnp.exp(alog_ref[...]))
    a_cs = _mm_hi(_tril(Q).astype(F32), adt, exact="a")
    a_cs_t = a_cs.T
    a_last = a_cs[Q - 1:Q]
    ex = _head_expand()
    dt_e = _mm_hi(dt, ex)
    ecs_e = _mm_hi(jnp.exp(a_cs), ex)
    ds_e = _mm_hi(jnp.exp(a_last - a_cs), ex)
    cd_e = _mm_hi(jnp.exp(a_last), ex)
    xd = xs * dt_e
    xds = xd * ds_e
    causal = _tril(Q)

    G = range(SSM_GROUPS)
    H = range(N_HEADS)
    rep = N_HEADS // SSM_GROUPS
    bm_b, cm_b, xd_b, xds_b = (t.astype(BF16) for t in (bm, cm, xd, xds))
    bm_g = [bm_b[:, g * SSM_STATE:(g + 1) * SSM_STATE] for g in G]
    cm_g = [cm_b[:, g * SSM_STATE:(g + 1) * SSM_STATE] for g in G]
    hg = [h_ref[g] for g in G]
    cb = [_mm_nt(cm_g[g], bm_g[g]) for g in G]
    lmat = [jnp.exp(jnp.where(causal, a_cs[:, h:h + 1] - a_cs_t[h:h + 1, :], -1e30)) for h in H]
    y_diag = [_mm(cb[h // rep] * lmat[h], xd_b[:, h * HEAD_DIM:(h + 1) * HEAD_DIM]) for h in H]
    y_off = [_mm(cm_g[g], hg[g]) * ecs_e[:, g * GP:(g + 1) * GP] for g in G]
    h_new = [hg[g] * cd_e[:, g * GP:(g + 1) * GP] + _mm_tn(bm_g[g], xds_b[:, g * GP:(g + 1) * GP]) for g in G]
    for g in G:
        h_ref[g] = h_new[g]
        y_ref[:, g * GP:(g + 1) * GP] = y_off[g]
    for h in H:
        y_ref[:, h * HEAD_DIM:(h + 1) * HEAD_DIM] += y_diag[h]

    y = y_ref[...] + dskip_ref[...] * xs
    z = z_ref[r0:r0 + Q, :]
    yz = y * (z * _sigmoid(z))
    nw = nw_ref[...]
    for g in range(SSM_GROUPS):
        seg = yz[:, g * GP:(g + 1) * GP]
        ms = jnp.mean(seg * seg, axis=-1, keepdims=True)
        o_ref[r0:r0 + Q, g * GP:(g + 1) * GP] = (seg * lax.rsqrt(ms + NORM_EPS) * nw[:, g * GP:(g + 1) * GP]).astype(BF16)


def _mix(x, carry_ref, mu):
    rolled = pltpu.roll(x, 1, 0)
    rows = lax.broadcasted_iota(I32, x.shape, 0)
    prev = jnp.where(rows == 0, carry_ref[SUBLANES - 1:SUBLANES], rolled)
    carry_ref[...] = x[x.shape[0] - SUBLANES:]
    return x + (prev - x) * mu


def _rwkv_ssd_kernel(r_ref, k_ref, v_ref, g_ref, lo_ref, mu_r_ref, mu_k_ref, mu_v_ref, mu_lo_ref,
                     w0_ref, w2_ref, a0_ref, a2_ref, kk_ref, ka_ref, rk_ref, lnw_ref, lnb_ref,
                     xbc_ref, z_ref, dt_ref, cw_ref, cb_ref, dtb_ref, alog_ref, dskip_ref, nw_ref,
                     o_ref, oa_ref, cr_ref, ck_ref, cv_ref, clo_ref, s_ref, y_ref, carry_ref, h_ref, ya_ref):
    C = RWKV_CHUNK
    R = RWKV_ROWS
    n_sub = R // C

    @pl.when(pl.program_id(1) == 0)
    def _():
        for ref in (cr_ref, ck_ref, cv_ref, clo_ref, s_ref, carry_ref, h_ref):
            ref[...] = jnp.zeros_like(ref)

    r = _mix(r_ref[...], cr_ref, mu_r_ref[...])
    k = _mix(k_ref[...], ck_ref, mu_k_ref[...])
    v = _mix(v_ref[...], cv_ref, mu_v_ref[...])
    lo = _mix(lo_ref[...], clo_ref, mu_lo_ref[...])
    wd = lo[:, :RWKV_LORA]
    ad = lo[:, RWKV_LORA:]

    wlog = -_softplus(-(w0_ref[...] + _mm(jnp.tanh(wd), w2_ref[...]))) - 0.5
    lw = -jnp.exp(wlog)
    a = _sigmoid(a0_ref[...] + _mm(ad, a2_ref[...]))

    ex = _head_expand()
    red = _head_reduce()
    kk = k * kk_ref[...]
    nrm = jnp.maximum(jnp.sqrt(_mm_hi(kk * kk, red)), 1e-12)
    kk = kk * _mm_hi(1.0 / nrm, ex)
    kmod = k * (1.0 + (a - 1.0) * ka_ref[...])
    b = kk * a

    ri = lax.broadcasted_iota(I32, (R, R), 0)
    ci = lax.broadcasted_iota(I32, (R, R), 1)
    chunk_tril = jnp.where((ci <= ri) & ((ri // C) == (ci // C)), 1.0, 0.0)
    cum = _mm_hi(chunk_tril, lw, exact="a")
    gam = jnp.exp(cum)
    igam = jnp.exp(-cum)
    at = jnp.exp(cum - lw) * kk
    bt = b * igam
    kt = kmod * igam
    rt = r * gam
    at_b, bt_b, kt_b, rt_b, v_b = (t.astype(BF16) for t in (at, bt, kt, rt, v))

    H = range(N_HEADS)
    P = range(N_HEADS // 2)
    lane = lax.broadcasted_iota(I32, (C, LANES), 1)
    rowi = lax.broadcasted_iota(I32, (C, LANES), 0)
    low = lane < HEAD_DIM
    lane_in = lane & (HEAD_DIM - 1)
    strict2 = lane_in < rowi
    incl2 = lane_in <= rowi
    eye2 = jnp.where(lane_in == rowi, 1.0, 0.0)
    low2 = jnp.concatenate([low, low], axis=0)
    zeros_b = jnp.zeros((C, LANES), BF16)
    blockdiag = (lax.broadcasted_iota(I32, (LANES, LANES), 0) < HEAD_DIM) == (
        lax.broadcasted_iota(I32, (LANES, LANES), 1) < HEAD_DIM)
    nat = lambda h, a, b: jnp.where(low, a, b) if h % 2 == 0 else jnp.where(low, b, a)
    nat2 = lambda h, a, b: jnp.where(low2, a, b) if h % 2 == 0 else jnp.where(low2, b, a)
    rows = lambda top, bot: jnp.concatenate([top, bot], axis=0)
    nat_half = lambda h, t: t[:, :HEAD_DIM] if h % 2 == 0 else t[:, HEAD_DIM:]
    oth_half = lambda h, t: t[:, HEAD_DIM:] if h % 2 == 0 else t[:, :HEAD_DIM]
    oth_rows = lambda h, t: rows(zeros_b, t) if h % 2 == 0 else rows(t, zeros_b)

    def front_all():
        J = [(sub, h) for sub in range(n_sub) for h in H]
        col = lambda t, sub, h: t[sub * C:(sub + 1) * C, (h // 2) * LANES:(h // 2 + 1) * LANES]
        at_c = [col(at_b, sub, h) for sub, h in J]
        bt_c = [col(bt_b, sub, h) for sub, h in J]
        kt_c = [col(kt_b, sub, h) for sub, h in J]
        rt_c = [col(rt_b, sub, h) for sub, h in J]
        rt_f = [col(rt, sub, h) for sub, h in J]
        N = range(len(J))
        hd = [h for _, h in J]
        p1 = [_mm_nt(rows(nat(hd[j], at_c[j], zeros_b), nat(hd[j], rt_c[j], zeros_b)),
                     rows(bt_c[j], kt_c[j]) if hd[j] % 2 == 0 else rows(kt_c[j], bt_c[j])) for j in N]
        top = [jnp.where(strict2, p[:C], 0.0) for p in p1]
        bot = [jnp.where(incl2, p[C:], 0.0) for p in p1]
        tp = [nat(hd[j], -top[j], eye2) for j in N]
        for _ in range(int(math.log2(C))):
            tp_b = [t.astype(BF16) for t in tp]
            out = [_mm(nat_half(hd[j], tp_b[j]), tp_b[j]) for j in N]
            tp = [out[j] + nat(hd[j], 0.0, tp[j]) for j in N]
        tp_b = [t.astype(BF16) for t in tp]
        top_b = [t.astype(BF16) for t in top]
        bot_b = [t.astype(BF16) for t in bot]
        xz = [_mm(oth_half(hd[j], tp_b[j]), nat(hd[j], at_c[j], top_b[j])) for j in N]
        xz_b = [t.astype(BF16) for t in xz]
        mxz = [_mm(nat_half(hd[j], bot_b[j]), xz_b[j]) for j in N]
        ra = [nat(hd[j], rt_f[j], bot[j]) - mxz[j] for j in N]
        flat = [rows(ra[j], xz[j]).astype(BF16) for j in N]
        return [flat[sub * N_HEADS:(sub + 1) * N_HEADS] for sub in range(n_sub)]

    def tail(sub, stack):
        cols = lambda t: [t[sub * C:(sub + 1) * C, p * LANES:(p + 1) * LANES] for p in P]
        bt_c, kt_c, v_c = cols(bt_b), cols(kt_b), cols(v_b)
        g_last = gam[(sub + 1) * C - 1:(sub + 1) * C]
        s_b = [s_ref[p].astype(BF16) for p in P]
        yw = [_mm_nt(nat2(h, stack[h], jnp.zeros_like(stack[h])), s_b[h // 2]) + _mm(stack[h], oth_rows(h, v_c[h // 2]))
              for h in H]
        yw_p = [jnp.where(low2, yw[2 * p], yw[2 * p + 1]) for p in P]
        inc = [_mm_tn(rows(v_c[p], (-yw_p[p][C:]).astype(BF16)), rows(kt_c[p], bt_c[p])) for p in P]
        for p in P:
            y_ref[sub * C:(sub + 1) * C, p * LANES:(p + 1) * LANES] = yw_p[p][:C]
            s_ref[p] = (s_ref[p] + jnp.where(blockdiag, inc[p], 0.0)) * g_last[:, p * LANES:(p + 1) * LANES]

    for sub in range(R // SSD_CHUNK):
        _ssd_chunk(sub * SSD_CHUNK, xbc_ref, z_ref, dt_ref, cw_ref, cb_ref, dtb_ref, alog_ref, dskip_ref, nw_ref,
                   oa_ref, carry_ref, h_ref, ya_ref)
    stacks = front_all()
    for sub in range(n_sub):
        tail(sub, stacks[sub])

    y = y_ref[...]
    inv_n = 1.0 / HEAD_DIM
    mean = _mm_hi(_mm_hi(y, red) * inv_n, ex)
    yc = y - mean
    var = _mm_hi(yc * yc, red) * inv_n
    yn = yc * _mm_hi(lax.rsqrt(var + RWKV_GN_EPS), ex) * lnw_ref[...] + lnb_ref[...]
    bonus = _mm_hi(_mm_hi(r * kmod * rk_ref[...], red), ex) * v
    gate = g_ref[...]
    o_ref[...] = ((yn + bonus) * (gate * _sigmoid(gate))).astype(BF16)


def _rwkv_ssd_branches(proj, B, S, mu, w0, w2, a0, a2, k_k, k_a, r_k, ln_w, ln_b,
                       conv_w, conv_b, dt_bias, a_log, d_skip, norm_w):
    R = RWKV_ROWS
    nc = S // R
    D = BRANCH_D
    row = lambda b, c: b * nc + c
    full = lambda shape: pl.BlockSpec(shape, lambda b, c: (0, 0))
    slab = lambda off, w=D: pl.BlockSpec((R, w), lambda b, c: (row(b, c), off // w))
    mu_r, mu_wd, mu_k, mu_v, mu_ad = (mu[:D], mu[D:D + RWKV_LORA], mu[D + RWKV_LORA:2 * D + RWKV_LORA],
                                      mu[2 * D + RWKV_LORA:3 * D + RWKV_LORA], mu[3 * D + RWKV_LORA:])
    r1 = lambda t: t.reshape(1, -1).astype(F32)
    out = pl.BlockSpec((R, D), lambda b, c: (row(b, c), 0))
    yb, ya = pl.pallas_call(
        _rwkv_ssd_kernel,
        grid=(B, nc),
        in_specs=[slab(OFF_RB), slab(OFF_KB), slab(OFF_VB), slab(OFF_GB), slab(OFF_LORA, LANES),
                  full((1, D)), full((1, D)), full((1, D)), full((1, LANES)),
                  full((1, D)), full((RWKV_LORA, D)), full((1, D)), full((RWKV_LORA, D)),
                  full((1, D)), full((1, D)), full((1, D)), full((1, D)), full((1, D)),
                  slab(OFF_XBC, SSM_CONV_DIM), slab(OFF_Z), slab(OFF_DT, LANES),
                  full((SSM_CONV, SSM_CONV_DIM)), full((1, SSM_CONV_DIM)), full((1, LANES)), full((1, LANES)),
                  full((1, D)), full((1, D))],
        out_specs=[out, out],
        out_shape=[jax.ShapeDtypeStruct((B * S, D), BF16)] * 2,
        scratch_shapes=[pltpu.VMEM((SUBLANES, D), F32), pltpu.VMEM((SUBLANES, D), F32),
                        pltpu.VMEM((SUBLANES, D), F32), pltpu.VMEM((SUBLANES, LANES), F32),
                        pltpu.VMEM((N_HEADS // 2, LANES, LANES), F32),
                        pltpu.VMEM((R, D), F32),
                        pltpu.VMEM((SUBLANES, SSM_CONV_DIM), F32),
                        pltpu.VMEM((SSM_GROUPS, SSM_STATE, D // SSM_GROUPS), F32),
                        pltpu.VMEM((SSD_CHUNK, D), F32)],
        compiler_params=_cparams(("parallel", "arbitrary")),
        name="rwkv_ssd_branches",
    )(proj, proj, proj, proj, proj, r1(mu_r), r1(mu_k), r1(mu_v), r1(jnp.concatenate([mu_wd, mu_ad])),
      r1(w0), w2, r1(a0), a2, r1(k_k), r1(k_a), r1(r_k), r1(ln_w), r1(ln_b),
      proj, proj, proj, conv_w, conv_b[None, :], _pad_lanes(dt_bias), _pad_lanes(a_log),
      jnp.repeat(d_skip, HEAD_DIM)[None, :], norm_w[None, :])
    return ya, yb


def _rope_table_kernel(pos_ref, inv_ref, cos_ref, sin_ref):
    ang = pos_ref[...] * inv_ref[...]
    lane = lax.broadcasted_iota(I32, ang.shape, 1)
    cos_ref[...] = jnp.cos(ang)
    s = jnp.sin(ang)
    sin_ref[...] = jnp.where((lane & (HEAD_DIM - 1)) < HEAD_DIM // 2, -s, s)


def _rope_tables(positions):
    T = positions.size
    tq = min(512, T)
    pos = positions.reshape(T, 1).astype(F32)
    inv = ROPE_THETA ** (-(jnp.arange(HEAD_DIM // 2, dtype=F32) * 2.0 / HEAD_DIM))
    inv = jnp.tile(inv, LANES // (HEAD_DIM // 2))[None, :]
    return pl.pallas_call(
        _rope_table_kernel,
        grid=(T // tq,),
        in_specs=[pl.BlockSpec((tq, 1), lambda i: (i, 0)), pl.BlockSpec((1, LANES), lambda i: (0, 0))],
        out_specs=[pl.BlockSpec((tq, LANES), lambda i: (i, 0))] * 2,
        out_shape=[jax.ShapeDtypeStruct((T, LANES), F32)] * 2,
        compiler_params=_cparams(("parallel",)),
        name="rope_tables",
    )(pos, inv)


def _rope(x, cos, sin_signed):
    lane = lax.broadcasted_iota(I32, (1, LANES), 1)
    first = (lane & (HEAD_DIM - 1)) < HEAD_DIM // 2
    outs = []
    for c in range(x.shape[1] // LANES):
        xb = x[:, c * LANES:(c + 1) * LANES]
        partner = jnp.where(first, pltpu.roll(xb, LANES - HEAD_DIM // 2, 1), pltpu.roll(xb, HEAD_DIM // 2, 1))
        outs.append(xb * cos + partner * sin_signed)
    return outs[0] if len(outs) == 1 else jnp.concatenate(outs, axis=1)


def _dsa_prep_kernel(q_ref, iq_ref, k_ref, v_ref, idx_ref, cos_ref, sin_ref, nw_ref, nb_ref,
                     qt_ref, iqt_ref, ko_ref, vat_ref, iko_ref, iwt_ref):
    QB = DSA_QB
    cos = cos_ref[...]
    sin = sin_ref[...]
    nblk = q_ref.shape[0] // QB
    qt = (_rope(q_ref[...], cos, sin) * (HEAD_DIM ** -0.5 * math.log2(math.e))).T
    iqt = _rope(iq_ref[...], cos, sin).T
    for j in range(nblk):
        qt_ref[j] = qt[:, j * QB:(j + 1) * QB].astype(BF16)
        iqt_ref[j] = iqt[:, j * QB:(j + 1) * QB].astype(BF16)
    ko_ref[...] = _rope(k_ref[...], cos, sin).astype(BF16)
    vt = v_ref[...].T
    ones = jnp.ones((HEAD_DIM, vt.shape[1]), F32)
    vat_ref[0] = jnp.concatenate(
        [t for g in range(ATT_KV_HEADS) for t in (vt[g * HEAD_DIM:(g + 1) * HEAD_DIM], ones)], axis=0).astype(BF16)
    idx = idx_ref[...]
    lane = lax.broadcasted_iota(I32, idx.shape, 1)
    is_k = lane < IDX_DIM
    mu = jnp.sum(jnp.where(is_k, idx, 0.0), axis=-1, keepdims=True) * (1.0 / IDX_DIM)
    dk = jnp.where(is_k, idx - mu, 0.0)
    var = jnp.sum(dk * dk, axis=-1, keepdims=True) * (1.0 / IDX_DIM)
    ikn = dk * lax.rsqrt(var + NORM_EPS) * nw_ref[...] + nb_ref[...]
    iko_ref[...] = _rope(ikn, cos, sin).astype(BF16)
    iwt = (pltpu.roll(idx, LANES - IDX_DIM, 1) * (IDX_HEADS ** -0.5 * IDX_DIM ** -0.5)).T
    for j in range(nblk):
        iwt_ref[j] = iwt[:IDX_HEADS, j * QB:(j + 1) * QB]


def _dsa_prep(proj, cos, sin, k_norm_w, k_norm_b):
    T = proj.shape[0]
    tq = DSA_KT
    QB = DSA_QB
    D = BRANCH_D
    VA = 2 * ATT_KV_D
    nb = tq // QB
    spec = lambda w, off: pl.BlockSpec((tq, w), lambda i: (i, off // w))
    out = lambda w: pl.BlockSpec((tq, w), lambda i: (i, 0))
    blk = lambda n: pl.BlockSpec((nb, n, QB), lambda i: (i, 0, 0))
    return pl.pallas_call(
        _dsa_prep_kernel,
        grid=(T // tq,),
        in_specs=[spec(D, OFF_QC), spec(D, OFF_IQ), spec(ATT_KV_D, OFF_KC), spec(ATT_KV_D, OFF_VC),
                  spec(LANES, OFF_IDX), out(LANES), out(LANES),
                  pl.BlockSpec((1, LANES), lambda i: (0, 0)), pl.BlockSpec((1, LANES), lambda i: (0, 0))],
        out_specs=[blk(D), blk(D), out(ATT_KV_D), pl.BlockSpec((1, VA, tq), lambda i: (i, 0, 0)), out(LANES),
                   blk(IDX_HEADS)],
        out_shape=[jax.ShapeDtypeStruct((T // QB, D, QB), BF16), jax.ShapeDtypeStruct((T // QB, D, QB), BF16),
                   jax.ShapeDtypeStruct((T, ATT_KV_D), BF16), jax.ShapeDtypeStruct((T // tq, VA, tq), BF16),
                   jax.ShapeDtypeStruct((T, LANES), BF16), jax.ShapeDtypeStruct((T // QB, IDX_HEADS, QB), F32)],
        compiler_params=_cparams(("parallel",)),
        name="dsa_prep",
    )(proj, proj, proj, proj, proj, cos, sin, _pad_lanes(k_norm_w), _pad_lanes(k_norm_b))


def _dsa_main_kernel(top_k, seq_len, qt_ref, iqt_ref, iwt_ref, gate_ref, ik_ref, k_ref, vat_ref, o_ref,
                     keys_ref, sacc_ref, bias_ref, m_ref, acc_ref, tie_ref, ot_ref, khi_ref, klo_ref):
    QB, KT = DSA_QB, DSA_KT
    qi = pl.program_id(1)
    nk = (qi * QB + QB + KT - 1) // KT
    int_min = jnp.int32(INT_MIN)
    rep = N_HEADS // ATT_KV_HEADS
    kpos0 = lax.broadcasted_iota(I32, (KT, QB), 0)
    qpos = qi * QB + lax.broadcasted_iota(I32, (KT, QB), 1)

    def score_tile(kt, _):
        off = pl.multiple_of(kt * KT, KT)
        ik_t = ik_ref[pl.ds(off, KT), :][:, :IDX_DIM]
        for h0 in range(0, IDX_HEADS, 4):
            lg = [jnp.dot(ik_t, iqt_ref[h * IDX_DIM:(h + 1) * IDX_DIM, :], preferred_element_type=F32)
                  for h in range(h0, h0 + 4)]
            term = sum(jnp.maximum(l, 0.0) * iwt_ref[h:h + 1, :] for l, h in zip(lg, range(h0, h0 + 4)))
            if h0 == 0:
                sacc_ref[...] = term
            else:
                sacc_ref[...] += term
        bits = pltpu.bitcast(sacc_ref[...], I32)
        skey = jnp.where(bits < 0, bits ^ jnp.int32(0x7FFFFFFF), bits)
        key = jnp.where(kpos0 + off <= qpos, skey, int_min)
        keys_ref[kt] = key
        khi_ref[kt] = (key >> 16).astype(I16)
        return 0

    lax.fori_loop(0, nk, score_tile, 0)

    def count(pred):
        def body(kt, acc):
            hit = pred(keys_ref[kt], kpos0 + kt * KT)
            return acc + _fold_rows(jnp.where(hit, 1.0, 0.0), jnp.add)
        acc = lax.fori_loop(0, nk, body, jnp.zeros((SUBLANES, QB), F32))
        return jnp.sum(acc, axis=0, keepdims=True)

    def count16(ref, cand):
        def body(kt, acc):
            return acc + _fold_rows(jnp.where(ref[kt] >= cand, jnp.int16(1), jnp.int16(0)), jnp.add, 2 * SUBLANES)
        acc = lax.fori_loop(0, nk, body, jnp.zeros((2 * SUBLANES, QB), I16))
        return jnp.sum(acc.astype(I32).astype(F32), axis=0, keepdims=True)

    def search16(ref, need):
        def bit_body(i, res):
            cand_u = res | jnp.left_shift(jnp.int32(1), 15 - i)
            cnt = count16(ref, (cand_u - 32768).astype(I16))
            return jnp.where(cnt >= need, cand_u, res)
        return lax.fori_loop(0, 16, bit_body, jnp.zeros((1, QB), I32))

    hi_s = search16(khi_ref, float(top_k)) - 32768
    hi_16 = hi_s.astype(I16)
    n_above = jnp.where(hi_s >= 32767, 0.0, count16(khi_ref, jnp.minimum(hi_s + 1, 32767).astype(I16)))

    def low_tile(kt, _):
        lo = ((keys_ref[kt] << 16) ^ int_min) >> 16
        klo_ref[kt] = jnp.where(khi_ref[kt] == hi_16, lo.astype(I16), jnp.int16(-32768))
        return 0

    lax.fori_loop(0, nk, low_tile, 0)
    lo_u = search16(klo_ref, float(top_k) - n_above)
    thr = jnp.maximum((hi_s << 16) | lo_u, jnp.int32(INT_MIN + 1))

    n_ge = count(lambda k, c: k >= thr)
    tie_ref[...] = jnp.full((1, QB), seq_len, I32)

    @pl.when(jnp.max(n_ge) > float(top_k))
    def _():
        need = float(top_k) - count(lambda k, c: k > thr)
        n_bits = max(1, (seq_len - 1).bit_length())

        def tie_body(i, x):
            cand = x | jnp.left_shift(jnp.int32(1), n_bits - 1 - i)
            below = count(lambda k, c: (k == thr) & (c < cand))
            return jnp.where(below < need, cand, x)

        tie_ref[...] = lax.fori_loop(0, n_bits, tie_body, jnp.zeros((1, QB), I32))

    tie = tie_ref[...]

    m_ref[...] = jnp.full(m_ref.shape, -1e29, F32)
    acc_ref[...] = jnp.zeros(acc_ref.shape, F32)

    def att_tile(kt, _):
        off = pl.multiple_of(kt * KT, KT)
        kk = keys_ref[kt]
        sel = (kk > thr) | ((kk == thr) & (kpos0 + off <= tie))
        bias_ref[...] = jnp.where(sel, 0.0, -1e30)
        k_t = k_ref[pl.ds(off, KT), :]
        hs = range(N_HEADS)
        k_g = [k_t[:, g * HEAD_DIM:(g + 1) * HEAD_DIM] for g in range(ATT_KV_HEADS)]
        va_g = [vat_ref[kt, g * LANES:(g + 1) * LANES, :] for g in range(ATT_KV_HEADS)]
        m_old = [m_ref[h] for h in hs]
        a_old = [acc_ref[h] for h in hs]
        lg = [jnp.dot(k_g[h // rep], qt_ref[h * HEAD_DIM:(h + 1) * HEAD_DIM, :], preferred_element_type=F32)
              + bias_ref[...] for h in hs]
        m_new = [jnp.maximum(mo, jnp.max(_fold_rows(l, jnp.maximum), axis=0, keepdims=True))
                 for mo, l in zip(m_old, lg)]
        p = [jnp.exp2(l - mn).astype(BF16) for l, mn in zip(lg, m_new)]
        pv = [jnp.dot(va_g[h // rep], p[h], preferred_element_type=F32) for h in hs]
        for h in hs:
            acc_ref[h] = a_old[h] * jnp.exp2(m_old[h] - m_new[h]) + pv[h]
            m_ref[h] = m_new[h]
        return 0

    lax.fori_loop(0, nk, att_tile, 0)

    for h in range(N_HEADS):
        a = acc_ref[h]
        ot_ref[h * HEAD_DIM:(h + 1) * HEAD_DIM, :] = a[:HEAD_DIM] / a[HEAD_DIM:]
    gate = gate_ref[...]
    o_ref[...] = (ot_ref[...].T * (gate * _sigmoid(gate))).astype(BF16)


def _dsa_main(proj, qt, iqt, kr, vat, ikr, iwt, B, S, top_k):
    QB, KT = DSA_QB, DSA_KT
    nq = S // QB
    nkt = S // KT
    D = BRANCH_D
    row = lambda b, i: b * nq + i
    return pl.pallas_call(
        functools.partial(_dsa_main_kernel, top_k, S),
        grid=(B, nq),
        in_specs=[pl.BlockSpec((None, D, QB), lambda b, i: (row(b, i), 0, 0)),
                  pl.BlockSpec((None, D, QB), lambda b, i: (row(b, i), 0, 0)),
                  pl.BlockSpec((None, IDX_HEADS, QB), lambda b, i: (row(b, i), 0, 0)),
                  pl.BlockSpec((QB, D), lambda b, i: (row(b, i), OFF_GC // D)),
                  pl.BlockSpec((S, LANES), lambda b, i: (b, 0)),
                  pl.BlockSpec((S, ATT_KV_D), lambda b, i: (b, 0)),
                  pl.BlockSpec((nkt, 2 * ATT_KV_D, KT), lambda b, i: (b, 0, 0))],
        out_specs=pl.BlockSpec((QB, D), lambda b, i: (row(b, i), 0)),
        out_shape=jax.ShapeDtypeStruct((B * S, D), BF16),
        scratch_shapes=[pltpu.VMEM((nkt, KT, QB), I32),
                        pltpu.VMEM((KT, QB), F32),
                        pltpu.VMEM((KT, QB), F32),
                        pltpu.VMEM((N_HEADS, 1, QB), F32),
                        pltpu.VMEM((N_HEADS, LANES, QB), F32),
                        pltpu.VMEM((1, QB), I32),
                        pltpu.VMEM((D, QB), F32),
                        pltpu.VMEM((nkt, KT, QB), I16),
                        pltpu.VMEM((nkt, KT, QB), I16)],
        compiler_params=_cparams(("parallel", "arbitrary")),
        name="dsa_main",
    )(qt, iqt, iwt, proj, ikr, kr, vat)


def _dsa_branch(proj, cos, sin, B, S, k_norm_w, k_norm_b, top_k):
    qt, iqt, kr, vat, ikr, iwt = _dsa_prep(proj, cos, sin, k_norm_w, k_norm_b)
    return _dsa_main(proj, qt, iqt, kr, vat, ikr, iwt, B, S, top_k)


def _merge_kernel(ya_ref, yb_ref, yc_ref, ga_ref, gb_ref, gc_ref, ba_ref, bb_ref, bc_ref,
                  wa_ref, wb_ref, wc_ref, o_ref):
    acc = _sigmoid(ga_ref[...] + ba_ref[...]) * jnp.dot(ya_ref[...], wa_ref[...], preferred_element_type=F32)
    acc += _sigmoid(gb_ref[...] + bb_ref[...]) * jnp.dot(yb_ref[...], wb_ref[...], preferred_element_type=F32)
    acc += _sigmoid(gc_ref[...] + bc_ref[...]) * jnp.dot(yc_ref[...], wc_ref[...], preferred_element_type=F32)
    o_ref[...] = acc.astype(BF16)


def _merge(ya, yb, yc, proj, b_gate, wa, wb, wc, layer):
    T = ya.shape[0]
    D, Db = D_MODEL, BRANCH_D
    tm = min(256, T)
    y = pl.BlockSpec((tm, Db), lambda i: (i, 0))
    gate = lambda n: pl.BlockSpec((tm, D), lambda i: (i, OFF_GATES // D + n))
    bias = lambda n: pl.BlockSpec((1, D), lambda i: (0, n))
    w = pl.BlockSpec((None, Db, D), lambda i: (layer, 0, 0))
    bg = b_gate[None, :]
    return pl.pallas_call(
        _merge_kernel,
        grid=(T // tm,),
        in_specs=[y, y, y, gate(0), gate(1), gate(2), bias(0), bias(1), bias(2), w, w, w],
        out_specs=pl.BlockSpec((tm, D), lambda i: (i, 0)),
        out_shape=jax.ShapeDtypeStruct((T, D), BF16),
        compiler_params=_cparams(("parallel",)),
        name="gated_merge",
    )(ya, yb, yc, proj, proj, proj, bg, bg, bg, wa, wb, wc)


def _out_proj_kernel(m_ref, w_ref, g_ref, x_ref, o_ref):
    y = jnp.dot(m_ref[...], w_ref[...], preferred_element_type=F32)
    ms = jnp.mean(y * y, axis=-1, keepdims=True)
    o_ref[...] = x_ref[...] + y * lax.rsqrt(ms + NORM_EPS) * g_ref[...]


def _out_proj(merged, w_out, g, x, layer):
    T, D = x.shape
    tm = min(256, T)
    return pl.pallas_call(
        _out_proj_kernel,
        grid=(T // tm,),
        in_specs=[pl.BlockSpec((tm, D), lambda i: (i, 0)), pl.BlockSpec((None, D, D), lambda i: (layer, 0, 0)),
                  pl.BlockSpec((1, D), lambda i: (0, 0)), pl.BlockSpec((tm, D), lambda i: (i, 0))],
        out_specs=pl.BlockSpec((tm, D), lambda i: (i, 0)),
        out_shape=jax.ShapeDtypeStruct((T, D), F32),
        compiler_params=_cparams(("parallel",)),
        name="out_proj_norm_residual",
    )(merged, w_out, g, x)


def _w_in_pieces():
    a0, b0 = 0, A_COLS
    c0 = A_COLS + B_COLS
    g0 = c0 + C_COLS
    D = BRANCH_D
    return [
        (a0 + D, SSM_CONV_DIM, OFF_XBC),
        (a0, D, OFF_Z),
        (b0, D, OFF_RB),
        (b0 + D + RWKV_LORA, D, OFF_KB),
        (b0 + 2 * D + RWKV_LORA, D, OFF_VB),
        (b0 + RWKV_SHIFT_DIM, D, OFF_GB),
        (c0, D, OFF_QC),
        (c0 + D + 2 * ATT_KV_D, D, OFF_GC),
        (c0 + 2 * D + 2 * ATT_KV_D, D, OFF_IQ),
        (g0, GATE_COLS, OFF_GATES),
        (c0 + D, ATT_KV_D, OFF_KC),
        (c0 + D + ATT_KV_D, ATT_KV_D, OFF_VC),
        (a0 + D + SSM_CONV_DIM, N_HEADS, OFF_DT),
        (b0 + D, RWKV_LORA, OFF_LORA),
        (b0 + 3 * D + RWKV_LORA, RWKV_LORA, OFF_LORA + RWKV_LORA),
        (c0 + 3 * D + 2 * ATT_KV_D, IDX_DIM + IDX_HEADS, OFF_IDX),
    ]


def _pack_w_in_kernel(w_ref, o_ref):
    o_ref[...] = jnp.zeros(o_ref.shape, BF16)
    for src, width, dst in _w_in_pieces():
        o_ref[:, dst:dst + width] = w_ref[:, src:src + width].astype(BF16)


def _pack_w_in(w_in):
    L, D, n_in = w_in.shape
    tr = 128
    nr = D // tr
    return pl.pallas_call(
        _pack_w_in_kernel,
        grid=(L, nr),
        in_specs=[pl.BlockSpec((tr, n_in), lambda l, i: (l * nr + i, 0))],
        out_specs=pl.BlockSpec((None, tr, NP_COLS), lambda l, i: (l, i, 0)),
        out_shape=jax.ShapeDtypeStruct((L, D, NP_COLS), BF16),
        compiler_params=_cparams(("parallel", "parallel")),
        name="pack_w_in",
    )(w_in.reshape(L * D, n_in))


def kernel(x, positions, pre_norm, post_norm, w_in, b_gate, ssm_conv_w, ssm_conv_b, ssm_dt_bias, ssm_a_log,
           ssm_d, ssm_norm, rwkv_mu, rwkv_w0, rwkv_w2, rwkv_a0, rwkv_a2, rwkv_k_k, rwkv_k_a, rwkv_r_k,
           rwkv_ln_w, rwkv_ln_b, idx_k_norm_w, idx_k_norm_b, w_branch_a, w_branch_b, w_branch_c, w_out):
    B, S, D = x.shape
    depth = w_in.shape[0]
    top_k = min(TOPK_MAX, S // 4)
    xt = x.reshape(B * S, D)
    w_in_p = _pack_w_in(w_in)
    wa, wb, wc, wo = (w.astype(BF16) for w in (w_branch_a, w_branch_b, w_branch_c, w_out))
    cos, sin = _rope_tables(positions)
    for i in range(depth):
        proj = _norm_matmul(xt, pre_norm[i][None, :], w_in_p, i)
        ya, yb = _rwkv_ssd_branches(proj, B, S, rwkv_mu[i], rwkv_w0[i], rwkv_w2[i], rwkv_a0[i], rwkv_a2[i],
                                    rwkv_k_k[i], rwkv_k_a[i], rwkv_r_k[i], rwkv_ln_w[i], rwkv_ln_b[i],
                                    ssm_conv_w[i], ssm_conv_b[i], ssm_dt_bias[i], ssm_a_log[i], ssm_d[i], ssm_norm[i])
        yc = _dsa_branch(proj, cos, sin, B, S, idx_k_norm_w[i], idx_k_norm_b[i], top_k)
        merged = _merge(ya, yb, yc, proj, b_gate[i], wa, wb, wc, i)
        xt = _out_proj(merged, wo, post_norm[i][None, :], xt, i)
    return xt.reshape(B, S, D)
```

```python
import functools
import math

import jax
import jax.numpy as jnp
from jax import lax
from jax.experimental import pallas as pl
from jax.experimental.pallas import tpu as pltpu

F32 = jnp.float32
BF16 = jnp.bfloat16
I32 = jnp.int32
I16 = jnp.int16

D_MODEL = 2048
BRANCH_D = D_MODEL // 2
NORM_EPS = 1e-6
HEAD_DIM = 64
N_HEADS = BRANCH_D // HEAD_DIM

SSM_GROUPS = 4
SSM_STATE = 128
SSM_CONV = 4
SSD_CHUNK = 128
SSM_CONV_DIM = BRANCH_D + 2 * SSM_GROUPS * SSM_STATE
A_COLS = BRANCH_D + SSM_CONV_DIM + N_HEADS

RWKV_LORA = 64
RWKV_SHIFT_DIM = 3 * BRANCH_D + 2 * RWKV_LORA
B_COLS = RWKV_SHIFT_DIM + BRANCH_D
RWKV_GN_EPS = 64e-5
RWKV_CHUNK = 64
RWKV_ROWS = 256

ATT_KV_HEADS = 4
ATT_KV_D = ATT_KV_HEADS * HEAD_DIM
IDX_HEADS = 16
IDX_DIM = 64
TOPK_MAX = 256
ROPE_THETA = 10000.0
C_COLS = BRANCH_D + 2 * ATT_KV_D + BRANCH_D + IDX_HEADS * IDX_DIM + IDX_DIM + IDX_HEADS
GATE_COLS = 3 * D_MODEL

DSA_QB = 256
DSA_KT = 512

LANES = 128
SUBLANES = 8
VMEM_LIMIT = 56 * 1024 * 1024

OFF_XBC = 0
OFF_Z = 2048
OFF_RB = 3072
OFF_KB = 4096
OFF_VB = 5120
OFF_GB = 6144
OFF_QC = 7168
OFF_GC = 8192
OFF_IQ = 9216
OFF_GATES = 10240
OFF_KC = 16384
OFF_VC = 16640
OFF_DT = 16896
OFF_LORA = 17024
OFF_IDX = 17152
NP_COLS = 17408

INT_MIN = -2 ** 31


def _cparams(sem):
    return pltpu.CompilerParams(dimension_semantics=sem, vmem_limit_bytes=VMEM_LIMIT)


def _mm(a, b):
    return jnp.dot(a.astype(BF16), b.astype(BF16), preferred_element_type=F32)


def _mm_nt(a, b):
    return lax.dot_general(a.astype(BF16), b.astype(BF16), (((1,), (1,)), ((), ())),
                           preferred_element_type=F32)


def _mm_tn(a, b):
    return lax.dot_general(a.astype(BF16), b.astype(BF16), (((0,), (0,)), ((), ())),
                           preferred_element_type=F32)


def _mm_hi(a, b, exact="b"):
    x, m = (a, b) if exact == "b" else (b, a)
    m = m.astype(BF16)
    acc = None
    for _ in range(3):
        xb = x.astype(BF16)
        part = jnp.dot(xb, m, preferred_element_type=F32) if exact == "b" else jnp.dot(m, xb, preferred_element_type=F32)
        acc = part if acc is None else acc + part
        x = x - xb.astype(F32)
    return acc


def _head_expand():
    r = lax.broadcasted_iota(I32, (LANES, BRANCH_D), 0)
    c = lax.broadcasted_iota(I32, (LANES, BRANCH_D), 1)
    return jnp.where((c >> 6) == r, 1.0, 0.0).astype(F32)


def _head_reduce():
    r = lax.broadcasted_iota(I32, (BRANCH_D, LANES), 0)
    c = lax.broadcasted_iota(I32, (BRANCH_D, LANES), 1)
    return jnp.where((r >> 6) == c, 1.0, 0.0).astype(F32)


def _tril(n):
    r = lax.broadcasted_iota(I32, (n, n), 0)
    c = lax.broadcasted_iota(I32, (n, n), 1)
    return c <= r


def _softplus(x):
    return jnp.maximum(x, 0.0) + jnp.log1p(jnp.exp(-jnp.abs(x)))


def _sigmoid(x):
    return 1.0 / (1.0 + jnp.exp(-x))


def _shift_rows(x, carry, s):
    rolled = pltpu.roll(x, s, 0)
    rows = lax.broadcasted_iota(I32, (SUBLANES, x.shape[1]), 0)
    top = jnp.where(rows < s, pltpu.roll(carry, s, 0), rolled[0:SUBLANES])
    return jnp.concatenate([top, rolled[SUBLANES:]], axis=0)


def _fold_rows(x, op, group=SUBLANES):
    parts = x.reshape(x.shape[0] // group, group, x.shape[1])
    while parts.shape[0] > 1:
        half = parts.shape[0] // 2
        parts = op(parts[:half], parts[half:])
    return parts[0]


def _pad_lanes(v, n=LANES):
    return jnp.zeros((1, n), F32).at[0, :v.shape[0]].set(v.astype(F32))


def _norm_matmul_kernel(x_ref, g_ref, w_ref, o_ref, xn_ref):
    @pl.when(pl.program_id(1) == 0)
    def _():
        x = x_ref[...]
        ms = jnp.mean(x * x, axis=-1, keepdims=True)
        xn_ref[...] = (x * lax.rsqrt(ms + NORM_EPS) * g_ref[...]).astype(BF16)

    o_ref[...] = jnp.dot(xn_ref[...], w_ref[...], preferred_element_type=F32)


def _norm_matmul(x, g, w, layer):
    T, D = x.shape
    Np = w.shape[2]
    tm = min(1024, T)
    tn = 1024
    return pl.pallas_call(
        _norm_matmul_kernel,
        grid=(T // tm, Np // tn),
        in_specs=[pl.BlockSpec((tm, D), lambda i, j: (i, 0)),
                  pl.BlockSpec((1, D), lambda i, j: (0, 0)),
                  pl.BlockSpec((None, D, tn), lambda i, j: (layer, 0, j))],
        out_specs=pl.BlockSpec((tm, tn), lambda i, j: (i, j)),
        out_shape=jax.ShapeDtypeStruct((T, Np), F32),
        scratch_shapes=[pltpu.VMEM((tm, D), BF16)],
        compiler_params=_cparams(("parallel", "arbitrary")),
        name="norm_in_proj",
    )(x, g, w)


def _ssd_chunk(r0, xbc_ref, z_ref, dt_ref, cw_ref, cb_ref, dtb_ref, alog_ref, dskip_ref, nw_ref,
               o_ref, carry_ref, h_ref, y_ref):
    Q = SSD_CHUNK
    GN = SSM_GROUPS * SSM_STATE
    GP = BRANCH_D // SSM_GROUPS

    x = xbc_ref[r0:r0 + Q, :]
    carry = carry_ref[...]
    cw = cw_ref[...]
    acc = x * cw[SSM_CONV - 1:SSM_CONV] + cb_ref[...]
    for s in range(1, SSM_CONV):
        acc = acc + _shift_rows(x, carry, s) * cw[SSM_CONV - 1 - s:SSM_CONV - s]
    carry_ref[...] = x[Q - SUBLANES:Q]
    xc = acc * _sigmoid(acc)
    xs = xc[:, :BRANCH_D]
    bm = xc[:, BRANCH_D:BRANCH_D + GN]
    cm = xc[:, BRANCH_D + GN:]

    dt = _softplus(dt_ref[r0:r0 + Q, :] + dtb_ref[...])
    adt = dt * (-jnp.exp(alog_ref[...]))
    a_cs = _mm_hi(_tril(Q).astype(F32), adt, exact="a")
    a_cs_t = a_cs.T
    a_last = a_cs[Q - 1:Q]
    ex = _head_expand()
    dt_e = _mm_hi(dt, ex)
    ecs_e = _mm_hi(jnp.exp(a_cs), ex)
    ds_e = _mm_hi(jnp.exp(a_last - a_cs), ex)
    cd_e = _mm_hi(jnp.exp(a_last), ex)
    xd = xs * dt_e
    xds = xd * ds_e
    causal = _tril(Q)

    G = range(SSM_GROUPS)
    H = range(N_HEADS)
    rep = N_HEADS // SSM_GROUPS
    bm_b, cm_b, xd_b, xds_b = (t.astype(BF16) for t in (bm, cm, xd, xds))
    bm_g = [bm_b[:, g * SSM_STATE:(g + 1) * SSM_STATE] for g in G]
    cm_g = [cm_b[:, g * SSM_STATE:(g + 1) * SSM_STATE] for g in G]
    hg = [h_ref[g] for g in G]
    cb = [_mm_nt(cm_g[g], bm_g[g]) for g in G]
    lmat = [jnp.exp(jnp.where(causal, a_cs[:, h:h + 1] - a_cs_t[h:h + 1, :], -1e30)) for h in H]
    y_diag = [_mm(cb[h // rep] * lmat[h], xd_b[:, h * HEAD_DIM:(h + 1) * HEAD_DIM]) for h in H]
    y_off = [_mm(cm_g[g], hg[g]) * ecs_e[:, g * GP:(g + 1) * GP] for g in G]
    h_new = [hg[g] * cd_e[:, g * GP:(g + 1) * GP] + _mm_tn(bm_g[g], xds_b[:, g * GP:(g + 1) * GP]) for g in G]
    for g in G:
        h_ref[g] = h_new[g]
        y_ref[:, g * GP:(g + 1) * GP] = y_off[g]
    for h in H:
        y_ref[:, h * HEAD_DIM:(h + 1) * HEAD_DIM] += y_diag[h]

    y = y_ref[...] + dskip_ref[...] * xs
    z = z_ref[r0:r0 + Q, :]
    yz = y * (z * _sigmoid(z))
    nw = nw_ref[...]
    for g in range(SSM_GROUPS):
        seg = yz[:, g * GP:(g + 1) * GP]
        ms = jnp.mean(seg * seg, axis=-1, keepdims=True)
        o_ref[r0:r0 + Q, g * GP:(g + 1) * GP] = (seg * lax.rsqrt(ms + NORM_EPS) * nw[:, g * GP:(g + 1) * GP]).astype(BF16)


def _mix(x, carry_ref, mu):
    rolled = pltpu.roll(x, 1, 0)
    rows = lax.broadcasted_iota(I32, x.shape, 0)
    prev = jnp.where(rows == 0, carry_ref[SUBLANES - 1:SUBLANES], rolled)
    carry_ref[...] = x[x.shape[0] - SUBLANES:]
    return x + (prev - x) * mu


def _rwkv_ssd_kernel(r_ref, k_ref, v_ref, g_ref, lo_ref, mu_r_ref, mu_k_ref, mu_v_ref, mu_lo_ref,
                     w0_ref, w2_ref, a0_ref, a2_ref, kk_ref, ka_ref, rk_ref, lnw_ref, lnb_ref,
                     xbc_ref, z_ref, dt_ref, cw_ref, cb_ref, dtb_ref, alog_ref, dskip_ref, nw_ref,
                     o_ref, oa_ref, cr_ref, ck_ref, cv_ref, clo_ref, s_ref, y_ref, carry_ref, h_ref, ya_ref):
    C = RWKV_CHUNK
    R = RWKV_ROWS
    n_sub = R // C

    @pl.when(pl.program_id(1) == 0)
    def _():
        for ref in (cr_ref, ck_ref, cv_ref, clo_ref, s_ref, carry_ref, h_ref):
            ref[...] = jnp.zeros_like(ref)

    r = _mix(r_ref[...], cr_ref, mu_r_ref[...])
    k = _mix(k_ref[...], ck_ref, mu_k_ref[...])
    v = _mix(v_ref[...], cv_ref, mu_v_ref[...])
    lo = _mix(lo_ref[...], clo_ref, mu_lo_ref[...])
    wd = lo[:, :RWKV_LORA]
    ad = lo[:, RWKV_LORA:]

    wlog = -_softplus(-(w0_ref[...] + _mm(jnp.tanh(wd), w2_ref[...]))) - 0.5
    lw = -jnp.exp(wlog)
    a = _sigmoid(a0_ref[...] + _mm(ad, a2_ref[...]))

    ex = _head_expand()
    red = _head_reduce()
    kk = k * kk_ref[...]
    nrm = jnp.maximum(jnp.sqrt(_mm_hi(kk * kk, red)), 1e-12)
    kk = kk * _mm_hi(1.0 / nrm, ex)
    kmod = k * (1.0 + (a - 1.0) * ka_ref[...])
    b = kk * a

    ri = lax.broadcasted_iota(I32, (R, R), 0)
    ci = lax.broadcasted_iota(I32, (R, R), 1)
    chunk_tril = jnp.where((ci <= ri) & ((ri // C) == (ci // C)), 1.0, 0.0)
    cum = _mm_hi(chunk_tril, lw, exact="a")
    gam = jnp.exp(cum)
    igam = jnp.exp(-cum)
    at = jnp.exp(cum - lw) * kk
    bt = b * igam
    kt = kmod * igam
    rt = r * gam
    at_b, bt_b, kt_b, rt_b, v_b = (t.astype(BF16) for t in (at, bt, kt, rt, v))

    H = range(N_HEADS)
    P = range(N_HEADS // 2)
    lane = lax.broadcasted_iota(I32, (C, LANES), 1)
    rowi = lax.broadcasted_iota(I32, (C, LANES), 0)
    low = lane < HEAD_DIM
    lane_in = lane & (HEAD_DIM - 1)
    strict2 = lane_in < rowi
    incl2 = lane_in <= rowi
    eye2 = jnp.where(lane_in == rowi, 1.0, 0.0)
    low2 = jnp.concatenate([low, low], axis=0)
    zeros_b = jnp.zeros((C, LANES), BF16)
    blockdiag = (lax.broadcasted_iota(I32, (LANES, LANES), 0) < HEAD_DIM) == (
        lax.broadcasted_iota(I32, (LANES, LANES), 1) < HEAD_DIM)
    nat = lambda h, a, b: jnp.where(low, a, b) if h % 2 == 0 else jnp.where(low, b, a)
    nat2 = lambda h, a, b: jnp.where(low2, a, b) if h % 2 == 0 else jnp.where(low2, b, a)
    rows = lambda top, bot: jnp.concatenate([top, bot], axis=0)
    nat_half = lambda h, t: t[:, :HEAD_DIM] if h % 2 == 0 else t[:, HEAD_DIM:]
    oth_half = lambda h, t: t[:, HEAD_DIM:] if h % 2 == 0 else t[:, :HEAD_DIM]
    oth_rows = lambda h, t: rows(zeros_b, t) if h % 2 == 0 else rows(t, zeros_b)

    def front_all():
        J = [(sub, h) for sub in range(n_sub) for h in H]
        col = lambda t, sub, h: t[sub * C:(sub + 1) * C, (h // 2) * LANES:(h // 2 + 1) * LANES]
        at_c = [col(at_b, sub, h) for sub, h in J]
        bt_c = [col(bt_b, sub, h) for sub, h in J]
        kt_c = [col(kt_b, sub, h) for sub, h in J]
        rt_c = [col(rt_b, sub, h) for sub, h in J]
        rt_f = [col(rt, sub, h) for sub, h in J]
        N = range(len(J))
        hd = [h for _, h in J]
        p1 = [_mm_nt(rows(nat(hd[j], at_c[j], zeros_b), nat(hd[j], rt_c[j], zeros_b)),
                     rows(bt_c[j], kt_c[j]) if hd[j] % 2 == 0 else rows(kt_c[j], bt_c[j])) for j in N]
        top = [jnp.where(strict2, p[:C], 0.0) for p in p1]
        bot = [jnp.where(incl2, p[C:], 0.0) for p in p1]
        tp = [nat(hd[j], -top[j], eye2) for j in N]
        for _ in range(int(math.log2(C))):
            tp_b = [t.astype(BF16) for t in tp]
            out = [_mm(nat_half(hd[j], tp_b[j]), tp_b[j]) for j in N]
            tp = [out[j] + nat(hd[j], 0.0, tp[j]) for j in N]
        tp_b = [t.astype(BF16) for t in tp]
        top_b = [t.astype(BF16) for t in top]
        bot_b = [t.astype(BF16) for t in bot]
        xz = [_mm(oth_half(hd[j], tp_b[j]), nat(hd[j], at_c[j], top_b[j])) for j in N]
        xz_b = [t.astype(BF16) for t in xz]
        mxz = [_mm(nat_half(hd[j], bot_b[j]), xz_b[j]) for j in N]
        ra = [nat(hd[j], rt_f[j], bot[j]) - mxz[j] for j in N]
        flat = [rows(ra[j], xz[j]).astype(BF16) for j in N]
        return [flat[sub * N_HEADS:(sub + 1) * N_HEADS] for sub in range(n_sub)]

    def tail(sub, stack):
        cols = lambda t: [t[sub * C:(sub + 1) * C, p * LANES:(p + 1) * LANES] for p in P]
        bt_c, kt_c, v_c = cols(bt_b), cols(kt_b), cols(v_b)
        g_last = gam[(sub + 1) * C - 1:(sub + 1) * C]
        s_b = [s_ref[p].astype(BF16) for p in P]
        yw = [_mm_nt(nat2(h, stack[h], jnp.zeros_like(stack[h])), s_b[h // 2]) + _mm(stack[h], oth_rows(h, v_c[h // 2]))
              for h in H]
        yw_p = [jnp.where(low2, yw[2 * p], yw[2 * p + 1]) for p in P]
        inc = [_mm_tn(rows(v_c[p], (-yw_p[p][C:]).astype(BF16)), rows(kt_c[p], bt_c[p])) for p in P]
        for p in P:
            y_ref[sub * C:(sub + 1) * C, p * LANES:(p + 1) * LANES] = yw_p[p][:C]
            s_ref[p] = (s_ref[p] + jnp.where(blockdiag, inc[p], 0.0)) * g_last[:, p * LANES:(p + 1) * LANES]

    for sub in range(R // SSD_CHUNK):
        _ssd_chunk(sub * SSD_CHUNK, xbc_ref, z_ref, dt_ref, cw_ref, cb_ref, dtb_ref, alog_ref, dskip_ref, nw_ref,
                   oa_ref, carry_ref, h_ref, ya_ref)
    stacks = front_all()
    for sub in range(n_sub):
        tail(sub, stacks[sub])

    y = y_ref[...]
    inv_n = 1.0 / HEAD_DIM
    mean = _mm_hi(_mm_hi(y, red) * inv_n, ex)
    yc = y - mean
    var = _mm_hi(yc * yc, red) * inv_n
    yn = yc * _mm_hi(lax.rsqrt(var + RWKV_GN_EPS), ex) * lnw_ref[...] + lnb_ref[...]
    bonus = _mm_hi(_mm_hi(r * kmod * rk_ref[...], red), ex) * v
    gate = g_ref[...]
    o_ref[...] = ((yn + bonus) * (gate * _sigmoid(gate))).astype(BF16)


def _rwkv_ssd_branches(proj, B, S, mu, w0, w2, a0, a2, k_k, k_a, r_k, ln_w, ln_b,
                       conv_w, conv_b, dt_bias, a_log, d_skip, norm_w):
    R = RWKV_ROWS
    nc = S // R
    D = BRANCH_D
    row = lambda b, c: b * nc + c
    full = lambda shape: pl.BlockSpec(shape, lambda b, c: (0, 0))
    slab = lambda off, w=D: pl.BlockSpec((R, w), lambda b, c: (row(b, c), off // w))
    mu_r, mu_wd, mu_k, mu_v, mu_ad = (mu[:D], mu[D:D + RWKV_LORA], mu[D + RWKV_LORA:2 * D + RWKV_LORA],
                                      mu[2 * D + RWKV_LORA:3 * D + RWKV_LORA], mu[3 * D + RWKV_LORA:])
    r1 = lambda t: t.reshape(1, -1).astype(F32)
    out = pl.BlockSpec((R, D), lambda b, c: (row(b, c), 0))
    yb, ya = pl.pallas_call(
        _rwkv_ssd_kernel,
        grid=(B, nc),
        in_specs=[slab(OFF_RB), slab(OFF_KB), slab(OFF_VB), slab(OFF_GB), slab(OFF_LORA, LANES),
                  full((1, D)), full((1, D)), full((1, D)), full((1, LANES)),
                  full((1, D)), full((RWKV_LORA, D)), full((1, D)), full((RWKV_LORA, D)),
                  full((1, D)), full((1, D)), full((1, D)), full((1, D)), full((1, D)),
                  slab(OFF_XBC, SSM_CONV_DIM), slab(OFF_Z), slab(OFF_DT, LANES),
                  full((SSM_CONV, SSM_CONV_DIM)), full((1, SSM_CONV_DIM)), full((1, LANES)), full((1, LANES)),
                  full((1, D)), full((1, D))],
        out_specs=[out, out],
        out_shape=[jax.ShapeDtypeStruct((B * S, D), BF16)] * 2,
        scratch_shapes=[pltpu.VMEM((SUBLANES, D), F32), pltpu.VMEM((SUBLANES, D), F32),
                        pltpu.VMEM((SUBLANES, D), F32), pltpu.VMEM((SUBLANES, LANES), F32),
                        pltpu.VMEM((N_HEADS // 2, LANES, LANES), F32),
                        pltpu.VMEM((R, D), F32),
                        pltpu.VMEM((SUBLANES, SSM_CONV_DIM), F32),
                        pltpu.VMEM((SSM_GROUPS, SSM_STATE, D // SSM_GROUPS), F32),
                        pltpu.VMEM((SSD_CHUNK, D), F32)],
        compiler_params=_cparams(("parallel", "arbitrary")),
        name="rwkv_ssd_branches",
    )(proj, proj, proj, proj, proj, r1(mu_r), r1(mu_k), r1(mu_v), r1(jnp.concatenate([mu_wd, mu_ad])),
      r1(w0), w2, r1(a0), a2, r1(k_k), r1(k_a), r1(r_k), r1(ln_w), r1(ln_b),
      proj, proj, proj, conv_w, conv_b[None, :], _pad_lanes(dt_bias), _pad_lanes(a_log),
      jnp.repeat(d_skip, HEAD_DIM)[None, :], norm_w[None, :])
    return ya, yb


def _rope_table_kernel(pos_ref, inv_ref, cos_ref, sin_ref):
    ang = pos_ref[...] * inv_ref[...]
    lane = lax.broadcasted_iota(I32, ang.shape, 1)
    cos_ref[...] = jnp.cos(ang)
    s = jnp.sin(ang)
    sin_ref[...] = jnp.where((lane & (HEAD_DIM - 1)) < HEAD_DIM // 2, -s, s)


def _rope_tables(positions):
    T = positions.size
    tq = min(512, T)
    pos = positions.reshape(T, 1).astype(F32)
    inv = ROPE_THETA ** (-(jnp.arange(HEAD_DIM // 2, dtype=F32) * 2.0 / HEAD_DIM))
    inv = jnp.tile(inv, LANES // (HEAD_DIM // 2))[None, :]
    return pl.pallas_call(
        _rope_table_kernel,
        grid=(T // tq,),
        in_specs=[pl.BlockSpec((tq, 1), lambda i: (i, 0)), pl.BlockSpec((1, LANES), lambda i: (0, 0))],
        out_specs=[pl.BlockSpec((tq, LANES), lambda i: (i, 0))] * 2,
        out_shape=[jax.ShapeDtypeStruct((T, LANES), F32)] * 2,
        compiler_params=_cparams(("parallel",)),
        name="rope_tables",
    )(pos, inv)


def _rope(x, cos, sin_signed):
    lane = lax.broadcasted_iota(I32, (1, LANES), 1)
    first = (lane & (HEAD_DIM - 1)) < HEAD_DIM // 2
    outs = []
    for c in range(x.shape[1] // LANES):
        xb = x[:, c * LANES:(c + 1) * LANES]
        partner = jnp.where(first, pltpu.roll(xb, LANES - HEAD_DIM // 2, 1), pltpu.roll(xb, HEAD_DIM // 2, 1))
        outs.append(xb * cos + partner * sin_signed)
    return outs[0] if len(outs) == 1 else jnp.concatenate(outs, axis=1)


def _dsa_prep_kernel(q_ref, iq_ref, k_ref, v_ref, idx_ref, cos_ref, sin_ref, nw_ref, nb_ref,
                     qt_ref, iqt_ref, ko_ref, vat_ref, iko_ref, iwt_ref):
    QB = DSA_QB
    cos = cos_ref[...]
    sin = sin_ref[...]
    nblk = q_ref.shape[0] // QB
    qt = (_rope(q_ref[...], cos, sin) * (HEAD_DIM ** -0.5 * math.log2(math.e))).T
    iqt = _rope(iq_ref[...], cos, sin).T
    for j in range(nblk):
        qt_ref[j] = qt[:, j * QB:(j + 1) * QB].astype(BF16)
        iqt_ref[j] = iqt[:, j * QB:(j + 1) * QB].astype(BF16)
    ko_ref[...] = _rope(k_ref[...], cos, sin).astype(BF16)
    vt = v_ref[...].T
    ones = jnp.ones((HEAD_DIM, vt.shape[1]), F32)
    vat_ref[0] = jnp.concatenate(
        [t for g in range(ATT_KV_HEADS) for t in (vt[g * HEAD_DIM:(g + 1) * HEAD_DIM], ones)], axis=0).astype(BF16)
    idx = idx_ref[...]
    lane = lax.broadcasted_iota(I32, idx.shape, 1)
    is_k = lane < IDX_DIM
    mu = jnp.sum(jnp.where(is_k, idx, 0.0), axis=-1, keepdims=True) * (1.0 / IDX_DIM)
    dk = jnp.where(is_k, idx - mu, 0.0)
    var = jnp.sum(dk * dk, axis=-1, keepdims=True) * (1.0 / IDX_DIM)
    ikn = dk * lax.rsqrt(var + NORM_EPS) * nw_ref[...] + nb_ref[...]
    iko_ref[...] = _rope(ikn, cos, sin).astype(BF16)
    iwt = (pltpu.roll(idx, LANES - IDX_DIM, 1) * (IDX_HEADS ** -0.5 * IDX_DIM ** -0.5)).T
    for j in range(nblk):
        iwt_ref[j] = iwt[:IDX_HEADS, j * QB:(j + 1) * QB]


def _dsa_prep(proj, cos, sin, k_norm_w, k_norm_b):
    T = proj.shape[0]
    tq = DSA_KT
    QB = DSA_QB
    D = BRANCH_D
    VA = 2 * ATT_KV_D
    nb = tq // QB
    spec = lambda w, off: pl.BlockSpec((tq, w), lambda i: (i, off // w))
    out = lambda w: pl.BlockSpec((tq, w), lambda i: (i, 0))
    blk = lambda n: pl.BlockSpec((nb, n, QB), lambda i: (i, 0, 0))
    return pl.pallas_call(
        _dsa_prep_kernel,
        grid=(T // tq,),
        in_specs=[spec(D, OFF_QC), spec(D, OFF_IQ), spec(ATT_KV_D, OFF_KC), spec(ATT_KV_D, OFF_VC),
                  spec(LANES, OFF_IDX), out(LANES), out(LANES),
                  pl.BlockSpec((1, LANES), lambda i: (0, 0)), pl.BlockSpec((1, LANES), lambda i: (0, 0))],
        out_specs=[blk(D), blk(D), out(ATT_KV_D), pl.BlockSpec((1, VA, tq), lambda i: (i, 0, 0)), out(LANES),
                   blk(IDX_HEADS)],
        out_shape=[jax.ShapeDtypeStruct((T // QB, D, QB), BF16), jax.ShapeDtypeStruct((T // QB, D, QB), BF16),
                   jax.ShapeDtypeStruct((T, ATT_KV_D), BF16), jax.ShapeDtypeStruct((T // tq, VA, tq), BF16),
                   jax.ShapeDtypeStruct((T, LANES), BF16), jax.ShapeDtypeStruct((T // QB, IDX_HEADS, QB), F32)],
        compiler_params=_cparams(("parallel",)),
        name="dsa_prep",
    )(proj, proj, proj, proj, proj, cos, sin, _pad_lanes(k_norm_w), _pad_lanes(k_norm_b))


def _dsa_main_kernel(top_k, seq_len, qt_ref, iqt_ref, iwt_ref, gate_ref, ik_ref, k_ref, vat_ref, o_ref,
                     keys_ref, sacc_ref, bias_ref, m_ref, acc_ref, tie_ref, ot_ref, khi_ref, klo_ref):
    QB, KT = DSA_QB, DSA_KT
    qi = pl.program_id(1)
    nk = (qi * QB + QB + KT - 1) // KT
    int_min = jnp.int32(INT_MIN)
    rep = N_HEADS // ATT_KV_HEADS
    kpos0 = lax.broadcasted_iota(I32, (KT, QB), 0)
    qpos = qi * QB + lax.broadcasted_iota(I32, (KT, QB), 1)

    def score_tile(kt, _):
        off = pl.multiple_of(kt * KT, KT)
        ik_t = ik_ref[pl.ds(off, KT), :][:, :IDX_DIM]
        for h0 in range(0, IDX_HEADS, 4):
            lg = [jnp.dot(ik_t, iqt_ref[h * IDX_DIM:(h + 1) * IDX_DIM, :], preferred_element_type=F32)
                  for h in range(h0, h0 + 4)]
            term = sum(jnp.maximum(l, 0.0) * iwt_ref[h:h + 1, :] for l, h in zip(lg, range(h0, h0 + 4)))
            if h0 == 0:
                sacc_ref[...] = term
            else:
                sacc_ref[...] += term
        bits = pltpu.bitcast(sacc_ref[...], I32)
        skey = jnp.where(bits < 0, bits ^ jnp.int32(0x7FFFFFFF), bits)
        key = jnp.where(kpos0 + off <= qpos, skey, int_min)
        keys_ref[kt] = key
        khi_ref[kt] = (key >> 16).astype(I16)
        return 0

    lax.fori_loop(0, nk, score_tile, 0)

    def count(pred):
        def body(kt, acc):
            hit = pred(keys_ref[kt], kpos0 + kt * KT)
            return acc + _fold_rows(jnp.where(hit, 1.0, 0.0), jnp.add)
        acc = lax.fori_loop(0, nk, body, jnp.zeros((SUBLANES, QB), F32))
        return jnp.sum(acc, axis=0, keepdims=True)

    def count16(ref, cand):
        def body(kt, acc):
            return acc + _fold_rows(jnp.where(ref[kt] >= cand, jnp.int16(1), jnp.int16(0)), jnp.add, 2 * SUBLANES)
        acc = lax.fori_loop(0, nk, body, jnp.zeros((2 * SUBLANES, QB), I16))
        return jnp.sum(acc.astype(I32).astype(F32), axis=0, keepdims=True)

    def search16(ref, need):
        def bit_body(i, res):
            cand_u = res | jnp.left_shift(jnp.int32(1), 15 - i)
            cnt = count16(ref, (cand_u - 32768).astype(I16))
            return jnp.where(cnt >= need, cand_u, res)
        return lax.fori_loop(0, 16, bit_body, jnp.zeros((1, QB), I32))

    hi_s = search16(khi_ref, float(top_k)) - 32768
    hi_16 = hi_s.astype(I16)
    n_above = jnp.where(hi_s >= 32767, 0.0, count16(khi_ref, jnp.minimum(hi_s + 1, 32767).astype(I16)))

    def low_tile(kt, _):
        lo = ((keys_ref[kt] << 16) ^ int_min) >> 16
        klo_ref[kt] = jnp.where(khi_ref[kt] == hi_16, lo.astype(I16), jnp.int16(-32768))
        return 0

    lax.fori_loop(0, nk, low_tile, 0)
    lo_u = search16(klo_ref, float(top_k) - n_above)
    thr = jnp.maximum((hi_s << 16) | lo_u, jnp.int32(INT_MIN + 1))

    n_ge = count(lambda k, c: k >= thr)
    tie_ref[...] = jnp.full((1, QB), seq_len, I32)

    @pl.when(jnp.max(n_ge) > float(top_k))
    def _():
        need = float(top_k) - count(lambda k, c: k > thr)
        n_bits = max(1, (seq_len - 1).bit_length())

        def tie_body(i, x):
            cand = x | jnp.left_shift(jnp.int32(1), n_bits - 1 - i)
            below = count(lambda k, c: (k == thr) & (c < cand))
            return jnp.where(below < need, cand, x)

        tie_ref[...] = lax.fori_loop(0, n_bits, tie_body, jnp.zeros((1, QB), I32))

    tie = tie_ref[...]

    m_ref[...] = jnp.full(m_ref.shape, -1e29, F32)
    acc_ref[...] = jnp.zeros(acc_ref.shape, F32)

    def att_tile(kt, _):
        off = pl.multiple_of(kt * KT, KT)
        kk = keys_ref[kt]
        sel = (kk > thr) | ((kk == thr) & (kpos0 + off <= tie))
        bias_ref[...] = jnp.where(sel, 0.0, -1e30)
        k_t = k_ref[pl.ds(off, KT), :]
        hs = range(N_HEADS)
        k_g = [k_t[:, g * HEAD_DIM:(g + 1) * HEAD_DIM] for g in range(ATT_KV_HEADS)]
        va_g = [vat_ref[kt, g * LANES:(g + 1) * LANES, :] for g in range(ATT_KV_HEADS)]
        m_old = [m_ref[h] for h in hs]
        a_old = [acc_ref[h] for h in hs]
        lg = [jnp.dot(k_g[h // rep], qt_ref[h * HEAD_DIM:(h + 1) * HEAD_DIM, :], preferred_element_type=F32)
              + bias_ref[...] for h in hs]
        m_new = [jnp.maximum(mo, jnp.max(_fold_rows(l, jnp.maximum), axis=0, keepdims=True))
                 for mo, l in zip(m_old, lg)]
        p = [jnp.exp2(l - mn).astype(BF16) for l, mn in zip(lg, m_new)]
        pv = [jnp.dot(va_g[h // rep], p[h], preferred_element_type=F32) for h in hs]
        for h in hs:
            acc_ref[h] = a_old[h] * jnp.exp2(m_old[h] - m_new[h]) + pv[h]
            m_ref[h] = m_new[h]
        return 0

    lax.fori_loop(0, nk, att_tile, 0)

    for h in range(N_HEADS):
        a = acc_ref[h]
        ot_ref[h * HEAD_DIM:(h + 1) * HEAD_DIM, :] = a[:HEAD_DIM] / a[HEAD_DIM:]
    gate = gate_ref[...]
    o_ref[...] = (ot_ref[...].T * (gate * _sigmoid(gate))).astype(BF16)


def _dsa_main(proj, qt, iqt, kr, vat, ikr, iwt, B, S, top_k):
    QB, KT = DSA_QB, DSA_KT
    nq = S // QB
    nkt = S // KT
    D = BRANCH_D
    row = lambda b, i: b * nq + i
    return pl.pallas_call(
        functools.partial(_dsa_main_kernel, top_k, S),
        grid=(B, nq),
        in_specs=[pl.BlockSpec((None, D, QB), lambda b, i: (row(b, i), 0, 0)),
                  pl.BlockSpec((None, D, QB), lambda b, i: (row(b, i), 0, 0)),
                  pl.BlockSpec((None, IDX_HEADS, QB), lambda b, i: (row(b, i), 0, 0)),
                  pl.BlockSpec((QB, D), lambda b, i: (row(b, i), OFF_GC // D)),
                  pl.BlockSpec((S, LANES), lambda b, i: (b, 0)),
                  pl.BlockSpec((S, ATT_KV_D), lambda b, i: (b, 0)),
                  pl.BlockSpec((nkt, 2 * ATT_KV_D, KT), lambda b, i: (b, 0, 0))],
        out_specs=pl.BlockSpec((QB, D), lambda b, i: (row(b, i), 0)),
        out_shape=jax.ShapeDtypeStruct((B * S, D), BF16),
        scratch_shapes=[pltpu.VMEM((nkt, KT, QB), I32),
                        pltpu.VMEM((KT, QB), F32),
                        pltpu.VMEM((KT, QB), F32),
                        pltpu.VMEM((N_HEADS, 1, QB), F32),
                        pltpu.VMEM((N_HEADS, LANES, QB), F32),
                        pltpu.VMEM((1, QB), I32),
                        pltpu.VMEM((D, QB), F32),
                        pltpu.VMEM((nkt, KT, QB), I16),
                        pltpu.VMEM((nkt, KT, QB), I16)],
        compiler_params=_cparams(("parallel", "arbitrary")),
        name="dsa_main",
    )(qt, iqt, iwt, proj, ikr, kr, vat)


def _dsa_branch(proj, cos, sin, B, S, k_norm_w, k_norm_b, top_k):
    qt, iqt, kr, vat, ikr, iwt = _dsa_prep(proj, cos, sin, k_norm_w, k_norm_b)
    return _dsa_main(proj, qt, iqt, kr, vat, ikr, iwt, B, S, top_k)


def _merge_kernel(ya_ref, yb_ref, yc_ref, ga_ref, gb_ref, gc_ref, ba_ref, bb_ref, bc_ref,
                  wa_ref, wb_ref, wc_ref, o_ref):
    acc = _sigmoid(ga_ref[...] + ba_ref[...]) * jnp.dot(ya_ref[...], wa_ref[...], preferred_element_type=F32)
    acc += _sigmoid(gb_ref[...] + bb_ref[...]) * jnp.dot(yb_ref[...], wb_ref[...], preferred_element_type=F32)
    acc += _sigmoid(gc_ref[...] + bc_ref[...]) * jnp.dot(yc_ref[...], wc_ref[...], preferred_element_type=F32)
    o_ref[...] = acc.astype(BF16)


def _merge(ya, yb, yc, proj, b_gate, wa, wb, wc, layer):
    T = ya.shape[0]
    D, Db = D_MODEL, BRANCH_D
    tm = min(256, T)
    y = pl.BlockSpec((tm, Db), lambda i: (i, 0))
    gate = lambda n: pl.BlockSpec((tm, D), lambda i: (i, OFF_GATES // D + n))
    bias = lambda n: pl.BlockSpec((1, D), lambda i: (0, n))
    w = pl.BlockSpec((None, Db, D), lambda i: (layer, 0, 0))
    bg = b_gate[None, :]
    return pl.pallas_call(
        _merge_kernel,
        grid=(T // tm,),
        in_specs=[y, y, y, gate(0), gate(1), gate(2), bias(0), bias(1), bias(2), w, w, w],
        out_specs=pl.BlockSpec((tm, D), lambda i: (i, 0)),
        out_shape=jax.ShapeDtypeStruct((T, D), BF16),
        compiler_params=_cparams(("parallel",)),
        name="gated_merge",
    )(ya, yb, yc, proj, proj, proj, bg, bg, bg, wa, wb, wc)


def _out_proj_kernel(m_ref, w_ref, g_ref, x_ref, o_ref):
    y = jnp.dot(m_ref[...], w_ref[...], preferred_element_type=F32)
    ms = jnp.mean(y * y, axis=-1, keepdims=True)
    o_ref[...] = x_ref[...] + y * lax.rsqrt(ms + NORM_EPS) * g_ref[...]


def _out_proj(merged, w_out, g, x, layer):
    T, D = x.shape
    tm = min(256, T)
    return pl.pallas_call(
        _out_proj_kernel,
        grid=(T // tm,),
        in_specs=[pl.BlockSpec((tm, D), lambda i: (i, 0)), pl.BlockSpec((None, D, D), lambda i: (layer, 0, 0)),
                  pl.BlockSpec((1, D), lambda i: (0, 0)), pl.BlockSpec((tm, D), lambda i: (i, 0))],
        out_specs=pl.BlockSpec((tm, D), lambda i: (i, 0)),
        out_shape=jax.ShapeDtypeStruct((T, D), F32),
        compiler_params=_cparams(("parallel",)),
        name="out_proj_norm_residual",
    )(merged, w_out, g, x)


def _merge_out_kernel(ya_ref, yb_ref, yc_ref, ga_ref, gb_ref, gc_ref, ba_ref, bb_ref, bc_ref,
                      wa_ref, wb_ref, wc_ref, wo_ref, g_ref, x_ref, o_ref):
    acc = _sigmoid(ga_ref[...] + ba_ref[...]) * jnp.dot(ya_ref[...], wa_ref[...], preferred_element_type=F32)
    acc += _sigmoid(gb_ref[...] + bb_ref[...]) * jnp.dot(yb_ref[...], wb_ref[...], preferred_element_type=F32)
    acc += _sigmoid(gc_ref[...] + bc_ref[...]) * jnp.dot(yc_ref[...], wc_ref[...], preferred_element_type=F32)
    y = jnp.dot(acc.astype(BF16), wo_ref[...], preferred_element_type=F32)
    ms = jnp.mean(y * y, axis=-1, keepdims=True)
    o_ref[...] = x_ref[...] + y * lax.rsqrt(ms + NORM_EPS) * g_ref[...]


def _merge_out(ya, yb, yc, proj, b_gate, wa, wb, wc, w_out, g, x, layer):
    T, D = x.shape
    Db = BRANCH_D
    tm = min(256, T)
    once = pl.Buffered(1)
    y = pl.BlockSpec((tm, Db), lambda i: (i, 0))
    gate = lambda n: pl.BlockSpec((tm, D), lambda i: (i, OFF_GATES // D + n))
    bias = lambda n: pl.BlockSpec((1, D), lambda i: (0, n))
    w = pl.BlockSpec((None, Db, D), lambda i: (layer, 0, 0), pipeline_mode=once)
    bg = b_gate[None, :]
    return pl.pallas_call(
        _merge_out_kernel,
        grid=(T // tm,),
        in_specs=[y, y, y, gate(0), gate(1), gate(2), bias(0), bias(1), bias(2), w, w, w,
                  pl.BlockSpec((None, D, D), lambda i: (layer, 0, 0), pipeline_mode=once),
                  pl.BlockSpec((1, D), lambda i: (0, 0)), pl.BlockSpec((tm, D), lambda i: (i, 0))],
        out_specs=pl.BlockSpec((tm, D), lambda i: (i, 0)),
        out_shape=jax.ShapeDtypeStruct((T, D), F32),
        compiler_params=_cparams(("parallel",)),
        name="merge_out_proj",
    )(ya, yb, yc, proj, proj, proj, bg, bg, bg, wa, wb, wc, w_out, g, x)


def _w_in_pieces():
    a0, b0 = 0, A_COLS
    c0 = A_COLS + B_COLS
    g0 = c0 + C_COLS
    D = BRANCH_D
    return [
        (a0 + D, SSM_CONV_DIM, OFF_XBC),
        (a0, D, OFF_Z),
        (b0, D, OFF_RB),
        (b0 + D + RWKV_LORA, D, OFF_KB),
        (b0 + 2 * D + RWKV_LORA, D, OFF_VB),
        (b0 + RWKV_SHIFT_DIM, D, OFF_GB),
        (c0, D, OFF_QC),
        (c0 + D + 2 * ATT_KV_D, D, OFF_GC),
        (c0 + 2 * D + 2 * ATT_KV_D, D, OFF_IQ),
        (g0, GATE_COLS, OFF_GATES),
        (c0 + D, ATT_KV_D, OFF_KC),
        (c0 + D + ATT_KV_D, ATT_KV_D, OFF_VC),
        (a0 + D + SSM_CONV_DIM, N_HEADS, OFF_DT),
        (b0 + D, RWKV_LORA, OFF_LORA),
        (b0 + 3 * D + RWKV_LORA, RWKV_LORA, OFF_LORA + RWKV_LORA),
        (c0 + 3 * D + 2 * ATT_KV_D, IDX_DIM + IDX_HEADS, OFF_IDX),
    ]


def _pack_w_in_kernel(w_ref, o_ref):
    o_ref[...] = jnp.zeros(o_ref.shape, BF16)
    for src, width, dst in _w_in_pieces():
        o_ref[:, dst:dst + width] = w_ref[:, src:src + width].astype(BF16)


def _pack_w_in(w_in):
    L, D, n_in = w_in.shape
    tr = 128
    nr = D // tr
    return pl.pallas_call(
        _pack_w_in_kernel,
        grid=(L, nr),
        in_specs=[pl.BlockSpec((tr, n_in), lambda l, i: (l * nr + i, 0))],
        out_specs=pl.BlockSpec((None, tr, NP_COLS), lambda l, i: (l, i, 0)),
        out_shape=jax.ShapeDtypeStruct((L, D, NP_COLS), BF16),
        compiler_params=_cparams(("parallel", "parallel")),
        name="pack_w_in",
    )(w_in.reshape(L * D, n_in))


def kernel(x, positions, pre_norm, post_norm, w_in, b_gate, ssm_conv_w, ssm_conv_b, ssm_dt_bias, ssm_a_log,
           ssm_d, ssm_norm, rwkv_mu, rwkv_w0, rwkv_w2, rwkv_a0, rwkv_a2, rwkv_k_k, rwkv_k_a, rwkv_r_k,
           rwkv_ln_w, rwkv_ln_b, idx_k_norm_w, idx_k_norm_b, w_branch_a, w_branch_b, w_branch_c, w_out):
    B, S, D = x.shape
    depth = w_in.shape[0]
    top_k = min(TOPK_MAX, S // 4)
    xt = x.reshape(B * S, D)
    w_in_p = _pack_w_in(w_in)
    wa, wb, wc, wo = (w.astype(BF16) for w in (w_branch_a, w_branch_b, w_branch_c, w_out))
    cos, sin = _rope_tables(positions)
    for i in range(depth):
        proj = _norm_matmul(xt, pre_norm[i][None, :], w_in_p, i)
        ya, yb = _rwkv_ssd_branches(proj, B, S, rwkv_mu[i], rwkv_w0[i], rwkv_w2[i], rwkv_a0[i], rwkv_a2[i],
                                    rwkv_k_k[i], rwkv_k_a[i], rwkv_r_k[i], rwkv_ln_w[i], rwkv_ln_b[i],
                                    ssm_conv_w[i], ssm_conv_b[i], ssm_dt_bias[i], ssm_a_log[i], ssm_d[i], ssm_norm[i])
        yc = _dsa_branch(proj, cos, sin, B, S, idx_k_norm_w[i], idx_k_norm_b[i], top_k)
        xt = _merge_out(ya, yb, yc, proj, b_gate[i], wa, wb, wc, wo, post_norm[i][None, :], xt, i)
    return xt.reshape(B, S, D)
```
